```python
import math
import jax, jax.numpy as jnp
from jax import lax
import numpy as np

D_MODEL = 2048
BATCH = 4
SEQ = 4096
DEPTH = 4

HEAD_DIM = 128
N_FOX_HEADS = 8
N_SB_HEADS = 8
FOX_W = N_FOX_HEADS * HEAD_DIM
SB_W = N_SB_HEADS * HEAD_DIM
MIX_W = FOX_W + SB_W
ATT_IN_COLS = 3 * FOX_W + N_FOX_HEADS + 3 * SB_W
ATT_SPLITS = [FOX_W, 2 * FOX_W, 3 * FOX_W, 3 * FOX_W + N_FOX_HEADS,
              3 * FOX_W + N_FOX_HEADS + SB_W, 3 * FOX_W + N_FOX_HEADS + 2 * SB_W]
Q_BLOCK = 128
CONV_WIDTH = 3
D_FF = 5632
N_EXPERTS = 8
TOP_K = 2
D_FF_EXPERT = 2816
N_EVEN = (DEPTH + 1) // 2
N_ODD = DEPTH // 2
RMS_EPS = 1e-6

kernel_name = "fox_stickbreak_shortconv_moe_hybrid"


def rmsnorm(x, g):
    xf = x.astype(jnp.float32)
    inv = lax.rsqrt(jnp.mean(xf * xf, axis=-1, keepdims=True) + RMS_EPS)
    return (xf * inv).astype(x.dtype) * g


def split_heads(t, n_heads):
    b, s, _ = t.shape
    return t.reshape(b, s, n_heads, HEAD_DIM).transpose(0, 2, 1, 3)


def merge_blocks(out):
    nb, b, h, qb, dh = out.shape
    return out.transpose(1, 2, 0, 3, 4).reshape(b, h, nb * qb, dh)


def fox_attention(q, k, v, log_f):
    seq = q.shape[2]
    scale = 1.0 / math.sqrt(HEAD_DIM)
    c = jnp.cumsum(log_f, axis=-1)
    key_pos = jnp.arange(seq)

    def one_block(i):
        start = i * Q_BLOCK
        qb = lax.dynamic_slice_in_dim(q, start, Q_BLOCK, axis=2)
        cb = lax.dynamic_slice_in_dim(c, start, Q_BLOCK, axis=2)
        q_pos = start + jnp.arange(Q_BLOCK)
        s = jnp.einsum('bhqd,bhkd->bhqk', qb, k).astype(jnp.float32) * scale
        s = s + cb[..., :, None] - c[..., None, :]
        causal = key_pos[None, :] <= q_pos[:, None]
        s = jnp.where(causal, s, -jnp.inf)
        p = jax.nn.softmax(s, axis=-1)
        return jnp.einsum('bhqk,bhkd->bhqd', p.astype(v.dtype), v)

    return merge_blocks(lax.map(one_block, jnp.arange(seq // Q_BLOCK)))


def stick_breaking_attention(q, k, v):
    seq = q.shape[2]
    scale = 1.0 / math.sqrt(HEAD_DIM)
    key_pos = jnp.arange(seq)

    def one_block(i):
        start = i * Q_BLOCK
        qb = lax.dynamic_slice_in_dim(q, start, Q_BLOCK, axis=2)
        q_pos = start + jnp.arange(Q_BLOCK)
        z = jnp.einsum('bhqd,bhkd->bhqk', qb, k).astype(jnp.float32) * scale
        strict = key_pos[None, :] < q_pos[:, None]
        log_beta = jax.nn.log_sigmoid(z)
        log_one_minus = jnp.where(strict, jax.nn.log_sigmoid(-z), 0.0)
        rev_incl = lax.cumsum(log_one_minus, axis=3, reverse=True)
        rev_excl = jnp.concatenate([rev_incl[..., 1:], jnp.zeros_like(rev_incl[..., :1])], axis=-1)
        log_a = jnp.where(strict, log_beta + rev_excl, -jnp.inf)
        a = jnp.exp(log_a)
        return jnp.einsum('bhqk,bhkd->bhqd', a.astype(v.dtype), v)

    return merge_blocks(lax.map(one_block, jnp.arange(seq // Q_BLOCK)))


def attention_mixer(h, w_in, b_forget, w_out):
    b, s, _ = h.shape
    proj = h @ w_in
    qa, ka, va, fa, qb, kb, vb = jnp.split(proj, ATT_SPLITS, axis=-1)
    log_f = jax.nn.log_sigmoid((fa + b_forget).astype(jnp.float32)).transpose(0, 2, 1)
    oa = fox_attention(split_heads(qa, N_FOX_HEADS), split_heads(ka, N_FOX_HEADS),
                       split_heads(va, N_FOX_HEADS), log_f)
    ob = stick_breaking_attention(split_heads(qb, N_SB_HEADS), split_heads(kb, N_SB_HEADS),
                                  split_heads(vb, N_SB_HEADS))
    o = jnp.concatenate([oa, ob], axis=1)
    o = o.transpose(0, 2, 1, 3).reshape(b, s, MIX_W)
    return o @ w_out


def short_conv_mixer(h, w_in, conv_w, w_out):
    seq = h.shape[1]
    gate_b, gate_c, u = jnp.split(h @ w_in, 3, axis=-1)
    g = gate_c * u
    gp = jnp.pad(g, ((0, 0), (CONV_WIDTH - 1, 0), (0, 0)))
    conv = sum(gp[:, j:j + seq, :] * conv_w[j] for j in range(CONV_WIDTH))
    return (gate_b * conv) @ w_out


def swiglu(h, w_gate, w_up, w_down):
    return (jax.nn.silu(h @ w_gate) * (h @ w_up)) @ w_down


def moe_swiglu(h, w_router, w_gate, w_up, w_down):
    b, s, d = h.shape
    ht = h.reshape(b * s, d)
    logits = (ht @ w_router).astype(jnp.float32)
    top_vals, top_idx = lax.top_k(logits, TOP_K)
    gates = jax.nn.softmax(top_vals, axis=-1)
    combine = jnp.sum(jax.nn.one_hot(top_idx, N_EXPERTS, dtype=jnp.float32) * gates[..., None], axis=1)
    combine = combine.astype(h.dtype)
    out = jnp.zeros_like(ht)
    for e in range(N_EXPERTS):
        out = out + combine[:, e:e + 1] * swiglu(ht, w_gate[e], w_up[e], w_down[e])
    return out.reshape(b, s, d)


def setup_inputs(seed: int = 0) -> dict:
    key = jax.random.key(seed)
    ks = jax.random.split(key, 20)
    D = D_MODEL
    nrm = lambda k, shape, fan_in: jax.random.normal(k, shape, jnp.float32) * (fan_in ** -0.5)
    return {
        "x": jax.random.normal(ks[0], (BATCH, SEQ, D), jnp.float32),
        "mix_norm": 1.0 + 0.02 * jax.random.normal(ks[1], (DEPTH, D), jnp.float32),
        "ffn_norm": 1.0 + 0.02 * jax.random.normal(ks[2], (DEPTH, D), jnp.float32),
        "final_norm": 1.0 + 0.02 * jax.random.normal(ks[3], (D,), jnp.float32),
        "w_in_att": nrm(ks[4], (N_EVEN, D, ATT_IN_COLS), D),
        "b_forget": jax.random.uniform(ks[5], (N_EVEN, N_FOX_HEADS), jnp.float32, 1.0, 4.0),
        "w_out_att": nrm(ks[6], (N_EVEN, MIX_W, D), MIX_W),
        "w_in_conv": nrm(ks[7], (N_ODD, D, 3 * D), D),
        "conv_w": nrm(ks[8], (N_ODD, CONV_WIDTH, D), CONV_WIDTH),
        "w_out_conv": nrm(ks[9], (N_ODD, D, D), D),
        "w_gate_dense": nrm(ks[10], (N_EVEN, D, D_FF), D),
        "w_up_dense": nrm(ks[11], (N_EVEN, D, D_FF), D),
        "w_down_dense": nrm(ks[12], (N_EVEN, D_FF, D), D_FF),
        "w_router": nrm(ks[13], (N_ODD, D, N_EXPERTS), D),
        "w_gate_moe": nrm(ks[14], (N_ODD, N_EXPERTS, D, D_FF_EXPERT), D),
        "w_up_moe": nrm(ks[15], (N_ODD, N_EXPERTS, D, D_FF_EXPERT), D),
        "w_down_moe": nrm(ks[16], (N_ODD, N_EXPERTS, D_FF_EXPERT, D), D_FF_EXPERT),
    }


def reference(x, mix_norm, ffn_norm, final_norm, w_in_att, b_forget, w_out_att,
              w_in_conv, conv_w, w_out_conv, w_gate_dense, w_up_dense, w_down_dense,
              w_router, w_gate_moe, w_up_moe, w_down_moe):
    for i in range(DEPTH):
        j = i // 2
        h = rmsnorm(x, mix_norm[i])
        if i % 2 == 0:
            x = x + attention_mixer(h, w_in_att[j], b_forget[j], w_out_att[j])
        else:
            x = x + short_conv_mixer(h, w_in_conv[j], conv_w[j], w_out_conv[j])
        h = rmsnorm(x, ffn_norm[i])
        if i % 2 == 0:
            x = x + swiglu(h, w_gate_dense[j], w_up_dense[j], w_down_dense[j])
        else:
            x = x + moe_swiglu(h, w_router[j], w_gate_moe[j], w_up_moe[j], w_down_moe[j])
    return rmsnorm(x, final_norm)
```

```python
import functools
import math

import jax
import jax.numpy as jnp
from jax import lax
from jax.experimental import pallas as pl
from jax.experimental.pallas import tpu as pltpu

F32 = jnp.float32
BF16 = jnp.bfloat16
I32 = jnp.int32
U32 = jnp.uint32

HEAD_DIM = 128
N_HEADS = 8
HEADS_W = N_HEADS * HEAD_DIM
N_EXPERTS = 8
RMS_EPS = 1e-6
CONV_WIDTH = 3
QK_SCALE = 1.0 / math.sqrt(HEAD_DIM)

V7X_VMEM_LIMIT_BYTES = 56 * 1024 * 1024
LANE = 128
BF16_SUBLANE_TILE = 16
NEG_INF = float("-inf")

_NT = (((1,), (1,)), ((), ()))


def _cp(*sem):
    return pltpu.CompilerParams(dimension_semantics=sem, vmem_limit_bytes=V7X_VMEM_LIMIT_BYTES)


def _tile(n, pref, unit=LANE):
    if n <= pref:
        return n
    t = (pref // unit) * unit
    while t > unit and n % t:
        t -= unit
    assert n % t == 0, (n, pref)
    return t


def _rms(x, g):
    ms = jnp.mean(x * x, axis=-1, keepdims=True)
    return x * lax.rsqrt(ms + RMS_EPS) * g


def _split_bf16(v):
    hi = v.astype(BF16)
    lo = (v - hi.astype(F32)).astype(BF16)
    return hi, lo


def _log_sigmoid_pair(z):
    sp = jnp.log1p(jnp.exp(-jnp.abs(z)))
    return jnp.minimum(z, 0.0) - sp, -jnp.maximum(z, 0.0) - sp


def _silu(g):
    return g / (1.0 + jnp.exp(-g))


def _skinny_nt(wh_ref, wl_ref, hb, hl):
    wh = wh_ref[...]
    out = lax.dot_general(wh, hb, _NT, preferred_element_type=F32)
    out += lax.dot_general(wh, hl, _NT, preferred_element_type=F32)
    out += lax.dot_general(wl_ref[...], hb, _NT, preferred_element_type=F32)
    return out


def _att_in_body(x_ref, g_ref, w_ref, wfh_ref, wfl_ref, o_ref, f_ref, h_scr, *, q_blocks):
    j = pl.program_id(1)

    @pl.when(j == 0)
    def _():
        h = _rms(x_ref[...], g_ref[...])
        hb, hl = _split_bf16(h)
        h_scr[...] = hb
        f_ref[...] = _skinny_nt(wfh_ref, wfl_ref, hb, hl)

    acc = jnp.dot(h_scr[...], w_ref[...], preferred_element_type=F32)
    is_q = (j < q_blocks) | ((j >= 3 * q_blocks) & (j < 4 * q_blocks))
    o_ref[...] = (acc * jnp.where(is_q, QK_SCALE, 1.0)).astype(BF16)


def _att_in_proj(x2, gnorm, w_big, wf_hi, wf_lo):
    T, D = x2.shape
    N = w_big.shape[1]
    tm = _tile(T, 1024)
    tn = _tile(HEADS_W, 512)
    return pl.pallas_call(
        functools.partial(_att_in_body, q_blocks=HEADS_W // tn),
        grid=(T // tm, N // tn),
        in_specs=[
            pl.BlockSpec((tm, D), lambda i, j: (i, 0)),
            pl.BlockSpec((1, D), lambda i, j: (0, 0)),
            pl.BlockSpec((D, tn), lambda i, j: (0, j)),
            pl.BlockSpec((BF16_SUBLANE_TILE, D), lambda i, j: (0, 0)),
            pl.BlockSpec((BF16_SUBLANE_TILE, D), lambda i, j: (0, 0)),
        ],
        out_specs=[
            pl.BlockSpec((tm, tn), lambda i, j: (i, j)),
            pl.BlockSpec((BF16_SUBLANE_TILE, tm), lambda i, j: (0, i)),
        ],
        out_shape=[
            jax.ShapeDtypeStruct((T, N), BF16),
            jax.ShapeDtypeStruct((BF16_SUBLANE_TILE, T), F32),
        ],
        scratch_shapes=[pltpu.VMEM((tm, D), BF16)],
        compiler_params=_cp("parallel", "arbitrary"),
        name="att_in_proj",
    )(x2, gnorm, w_big, wf_hi, wf_lo)


def _forget_cumsum_body(f_ref, b_ref, c_ref):
    z = f_ref[...] + b_ref[...]
    lf, _ = _log_sigmoid_pair(z)
    S = lf.shape[1]
    lane = lax.broadcasted_iota(I32, lf.shape, 1)
    c = lf
    sh = 1
    while sh < S:
        c = c + jnp.where(lane >= sh, pltpu.roll(c, sh, axis=1), 0.0)
        sh *= 2
    c_ref[0] = c


def _forget_cumsum(f16, b_col, B, S):
    return pl.pallas_call(
        _forget_cumsum_body,
        grid=(B,),
        in_specs=[
            pl.BlockSpec((N_HEADS, S), lambda b: (0, b)),
            pl.BlockSpec((N_HEADS, 1), lambda b: (0, 0)),
        ],
        out_specs=pl.BlockSpec((1, N_HEADS, S), lambda b: (b, 0, 0)),
        out_shape=jax.ShapeDtypeStruct((B, N_HEADS, S), F32),
        compiler_params=_cp("parallel"),
        name="forget_cumsum",
    )(f16, b_col)


def _fox_body(q_ref, k_ref, v_ref, c_ref, o_ref, *, tq):
    i = pl.program_id(2)
    q = q_ref[...]
    c_q = c_ref[0, :, pl.ds(pl.multiple_of(i * tq, tq), tq)]
    c0 = jnp.max(c_q, axis=1, keepdims=True)
    row = lax.broadcasted_iota(I32, (tq, tq), 0)
    col = lax.broadcasted_iota(I32, (tq, tq), 1)

    def tile(kt, carry, masked):
        m, l, acc = carry
        ks = pl.multiple_of(kt * tq, tq)
        kb = k_ref[pl.ds(ks, tq), :]
        vb = v_ref[pl.ds(ks, tq), :]
        c_k = c_ref[0, :, pl.ds(ks, tq)]
        s = lax.dot_general(q, kb, _NT, preferred_element_type=F32) + (c0 - c_k)
        if masked:
            s = jnp.where(col <= row, s, NEG_INF)
        m_new = jnp.maximum(m, jnp.max(s, axis=1, keepdims=True))
        alpha = jnp.exp(m - m_new)
        p = jnp.exp(s - m_new)
        l = alpha * l + jnp.sum(p, axis=1, keepdims=True)
        acc = alpha * acc + jnp.dot(p.astype(BF16), vb, preferred_element_type=F32)
        return m_new, l, acc

    init = (jnp.full((tq, 1), NEG_INF, F32), jnp.zeros((tq, 1), F32), jnp.zeros((tq, HEAD_DIM), F32))
    carry = lax.fori_loop(0, i, lambda kt, c: tile(kt, c, False), init)
    _, l, acc = tile(i, carry, True)
    o_ref[...] = (acc / l).astype(BF16)


def _fox_attention(proj, c3, B, S):
    T = B * S
    tq = _tile(S, 256)
    nq = S // tq
    return pl.pallas_call(
        functools.partial(_fox_body, tq=tq),
        grid=(B, N_HEADS, nq),
        in_specs=[
            pl.BlockSpec((tq, HEAD_DIM), lambda b, h, i: (b * nq + i, h)),
            pl.BlockSpec((S, HEAD_DIM), lambda b, h, i: (b, N_HEADS + h)),
            pl.BlockSpec((S, HEAD_DIM), lambda b, h, i: (b, 2 * N_HEADS + h)),
            pl.BlockSpec((1, 1, S), lambda b, h, i: (b * N_HEADS + h, 0, 0)),
        ],
        out_specs=pl.BlockSpec((tq, HEAD_DIM), lambda b, h, i: (b * nq + i, h)),
        out_shape=jax.ShapeDtypeStruct((T, HEADS_W), BF16),
        compiler_params=_cp("parallel", "parallel", "arbitrary"),
        name="fox_attention",
    )(proj, proj, proj, c3)


def _sb_body(q_ref, k_ref, v_ref, u_ref, o_ref, *, tq):
    i = pl.program_id(2)
    q = q_ref[...]
    u = u_ref[...]
    row = lax.broadcasted_iota(I32, (tq, tq), 0)
    col = lax.broadcasted_iota(I32, (tq, tq), 1)
    strict = col < row

    def tile(kt, carry, masked):
        r_sum, acc = carry
        ks = pl.multiple_of(kt * tq, tq)
        kb = k_ref[pl.ds(ks, tq), :]
        vb = v_ref[pl.ds(ks, tq), :]
        z = lax.dot_general(q, kb, _NT, preferred_element_type=F32)
        log_beta, log_om = _log_sigmoid_pair(z)
        if masked:
            log_om = jnp.where(strict, log_om, 0.0)
        hi, lo = _split_bf16(log_om)
        e = jnp.dot(hi, u, preferred_element_type=F32) + jnp.dot(lo, u, preferred_element_type=F32)
        a = jnp.exp(log_beta + e + r_sum)
        if masked:
            a = jnp.where(strict, a, 0.0)
        acc = acc + jnp.dot(a.astype(BF16), vb, preferred_element_type=F32)
        r_sum = r_sum + jnp.sum(log_om, axis=1, keepdims=True)
        return r_sum, acc

    carry = tile(i, (jnp.zeros((tq, 1), F32), jnp.zeros((tq, HEAD_DIM), F32)), True)
    _, acc = lax.fori_loop(0, i, lambda n, c: tile(i - 1 - n, c, False), carry)
    o_ref[...] = acc.astype(BF16)


def _sb_attention(proj, B, S):
    T = B * S
    tq = _tile(S, 256)
    nq = S // tq
    r = lax.broadcasted_iota(I32, (tq, tq), 0)
    c = lax.broadcasted_iota(I32, (tq, tq), 1)
    u = (r > c).astype(BF16)
    return pl.pallas_call(
        functools.partial(_sb_body, tq=tq),
        grid=(B, N_HEADS, nq),
        in_specs=[
            pl.BlockSpec((tq, HEAD_DIM), lambda b, h, i: (b * nq + i, 3 * N_HEADS + h)),
            pl.BlockSpec((S, HEAD_DIM), lambda b, h, i: (b, 4 * N_HEADS + h)),
            pl.BlockSpec((S, HEAD_DIM), lambda b, h, i: (b, 5 * N_HEADS + h)),
            pl.BlockSpec((tq, tq), lambda b, h, i: (0, 0)),
        ],
        out_specs=pl.BlockSpec((tq, HEAD_DIM), lambda b, h, i: (b * nq + i, h)),
        out_shape=jax.ShapeDtypeStruct((T, HEADS_W), BF16),
        compiler_params=_cp("parallel", "parallel", "arbitrary"),
        name="sb_attention",
    )(proj, proj, proj, u)


def _att_out_body(oa_ref, ob_ref, wa_ref, wb_ref, x_ref, o_ref):
    acc = jnp.dot(oa_ref[...], wa_ref[...], preferred_element_type=F32)
    acc += jnp.dot(ob_ref[...], wb_ref[...], preferred_element_type=F32)
    o_ref[...] = x_ref[...] + acc


def _att_out_proj(oa, ob, w_out, x2):
    T, D = x2.shape
    tm = _tile(T, 1024)
    tn = _tile(D, 512)
    return pl.pallas_call(
        _att_out_body,
        grid=(T // tm, D // tn),
        in_specs=[
            pl.BlockSpec((tm, HEADS_W), lambda i, j: (i, 0)),
            pl.BlockSpec((tm, HEADS_W), lambda i, j: (i, 0)),
            pl.BlockSpec((HEADS_W, tn), lambda i, j: (0, j)),
            pl.BlockSpec((HEADS_W, tn), lambda i, j: (1, j)),
            pl.BlockSpec((tm, tn), lambda i, j: (i, j)),
        ],
        out_specs=pl.BlockSpec((tm, tn), lambda i, j: (i, j)),
        out_shape=jax.ShapeDtypeStruct((T, D), F32),
        compiler_params=_cp("parallel", "parallel"),
        name="att_out_proj",
    )(oa, ob, w_out, w_out, x2)


def _ffn_up_body(x_ref, g_ref, wg_ref, wu_ref, a_ref, h_scr):
    @pl.when(pl.program_id(1) == 0)
    def _():
        h_scr[...] = _rms(x_ref[...], g_ref[...]).astype(BF16)

    h = h_scr[...]
    g = jnp.dot(h, wg_ref[...], preferred_element_type=F32)
    u = jnp.dot(h, wu_ref[...], preferred_element_type=F32)
    a_ref[...] = (_silu(g) * u).astype(BF16)


def _ffn_up(x2, gnorm, wg, wu):
    T, D = x2.shape
    F = wg.shape[1]
    tm = _tile(T, 1024)
    tn = _tile(F, 512)
    return pl.pallas_call(
        _ffn_up_body,
        grid=(T // tm, F // tn),
        in_specs=[
            pl.BlockSpec((tm, D), lambda i, j: (i, 0)),
            pl.BlockSpec((1, D), lambda i, j: (0, 0)),
            pl.BlockSpec((D, tn), lambda i, j: (0, j)),
            pl.BlockSpec((D, tn), lambda i, j: (0, j)),
        ],
        out_specs=pl.BlockSpec((tm, tn), lambda i, j: (i, j)),
        out_shape=jax.ShapeDtypeStruct((T, F), BF16),
        scratch_shapes=[pltpu.VMEM((tm, D), BF16)],
        compiler_params=_cp("parallel", "arbitrary"),
        name="ffn_up",
    )(x2, gnorm, wg, wu)


def _ffn_down_body(a_ref, w_ref, x_ref, o_ref):
    o_ref[...] = x_ref[...] + jnp.dot(a_ref[...], w_ref[...], preferred_element_type=F32)


def _ffn_down(a, wd, x2):
    T, D = x2.shape
    F = a.shape[1]
    tm = _tile(T, 1024)
    tn = _tile(D, 256)
    return pl.pallas_call(
        _ffn_down_body,
        grid=(T // tm, D // tn),
        in_specs=[
            pl.BlockSpec((tm, F), lambda i, j: (i, 0)),
            pl.BlockSpec((F, tn), lambda i, j: (0, j)),
            pl.BlockSpec((tm, tn), lambda i, j: (i, j)),
        ],
        out_specs=pl.BlockSpec((tm, tn), lambda i, j: (i, j)),
        out_shape=jax.ShapeDtypeStruct((T, D), F32),
        compiler_params=_cp("parallel", "parallel"),
        name="ffn_down",
    )(a, wd, x2)


def _conv_in_body(x_ref, g_ref, wb_ref, wc_ref, wu_ref, gb_ref, gg_ref, h_scr):
    @pl.when(pl.program_id(1) == 0)
    def _():
        h_scr[...] = _rms(x_ref[...], g_ref[...]).astype(BF16)

    h = h_scr[...]
    gb_ref[...] = jnp.dot(h, wb_ref[...], preferred_element_type=F32).astype(BF16)
    c = jnp.dot(h, wc_ref[...], preferred_element_type=F32)
    u = jnp.dot(h, wu_ref[...], preferred_element_type=F32)
    gg_ref[...] = (c * u).astype(BF16)


def _conv_in(x2, gnorm, w_in):
    T, D = x2.shape
    tm = _tile(T, 1024)
    tn = _tile(D, 512)
    nd = D // tn
    return pl.pallas_call(
        _conv_in_body,
        grid=(T // tm, nd),
        in_specs=[
            pl.BlockSpec((tm, D), lambda i, j: (i, 0)),
            pl.BlockSpec((1, D), lambda i, j: (0, 0)),
            pl.BlockSpec((D, tn), lambda i, j: (0, j)),
            pl.BlockSpec((D, tn), lambda i, j: (0, nd + j)),
            pl.BlockSpec((D, tn), lambda i, j: (0, 2 * nd + j)),
        ],
        out_specs=[
            pl.BlockSpec((tm, tn), lambda i, j: (i, j)),
            pl.BlockSpec((tm, tn), lambda i, j: (i, j)),
        ],
        out_shape=[jax.ShapeDtypeStruct((T, D), BF16), jax.ShapeDtypeStruct((T, D), BF16)],
        scratch_shapes=[pltpu.VMEM((tm, D), BF16)],
        compiler_params=_cp("parallel", "arbitrary"),
        name="conv_in",
    )(x2, gnorm, w_in, w_in, w_in)


def _conv_out_body(gb_ref, g_ref, gp_ref, cw_ref, w_ref, x_ref, o_ref, *, tiles_per_seq):
    i = pl.program_id(0)
    g = g_ref[...].astype(F32)
    tm = g.shape[0]
    keep = jnp.where(i % tiles_per_seq == 0, 0.0, 1.0)
    prev = gp_ref[...].astype(F32) * keep
    p1 = prev[BF16_SUBLANE_TILE - 1:BF16_SUBLANE_TILE, :]
    p2 = prev[BF16_SUBLANE_TILE - 2:BF16_SUBLANE_TILE - 1, :]
    row = lax.broadcasted_iota(I32, g.shape, 0)
    g1 = jnp.where(row == 0, p1, pltpu.roll(g, 1, axis=0))
    g2 = jnp.where(row == 0, p2, jnp.where(row == 1, p1, pltpu.roll(g, 2, axis=0)))
    cw = cw_ref[...]
    conv = g2 * cw[0:1, :] + g1 * cw[1:2, :] + g * cw[2:3, :]
    y = (gb_ref[...].astype(F32) * conv).astype(BF16)
    o_ref[...] = x_ref[...] + jnp.dot(y, w_ref[...], preferred_element_type=F32)


def _conv_out(gb, gg, conv_w, w_out, x2, S):
    T, D = x2.shape
    tm = _tile(S, 256)
    pt = BF16_SUBLANE_TILE
    return pl.pallas_call(
        functools.partial(_conv_out_body, tiles_per_seq=S // tm),
        grid=(T // tm,),
        in_specs=[
            pl.BlockSpec((tm, D), lambda i: (i, 0)),
            pl.BlockSpec((tm, D), lambda i: (i, 0)),
            pl.BlockSpec((pt, D), lambda i: (jnp.maximum(i * (tm // pt) - 1, 0), 0)),
            pl.BlockSpec((CONV_WIDTH, D), lambda i: (0, 0)),
            pl.BlockSpec((D, D), lambda i: (0, 0)),
            pl.BlockSpec((tm, D), lambda i: (i, 0)),
        ],
        out_specs=pl.BlockSpec((tm, D), lambda i: (i, 0)),
        out_shape=jax.ShapeDtypeStruct((T, D), F32),
        compiler_params=_cp("parallel"),
        name="conv_out",
    )(gb, gg, gg, conv_w, w_out, x2)


def _router_body(x_ref, g_ref, wrh_ref, wrl_ref, tri_ref, hp_ref, mi_ref, mf_ref, cnt_ref, carry):
    @pl.when(pl.program_id(0) == 0)
    def _():
        carry[...] = jnp.zeros_like(carry)

    h = _rms(x_ref[...], g_ref[...])
    hb, hl = _split_bf16(h)
    bits = pltpu.bitcast(hb.astype(F32), U32)
    half = bits.shape[1] // 2
    hp_ref[...] = (bits[:, :half] >> 16) | bits[:, half:]

    logits = _skinny_nt(wrh_ref, wrl_ref, hb, hl)[:N_EXPERTS]
    eidx = lax.broadcasted_iota(I32, logits.shape, 0).astype(F32)
    ne = float(N_EXPERTS)
    m1 = jnp.max(logits, axis=0, keepdims=True)
    i1 = jnp.min(jnp.where(logits == m1, eidx, ne), axis=0, keepdims=True)
    rest = jnp.where(eidx == i1, NEG_INF, logits)
    m2 = jnp.max(rest, axis=0, keepdims=True)
    i2 = jnp.min(jnp.where(rest == m2, eidx, ne), axis=0, keepdims=True)
    e21 = jnp.exp(m2 - m1)
    g1 = 1.0 / (1.0 + e21)
    g2 = e21 * g1

    sel = jnp.where((eidx == i1) | (eidx == i2), 1.0, 0.0)
    incl = jnp.dot(sel, tri_ref[...], preferred_element_type=F32)
    pos = carry[...] + incl - sel
    carry[...] = carry[...] + jnp.sum(sel, axis=1, keepdims=True)
    p1 = jnp.sum(jnp.where(eidx == i1, pos, 0.0), axis=0, keepdims=True)
    p2 = jnp.sum(jnp.where(eidx == i2, pos, 0.0), axis=0, keepdims=True)
    meta = jnp.where(eidx == 0, i1, jnp.where(eidx == 1, i2, jnp.where(eidx == 2, p1, jnp.where(eidx == 3, p2, 0.0))))
    mi_ref[...] = meta.astype(I32)
    mf_ref[...] = jnp.where(eidx == 0, g1, jnp.where(eidx == 1, g2, 0.0))
    cnt_ref[...] = jnp.broadcast_to(carry[...], cnt_ref.shape)


def _router(x2, gnorm, wr_hi, wr_lo):
    T, D = x2.shape
    tm = _tile(T, 512)
    r = lax.broadcasted_iota(I32, (tm, tm), 0)
    c = lax.broadcasted_iota(I32, (tm, tm), 1)
    tri = (r <= c).astype(F32)
    return pl.pallas_call(
        _router_body,
        grid=(T // tm,),
        in_specs=[
            pl.BlockSpec((tm, D), lambda i: (i, 0)),
            pl.BlockSpec((1, D), lambda i: (0, 0)),
            pl.BlockSpec((BF16_SUBLANE_TILE, D), lambda i: (0, 0)),
            pl.BlockSpec((BF16_SUBLANE_TILE, D), lambda i: (0, 0)),
            pl.BlockSpec((tm, tm), lambda i: (0, 0)),
        ],
        out_specs=[
            pl.BlockSpec((tm, D // 2), lambda i: (i, 0)),
            pl.BlockSpec((N_EXPERTS, tm), lambda i: (0, i)),
            pl.BlockSpec((N_EXPERTS, tm), lambda i: (0, i)),
            pl.BlockSpec((N_EXPERTS, LANE), lambda i: (0, 0)),
        ],
        out_shape=[
            jax.ShapeDtypeStruct((T, D // 2), U32),
            jax.ShapeDtypeStruct((N_EXPERTS, T), I32),
            jax.ShapeDtypeStruct((N_EXPERTS, T), F32),
            jax.ShapeDtypeStruct((N_EXPERTS, LANE), F32),
        ],
        scratch_shapes=[pltpu.VMEM((N_EXPERTS, 1), F32)],
        compiler_params=_cp("arbitrary"),
        name="moe_router",
    )(x2, gnorm, wr_hi, wr_lo, tri)


def _dispatch_body(d1_ref, d2_ref, hp_ref, zero_ref, hs_ref, sem):
    del zero_ref
    n = hp_ref.shape[0]

    def issue(r, carry):
        src = hp_ref.at[pl.ds(r, 1)]
        pltpu.make_async_copy(src, hs_ref.at[pl.ds(d1_ref[r], 1)], sem).start()
        pltpu.make_async_copy(src, hs_ref.at[pl.ds(d2_ref[r], 1)], sem).start()
        return carry

    def drain(r, carry):
        row = pltpu.make_async_copy(hp_ref.at[pl.ds(0, 1)], hs_ref.at[pl.ds(0, 1)], sem)
        row.wait()
        row.wait()
        return carry

    lax.fori_loop(0, n, issue, 0)
    lax.fori_loop(0, n, drain, 0)


def _dispatch(hp, d1, d2, P):
    T, Dh = hp.shape
    tm = _tile(T, 256)
    zeros = jnp.zeros((P, Dh), U32)
    return pl.pallas_call(
        _dispatch_body,
        grid=(T // tm,),
        in_specs=[
            pl.BlockSpec((tm,), lambda i: (i,), memory_space=pltpu.SMEM),
            pl.BlockSpec((tm,), lambda i: (i,), memory_space=pltpu.SMEM),
            pl.BlockSpec((tm, Dh), lambda i: (i, 0)),
            pl.BlockSpec(memory_space=pl.ANY),
        ],
        out_specs=pl.BlockSpec(memory_space=pl.ANY),
        out_shape=jax.ShapeDtypeStruct((P, Dh), U32),
        scratch_shapes=[pltpu.SemaphoreType.DMA(())],
        input_output_aliases={3: 0},
        compiler_params=_cp("arbitrary"),
        name="moe_dispatch",
    )(d1, d2, hp, zeros)


def _unpack_rows(words):
    lo = pltpu.bitcast(words << 16, F32)
    hi = pltpu.bitcast(words & jnp.uint32(0xFFFF0000), F32)
    return jnp.concatenate([lo, hi], axis=1).astype(BF16)


def _expert_up_body(te_ref, hs_ref, wg_ref, wu_ref, a_ref):
    del te_ref
    h = _unpack_rows(hs_ref[...])
    g = jnp.dot(h, wg_ref[...], preferred_element_type=F32)
    u = jnp.dot(h, wu_ref[...], preferred_element_type=F32)
    a_ref[...] = (_silu(g) * u).astype(BF16)


def _expert_up(tile_expert, hs, wg, wu, tme):
    P, Dh = hs.shape
    _, D, F = wg.shape
    tn = F // 2 if (F // 2) % LANE == 0 else F
    n_tiles = P // tme
    return pl.pallas_call(
        _expert_up_body,
        grid_spec=pltpu.PrefetchScalarGridSpec(
            num_scalar_prefetch=1,
            grid=(F // tn, n_tiles),
            in_specs=[
                pl.BlockSpec((tme, Dh), lambda j, i, te: (i, 0)),
                pl.BlockSpec((None, D, tn), lambda j, i, te: (te[i], 0, j)),
                pl.BlockSpec((None, D, tn), lambda j, i, te: (te[i], 0, j)),
            ],
            out_specs=pl.BlockSpec((tme, tn), lambda j, i, te: (i, j)),
        ),
        out_shape=jax.ShapeDtypeStruct((P, F), BF16),
        compiler_params=_cp("arbitrary", "arbitrary"),
        name="moe_expert_up",
    )(tile_expert, hs, wg, wu)


def _expert_down_body(te_ref, a_ref, w_ref, y_ref):
    del te_ref
    y_ref[...] = jnp.dot(a_ref[...], w_ref[...], preferred_element_type=F32)


def _expert_down(tile_expert, act, wd, tme):
    P, F = act.shape
    D = wd.shape[2]
    return pl.pallas_call(
        _expert_down_body,
        grid_spec=pltpu.PrefetchScalarGridSpec(
            num_scalar_prefetch=1,
            grid=(P // tme,),
            in_specs=[
                pl.BlockSpec((tme, F), lambda i, te: (i, 0)),
                pl.BlockSpec((None, F, D), lambda i, te: (te[i], 0, 0)),
            ],
            out_specs=pl.BlockSpec((tme, D), lambda i, te: (i, 0)),
        ),
        out_shape=jax.ShapeDtypeStruct((P, D), F32),
        compiler_params=_cp("arbitrary"),
        name="moe_expert_down",
    )(tile_expert, act, wd)


def _combine_body(d1_ref, d2_ref, mf_ref, x_ref, fn_ref, y_ref, o_ref, ya, yb, sem, *, final_norm):
    n = x_ref.shape[0]

    def issue(r, carry):
        pltpu.make_async_copy(y_ref.at[pl.ds(d1_ref[r], 1)], ya.at[pl.ds(r, 1)], sem).start()
        pltpu.make_async_copy(y_ref.at[pl.ds(d2_ref[r], 1)], yb.at[pl.ds(r, 1)], sem).start()
        return carry

    def drain(r, carry):
        row = pltpu.make_async_copy(y_ref.at[pl.ds(0, 1)], ya.at[pl.ds(0, 1)], sem)
        row.wait()
        row.wait()
        return carry

    lax.fori_loop(0, n, issue, 0)
    lax.fori_loop(0, n, drain, 0)

    gates = mf_ref[...]
    eye = lax.broadcasted_iota(I32, (n, n), 0) == lax.broadcasted_iota(I32, (n, n), 1)
    g1 = jnp.sum(jnp.where(eye, gates[0:1, :], 0.0), axis=1, keepdims=True)
    g2 = jnp.sum(jnp.where(eye, gates[1:2, :], 0.0), axis=1, keepdims=True)
    out = x_ref[...] + (g1 * ya[...] + g2 * yb[...])
    if final_norm:
        out = _rms(out, fn_ref[...])
    o_ref[...] = out


def _combine(d1, d2, mf, x2, fnorm, y, final_norm):
    T, D = x2.shape
    tm = _tile(T, 256)
    return pl.pallas_call(
        functools.partial(_combine_body, final_norm=final_norm),
        grid=(T // tm,),
        in_specs=[
            pl.BlockSpec((tm,), lambda i: (i,), memory_space=pltpu.SMEM),
            pl.BlockSpec((tm,), lambda i: (i,), memory_space=pltpu.SMEM),
            pl.BlockSpec((N_EXPERTS, tm), lambda i: (0, i)),
            pl.BlockSpec((tm, D), lambda i: (i, 0)),
            pl.BlockSpec((1, D), lambda i: (0, 0)),
            pl.BlockSpec(memory_space=pl.ANY),
        ],
        out_specs=pl.BlockSpec((tm, D), lambda i: (i, 0)),
        out_shape=jax.ShapeDtypeStruct((T, D), F32),
        scratch_shapes=[pltpu.VMEM((tm, D), F32), pltpu.VMEM((tm, D), F32), pltpu.SemaphoreType.DMA(())],
        compiler_params=_cp("arbitrary"),
        name="moe_combine",
    )(d1, d2, mf, x2, fnorm, y)


def _pad_rows16(w_t):
    pad = jnp.zeros((BF16_SUBLANE_TILE - w_t.shape[0], w_t.shape[1]), F32)
    w = jnp.concatenate([w_t, pad], axis=0)
    hi = w.astype(BF16)
    lo = (w - hi.astype(F32)).astype(BF16)
    return hi, lo


def _attention_layer(x2, B, S, gnorm, w_in, b_forget, w_out):
    w_big = jnp.concatenate([w_in[:, :3 * HEADS_W], w_in[:, 3 * HEADS_W + N_HEADS:]], axis=1).astype(BF16)
    wf_hi, wf_lo = _pad_rows16(w_in[:, 3 * HEADS_W:3 * HEADS_W + N_HEADS].T)
    proj, f16 = _att_in_proj(x2, gnorm, w_big, wf_hi, wf_lo)
    c = _forget_cumsum(f16, b_forget.reshape(N_HEADS, 1), B, S)
    oa = _fox_attention(proj, c.reshape(B * N_HEADS, 1, S), B, S)
    ob = _sb_attention(proj, B, S)
    return _att_out_proj(oa, ob, w_out.astype(BF16), x2)


def _dense_ffn_layer(x2, gnorm, wg, wu, wd):
    a = _ffn_up(x2, gnorm, wg.astype(BF16), wu.astype(BF16))
    return _ffn_down(a, wd.astype(BF16), x2)


def _conv_layer(x2, S, gnorm, w_in, conv_w, w_out):
    gb, gg = _conv_in(x2, gnorm, w_in.astype(BF16))
    return _conv_out(gb, gg, conv_w, w_out.astype(BF16), x2, S)


def _moe_layer(x2, gnorm, w_router, wg, wu, wd, fnorm, final_norm):
    T, D = x2.shape
    tme = 512 if T >= 4096 else 128
    n_tiles = (2 * T) // tme + N_EXPERTS
    P = n_tiles * tme

    wr_hi, wr_lo = _pad_rows16(w_router.T)
    hp, mi, mf, cnt = _router(x2, gnorm, wr_hi, wr_lo)

    counts = cnt[:, 0].astype(I32)
    padded = ((counts + tme - 1) // tme) * tme
    ends = jnp.cumsum(padded)
    offs = ends - padded
    d1 = jnp.take(offs, mi[0]) + mi[2]
    d2 = jnp.take(offs, mi[1]) + mi[3]
    tile_start = jnp.arange(n_tiles, dtype=I32) * tme
    tile_expert = jnp.minimum(jnp.searchsorted(ends, tile_start, side="right"), N_EXPERTS - 1).astype(I32)

    hs = _dispatch(hp, d1, d2, P)
    act = _expert_up(tile_expert, hs, wg.astype(BF16), wu.astype(BF16), tme)
    y = _expert_down(tile_expert, act, wd.astype(BF16), tme)
    return _combine(d1, d2, mf, x2, fnorm, y, final_norm)


def kernel(x, mix_norm, ffn_norm, final_norm, w_in_att, b_forget, w_out_att, w_in_conv, conv_w,
           w_out_conv, w_gate_dense, w_up_dense, w_down_dense, w_router, w_gate_moe, w_up_moe,
           w_down_moe):
    B, S, D = x.shape
    depth = mix_norm.shape[0]
    assert depth % 2 == 0, "the final rmsnorm is fused into the last (routed) layer"
    x2 = x.reshape(B * S, D)
    fnorm = final_norm.reshape(1, D)
    for i in range(depth):
        j = i // 2
        mg = mix_norm[i].reshape(1, D)
        fg = ffn_norm[i].reshape(1, D)
        if i % 2 == 0:
            x2 = _attention_layer(x2, B, S, mg, w_in_att[j], b_forget[j], w_out_att[j])
            x2 = _dense_ffn_layer(x2, fg, w_gate_dense[j], w_up_dense[j], w_down_dense[j])
        else:
            x2 = _conv_layer(x2, S, mg, w_in_conv[j], conv_w[j], w_out_conv[j])
            x2 = _moe_layer(x2, fg, w_router[j], w_gate_moe[j], w_up_moe[j], w_down_moe[j],
                            fnorm, final_norm=(i == depth - 1))
    return x2.reshape(B, S, D)
```

```python
import functools
import math

import jax
import jax.numpy as jnp
from jax import lax
from jax.experimental import pallas as pl
from jax.experimental.pallas import tpu as pltpu

F32 = jnp.float32
BF16 = jnp.bfloat16
I32 = jnp.int32
U32 = jnp.uint32

HEAD_DIM = 128
N_HEADS = 8
HEADS_W = N_HEADS * HEAD_DIM
N_EXPERTS = 8
RMS_EPS = 1e-6
CONV_WIDTH = 3
QK_SCALE = 1.0 / math.sqrt(HEAD_DIM)

V7X_VMEM_LIMIT_BYTES = 56 * 1024 * 1024
LANE = 128
BF16_SUBLANE_TILE = 16
NEG_INF = float("-inf")

_NT = (((1,), (1,)), ((), ()))


def _cp(*sem):
    return pltpu.CompilerParams(dimension_semantics=sem, vmem_limit_bytes=V7X_VMEM_LIMIT_BYTES)


def _tile(n, pref, unit=LANE):
    if n <= pref:
        return n
    t = (pref // unit) * unit
    while t > unit and n % t:
        t -= unit
    assert n % t == 0, (n, pref)
    return t


def _rms(x, g):
    ms = jnp.mean(x * x, axis=-1, keepdims=True)
    return x * lax.rsqrt(ms + RMS_EPS) * g


def _split_bf16(v):
    hi = v.astype(BF16)
    lo = (v - hi.astype(F32)).astype(BF16)
    return hi, lo


def _log_sigmoid_pair(z):
    sp = jnp.log1p(jnp.exp(-jnp.abs(z)))
    return jnp.minimum(z, 0.0) - sp, -jnp.maximum(z, 0.0) - sp


def _silu(g):
    return g / (1.0 + jnp.exp(-g))


def _skinny_nt(wh_ref, wl_ref, hb, hl):
    wh = wh_ref[...]
    out = lax.dot_general(wh, hb, _NT, preferred_element_type=F32)
    out += lax.dot_general(wh, hl, _NT, preferred_element_type=F32)
    out += lax.dot_general(wl_ref[...], hb, _NT, preferred_element_type=F32)
    return out


def _att_in_body(x_ref, g_ref, w_ref, wfh_ref, wfl_ref, o_ref, f_ref, h_scr, *, q_blocks):
    j = pl.program_id(1)

    @pl.when(j == 0)
    def _():
        h = _rms(x_ref[...], g_ref[...])
        hb, hl = _split_bf16(h)
        h_scr[...] = hb
        f_ref[...] = _skinny_nt(wfh_ref, wfl_ref, hb, hl)

    acc = jnp.dot(h_scr[...], w_ref[...], preferred_element_type=F32)
    is_q = (j < q_blocks) | ((j >= 3 * q_blocks) & (j < 4 * q_blocks))
    o_ref[...] = (acc * jnp.where(is_q, QK_SCALE, 1.0)).astype(BF16)


def _att_in_proj(x2, gnorm, w_big, wf_hi, wf_lo):
    T, D = x2.shape
    N = w_big.shape[1]
    tm = _tile(T, 1024)
    tn = _tile(HEADS_W, 512)
    return pl.pallas_call(
        functools.partial(_att_in_body, q_blocks=HEADS_W // tn),
        grid=(T // tm, N // tn),
        in_specs=[
            pl.BlockSpec((tm, D), lambda i, j: (i, 0)),
            pl.BlockSpec((1, D), lambda i, j: (0, 0)),
            pl.BlockSpec((D, tn), lambda i, j: (0, j)),
            pl.BlockSpec((BF16_SUBLANE_TILE, D), lambda i, j: (0, 0)),
            pl.BlockSpec((BF16_SUBLANE_TILE, D), lambda i, j: (0, 0)),
        ],
        out_specs=[
            pl.BlockSpec((tm, tn), lambda i, j: (i, j)),
            pl.BlockSpec((BF16_SUBLANE_TILE, tm), lambda i, j: (0, i)),
        ],
        out_shape=[
            jax.ShapeDtypeStruct((T, N), BF16),
            jax.ShapeDtypeStruct((BF16_SUBLANE_TILE, T), F32),
        ],
        scratch_shapes=[pltpu.VMEM((tm, D), BF16)],
        compiler_params=_cp("parallel", "arbitrary"),
        name="att_in_proj",
    )(x2, gnorm, w_big, wf_hi, wf_lo)


def _forget_cumsum_body(f_ref, b_ref, c_ref):
    z = f_ref[...] + b_ref[...]
    lf, _ = _log_sigmoid_pair(z)
    S = lf.shape[1]
    lane = lax.broadcasted_iota(I32, lf.shape, 1)
    c = lf
    sh = 1
    while sh < S:
        c = c + jnp.where(lane >= sh, pltpu.roll(c, sh, axis=1), 0.0)
        sh *= 2
    c_ref[0] = c


def _forget_cumsum(f16, b_col, B, S):
    return pl.pallas_call(
        _forget_cumsum_body,
        grid=(B,),
        in_specs=[
            pl.BlockSpec((N_HEADS, S), lambda b: (0, b)),
            pl.BlockSpec((N_HEADS, 1), lambda b: (0, 0)),
        ],
        out_specs=pl.BlockSpec((1, N_HEADS, S), lambda b: (b, 0, 0)),
        out_shape=jax.ShapeDtypeStruct((B, N_HEADS, S), F32),
        compiler_params=_cp("parallel"),
        name="forget_cumsum",
    )(f16, b_col)


def _head_cols(g):
    return slice(g * HEAD_DIM, (g + 1) * HEAD_DIM)


def _fox_body(q_ref, k_ref, v_ref, c_ref, o_ref, *, tq, heads):
    i = pl.program_id(2)
    qs = [q_ref[:, _head_cols(g)] for g in range(heads)]
    q0 = pl.multiple_of(i * tq, tq)
    c0 = [jnp.max(c_ref[g, :, pl.ds(q0, tq)], axis=1, keepdims=True) for g in range(heads)]
    row = lax.broadcasted_iota(I32, (tq, tq), 0)
    col = lax.broadcasted_iota(I32, (tq, tq), 1)

    def tile(kt, carry, masked):
        ks = pl.multiple_of(kt * tq, tq)
        out = []
        for g in range(heads):
            m, l, acc = carry[g]
            kb = k_ref[pl.ds(ks, tq), _head_cols(g)]
            vb = v_ref[pl.ds(ks, tq), _head_cols(g)]
            c_k = c_ref[g, :, pl.ds(ks, tq)]
            s = lax.dot_general(qs[g], kb, _NT, preferred_element_type=F32) + (c0[g] - c_k)
            if masked:
                s = jnp.where(col <= row, s, NEG_INF)
            m_new = jnp.maximum(m, jnp.max(s, axis=1, keepdims=True))
            alpha = jnp.exp(m - m_new)
            p = jnp.exp(s - m_new)
            l = alpha * l + jnp.sum(p, axis=1, keepdims=True)
            acc = alpha * acc + jnp.dot(p.astype(BF16), vb, preferred_element_type=F32)
            out.append((m_new, l, acc))
        return tuple(out)

    init = tuple((jnp.full((tq, 1), NEG_INF, F32), jnp.zeros((tq, 1), F32),
                  jnp.zeros((tq, HEAD_DIM), F32)) for _ in range(heads))
    carry = lax.fori_loop(0, i, lambda kt, c: tile(kt, c, False), init)
    carry = tile(i, carry, True)
    for g in range(heads):
        _, l, acc = carry[g]
        o_ref[:, _head_cols(g)] = (acc / l).astype(BF16)


ATT_TILE = 256
ATT_HEADS_PER_STEP = 4


def _fox_attention(proj, c3, B, S):
    T = B * S
    tq = _tile(S, ATT_TILE)
    nq = S // tq
    hp = ATT_HEADS_PER_STEP
    ng = N_HEADS // hp
    w = hp * HEAD_DIM
    return pl.pallas_call(
        functools.partial(_fox_body, tq=tq, heads=hp),
        grid=(B, ng, nq),
        in_specs=[
            pl.BlockSpec((tq, w), lambda b, h, i: (b * nq + i, h)),
            pl.BlockSpec((S, w), lambda b, h, i: (b, ng + h)),
            pl.BlockSpec((S, w), lambda b, h, i: (b, 2 * ng + h)),
            pl.BlockSpec((hp, 1, S), lambda b, h, i: (b * ng + h, 0, 0)),
        ],
        out_specs=pl.BlockSpec((tq, w), lambda b, h, i: (b * nq + i, h)),
        out_shape=jax.ShapeDtypeStruct((T, HEADS_W), BF16),
        compiler_params=_cp("parallel", "parallel", "arbitrary"),
        name="fox_attention",
    )(proj, proj, proj, c3)


F32_EXP_UNDERFLOW = -104.0


def _sb_body(q_ref, k_ref, v_ref, u_ref, o_ref, *, tq, heads):
    i = pl.program_id(2)
    qs = [q_ref[:, _head_cols(g)] for g in range(heads)]
    u = u_ref[...]
    row = lax.broadcasted_iota(I32, (tq, tq), 0)
    col = lax.broadcasted_iota(I32, (tq, tq), 1)
    strict = col < row

    def tile(kt, carry, masked):
        ks = pl.multiple_of(kt * tq, tq)
        out = []
        for g in range(heads):
            r_sum, acc = carry[g]
            kb = k_ref[pl.ds(ks, tq), _head_cols(g)]
            vb = v_ref[pl.ds(ks, tq), _head_cols(g)]
            z = lax.dot_general(qs[g], kb, _NT, preferred_element_type=F32)
            log_beta, log_om = _log_sigmoid_pair(z)
            if masked:
                log_om = jnp.where(strict, log_om, 0.0)
            hi, lo = _split_bf16(log_om)
            e = jnp.dot(hi, u, preferred_element_type=F32) + jnp.dot(lo, u, preferred_element_type=F32)
            a = jnp.exp(log_beta + e + r_sum)
            if masked:
                a = jnp.where(strict, a, 0.0)
            acc = acc + jnp.dot(a.astype(BF16), vb, preferred_element_type=F32)
            r_sum = r_sum + jnp.sum(log_om, axis=1, keepdims=True)
            out.append((r_sum, acc))
        return tuple(out)

    def live(carry):
        top = carry[0][0]
        for g in range(1, heads):
            top = jnp.maximum(top, carry[g][0])
        return (jnp.max(top) > F32_EXP_UNDERFLOW).astype(I32)

    init = tuple((jnp.zeros((tq, 1), F32), jnp.zeros((tq, HEAD_DIM), F32)) for _ in range(heads))
    carry = tile(i, init, True)

    def step(state):
        n, _, carry = state
        carry = tile(i - 1 - n, carry, False)
        return n + 1, live(carry), carry

    _, _, carry = lax.while_loop(lambda st: (st[0] < i) & (st[1] > 0), step, (jnp.int32(0), live(carry), carry))
    for g in range(heads):
        o_ref[:, _head_cols(g)] = carry[g][1].astype(BF16)


def _sb_attention(proj, B, S):
    T = B * S
    tq = _tile(S, ATT_TILE)
    nq = S // tq
    hp = ATT_HEADS_PER_STEP
    ng = N_HEADS // hp
    w = hp * HEAD_DIM
    r = lax.broadcasted_iota(I32, (tq, tq), 0)
    c = lax.broadcasted_iota(I32, (tq, tq), 1)
    u = (r > c).astype(BF16)
    return pl.pallas_call(
        functools.partial(_sb_body, tq=tq, heads=hp),
        grid=(B, ng, nq),
        in_specs=[
            pl.BlockSpec((tq, w), lambda b, h, i: (b * nq + i, 3 * ng + h)),
            pl.BlockSpec((S, w), lambda b, h, i: (b, 4 * ng + h)),
            pl.BlockSpec((S, w), lambda b, h, i: (b, 5 * ng + h)),
            pl.BlockSpec((tq, tq), lambda b, h, i: (0, 0)),
        ],
        out_specs=pl.BlockSpec((tq, w), lambda b, h, i: (b * nq + i, h)),
        out_shape=jax.ShapeDtypeStruct((T, HEADS_W), BF16),
        compiler_params=_cp("parallel", "parallel", "arbitrary"),
        name="sb_attention",
    )(proj, proj, proj, u)


def _att_out_body(oa_ref, ob_ref, wa_ref, wb_ref, x_ref, o_ref):
    acc = jnp.dot(oa_ref[...], wa_ref[...], preferred_element_type=F32)
    acc += jnp.dot(ob_ref[...], wb_ref[...], preferred_element_type=F32)
    o_ref[...] = x_ref[...] + acc


def _att_out_proj(oa, ob, w_out, x2):
    T, D = x2.shape
    tm = _tile(T, 1024)
    tn = _tile(D, 512)
    return pl.pallas_call(
        _att_out_body,
        grid=(T // tm, D // tn),
        in_specs=[
            pl.BlockSpec((tm, HEADS_W), lambda i, j: (i, 0)),
            pl.BlockSpec((tm, HEADS_W), lambda i, j: (i, 0)),
            pl.BlockSpec((HEADS_W, tn), lambda i, j: (0, j)),
            pl.BlockSpec((HEADS_W, tn), lambda i, j: (1, j)),
            pl.BlockSpec((tm, tn), lambda i, j: (i, j)),
        ],
        out_specs=pl.BlockSpec((tm, tn), lambda i, j: (i, j)),
        out_shape=jax.ShapeDtypeStruct((T, D), F32),
        compiler_params=_cp("parallel", "parallel"),
        name="att_out_proj",
    )(oa, ob, w_out, w_out, x2)


def _ffn_up_body(x_ref, g_ref, wg_ref, wu_ref, a_ref, h_scr):
    @pl.when(pl.program_id(1) == 0)
    def _():
        h_scr[...] = _rms(x_ref[...], g_ref[...]).astype(BF16)

    h = h_scr[...]
    g = jnp.dot(h, wg_ref[...], preferred_element_type=F32)
    u = jnp.dot(h, wu_ref[...], preferred_element_type=F32)
    a_ref[...] = (_silu(g) * u).astype(BF16)


def _ffn_up(x2, gnorm, wg, wu):
    T, D = x2.shape
    F = wg.shape[1]
    tm = _tile(T, 1024)
    tn = _tile(F, 512)
    return pl.pallas_call(
        _ffn_up_body,
        grid=(T // tm, F // tn),
        in_specs=[
            pl.BlockSpec((tm, D), lambda i, j: (i, 0)),
            pl.BlockSpec((1, D), lambda i, j: (0, 0)),
            pl.BlockSpec((D, tn), lambda i, j: (0, j)),
            pl.BlockSpec((D, tn), lambda i, j: (0, j)),
        ],
        out_specs=pl.BlockSpec((tm, tn), lambda i, j: (i, j)),
        out_shape=jax.ShapeDtypeStruct((T, F), BF16),
        scratch_shapes=[pltpu.VMEM((tm, D), BF16)],
        compiler_params=_cp("parallel", "arbitrary"),
        name="ffn_up",
    )(x2, gnorm, wg, wu)


def _ffn_down_body(a_ref, w_ref, x_ref, o_ref):
    o_ref[...] = x_ref[...] + jnp.dot(a_ref[...], w_ref[...], preferred_element_type=F32)


def _ffn_down(a, wd, x2):
    T, D = x2.shape
    F = a.shape[1]
    tm = _tile(T, 1024)
    tn = _tile(D, 256)
    return pl.pallas_call(
        _ffn_down_body,
        grid=(T // tm, D // tn),
        in_specs=[
            pl.BlockSpec((tm, F), lambda i, j: (i, 0)),
            pl.BlockSpec((F, tn), lambda i, j: (0, j)),
            pl.BlockSpec((tm, tn), lambda i, j: (i, j)),
        ],
        out_specs=pl.BlockSpec((tm, tn), lambda i, j: (i, j)),
        out_shape=jax.ShapeDtypeStruct((T, D), F32),
        compiler_params=_cp("parallel", "parallel"),
        name="ffn_down",
    )(a, wd, x2)


def _conv_in_body(x_ref, g_ref, wb_ref, wc_ref, wu_ref, gb_ref, gg_ref, h_scr):
    @pl.when(pl.program_id(1) == 0)
    def _():
        h_scr[...] = _rms(x_ref[...], g_ref[...]).astype(BF16)

    h = h_scr[...]
    gb_ref[...] = jnp.dot(h, wb_ref[...], preferred_element_type=F32).astype(BF16)
    c = jnp.dot(h, wc_ref[...], preferred_element_type=F32)
    u = jnp.dot(h, wu_ref[...], preferred_element_type=F32)
    gg_ref[...] = (c * u).astype(BF16)


def _conv_in(x2, gnorm, w_in):
    T, D = x2.shape
    tm = _tile(T, 1024)
    tn = _tile(D, 512)
    nd = D // tn
    return pl.pallas_call(
        _conv_in_body,
        grid=(T // tm, nd),
        in_specs=[
            pl.BlockSpec((tm, D), lambda i, j: (i, 0)),
            pl.BlockSpec((1, D), lambda i, j: (0, 0)),
            pl.BlockSpec((D, tn), lambda i, j: (0, j)),
            pl.BlockSpec((D, tn), lambda i, j: (0, nd + j)),
            pl.BlockSpec((D, tn), lambda i, j: (0, 2 * nd + j)),
        ],
        out_specs=[
            pl.BlockSpec((tm, tn), lambda i, j: (i, j)),
            pl.BlockSpec((tm, tn), lambda i, j: (i, j)),
        ],
        out_shape=[jax.ShapeDtypeStruct((T, D), BF16), jax.ShapeDtypeStruct((T, D), BF16)],
        scratch_shapes=[pltpu.VMEM((tm, D), BF16)],
        compiler_params=_cp("parallel", "arbitrary"),
        name="conv_in",
    )(x2, gnorm, w_in, w_in, w_in)


def _conv_out_body(gb_ref, g_ref, gp_ref, cw_ref, w_ref, x_ref, o_ref, *, tiles_per_seq):
    i = pl.program_id(0)
    g = g_ref[...].astype(F32)
    tm = g.shape[0]
    keep = jnp.where(i % tiles_per_seq == 0, 0.0, 1.0)
    prev = gp_ref[...].astype(F32) * keep
    p1 = prev[BF16_SUBLANE_TILE - 1:BF16_SUBLANE_TILE, :]
    p2 = prev[BF16_SUBLANE_TILE - 2:BF16_SUBLANE_TILE - 1, :]
    row = lax.broadcasted_iota(I32, g.shape, 0)
    g1 = jnp.where(row == 0, p1, pltpu.roll(g, 1, axis=0))
    g2 = jnp.where(row == 0, p2, jnp.where(row == 1, p1, pltpu.roll(g, 2, axis=0)))
    cw = cw_ref[...]
    conv = g2 * cw[0:1, :] + g1 * cw[1:2, :] + g * cw[2:3, :]
    y = (gb_ref[...].astype(F32) * conv).astype(BF16)
    o_ref[...] = x_ref[...] + jnp.dot(y, w_ref[...], preferred_element_type=F32)


def _conv_out(gb, gg, conv_w, w_out, x2, S):
    T, D = x2.shape
    tm = _tile(S, 256)
    pt = BF16_SUBLANE_TILE
    return pl.pallas_call(
        functools.partial(_conv_out_body, tiles_per_seq=S // tm),
        grid=(T // tm,),
        in_specs=[
            pl.BlockSpec((tm, D), lambda i: (i, 0)),
            pl.BlockSpec((tm, D), lambda i: (i, 0)),
            pl.BlockSpec((pt, D), lambda i: (jnp.maximum(i * (tm // pt) - 1, 0), 0)),
            pl.BlockSpec((CONV_WIDTH, D), lambda i: (0, 0)),
            pl.BlockSpec((D, D), lambda i: (0, 0)),
            pl.BlockSpec((tm, D), lambda i: (i, 0)),
        ],
        out_specs=pl.BlockSpec((tm, D), lambda i: (i, 0)),
        out_shape=jax.ShapeDtypeStruct((T, D), F32),
        compiler_params=_cp("parallel"),
        name="conv_out",
    )(gb, gg, gg, conv_w, w_out, x2)


def _router_body(x_ref, g_ref, wrh_ref, wrl_ref, tri_ref, hp_ref, mi_ref, mf_ref, cnt_ref, carry):
    @pl.when(pl.program_id(0) == 0)
    def _():
        carry[...] = jnp.zeros_like(carry)

    h = _rms(x_ref[...], g_ref[...])
    hb, hl = _split_bf16(h)
    bits = pltpu.bitcast(hb.astype(F32), U32)
    half = bits.shape[1] // 2
    hp_ref[...] = (bits[:, :half] >> 16) | bits[:, half:]

    logits = _skinny_nt(wrh_ref, wrl_ref, hb, hl)[:N_EXPERTS]
    eidx = lax.broadcasted_iota(I32, logits.shape, 0).astype(F32)
    ne = float(N_EXPERTS)
    m1 = jnp.max(logits, axis=0, keepdims=True)
    i1 = jnp.min(jnp.where(logits == m1, eidx, ne), axis=0, keepdims=True)
    rest = jnp.where(eidx == i1, NEG_INF, logits)
    m2 = jnp.max(rest, axis=0, keepdims=True)
    i2 = jnp.min(jnp.where(rest == m2, eidx, ne), axis=0, keepdims=True)
    e21 = jnp.exp(m2 - m1)
    g1 = 1.0 / (1.0 + e21)
    g2 = e21 * g1

    sel = jnp.where((eidx == i1) | (eidx == i2), 1.0, 0.0)
    incl = jnp.dot(sel, tri_ref[...], preferred_element_type=F32)
    pos = carry[...] + incl - sel
    carry[...] = carry[...] + jnp.sum(sel, axis=1, keepdims=True)
    p1 = jnp.sum(jnp.where(eidx == i1, pos, 0.0), axis=0, keepdims=True)
    p2 = jnp.sum(jnp.where(eidx == i2, pos, 0.0), axis=0, keepdims=True)
    meta = jnp.where(eidx == 0, i1, jnp.where(eidx == 1, i2, jnp.where(eidx == 2, p1, jnp.where(eidx == 3, p2, 0.0))))
    mi_ref[...] = meta.astype(I32)
    mf_ref[...] = jnp.where(eidx == 0, g1, jnp.where(eidx == 1, g2, 0.0))
    cnt_ref[...] = jnp.broadcast_to(carry[...], cnt_ref.shape)


def _router(x2, gnorm, wr_hi, wr_lo):
    T, D = x2.shape
    tm = _tile(T, 512)
    r = lax.broadcasted_iota(I32, (tm, tm), 0)
    c = lax.broadcasted_iota(I32, (tm, tm), 1)
    tri = (r <= c).astype(F32)
    return pl.pallas_call(
        _router_body,
        grid=(T // tm,),
        in_specs=[
            pl.BlockSpec((tm, D), lambda i: (i, 0)),
            pl.BlockSpec((1, D), lambda i: (0, 0)),
            pl.BlockSpec((BF16_SUBLANE_TILE, D), lambda i: (0, 0)),
            pl.BlockSpec((BF16_SUBLANE_TILE, D), lambda i: (0, 0)),
            pl.BlockSpec((tm, tm), lambda i: (0, 0)),
        ],
        out_specs=[
            pl.BlockSpec((tm, D // 2), lambda i: (i, 0)),
            pl.BlockSpec((N_EXPERTS, tm), lambda i: (0, i)),
            pl.BlockSpec((N_EXPERTS, tm), lambda i: (0, i)),
            pl.BlockSpec((N_EXPERTS, LANE), lambda i: (0, 0)),
        ],
        out_shape=[
            jax.ShapeDtypeStruct((T, D // 2), U32),
            jax.ShapeDtypeStruct((N_EXPERTS, T), I32),
            jax.ShapeDtypeStruct((N_EXPERTS, T), F32),
            jax.ShapeDtypeStruct((N_EXPERTS, LANE), F32),
        ],
        scratch_shapes=[pltpu.VMEM((N_EXPERTS, 1), F32)],
        compiler_params=_cp("arbitrary"),
        name="moe_router",
    )(x2, gnorm, wr_hi, wr_lo, tri)


def _dispatch_body(d1_ref, d2_ref, hp_ref, zero_ref, hs_ref, sem):
    del zero_ref
    n = hp_ref.shape[0]

    def issue(r, carry):
        src = hp_ref.at[pl.ds(r, 1)]
        pltpu.make_async_copy(src, hs_ref.at[pl.ds(d1_ref[r], 1)], sem).start()
        pltpu.make_async_copy(src, hs_ref.at[pl.ds(d2_ref[r], 1)], sem).start()
        return carry

    def drain(r, carry):
        row = pltpu.make_async_copy(hp_ref.at[pl.ds(0, 1)], hs_ref.at[pl.ds(0, 1)], sem)
        row.wait()
        row.wait()
        return carry

    lax.fori_loop(0, n, issue, 0)
    lax.fori_loop(0, n, drain, 0)


def _dispatch(hp, d1, d2, P):
    T, Dh = hp.shape
    tm = _tile(T, 256)
    zeros = jnp.zeros((P, Dh), U32)
    return pl.pallas_call(
        _dispatch_body,
        grid=(T // tm,),
        in_specs=[
            pl.BlockSpec((tm,), lambda i: (i,), memory_space=pltpu.SMEM),
            pl.BlockSpec((tm,), lambda i: (i,), memory_space=pltpu.SMEM),
            pl.BlockSpec((tm, Dh), lambda i: (i, 0)),
            pl.BlockSpec(memory_space=pl.ANY),
        ],
        out_specs=pl.BlockSpec(memory_space=pl.ANY),
        out_shape=jax.ShapeDtypeStruct((P, Dh), U32),
        scratch_shapes=[pltpu.SemaphoreType.DMA(())],
        input_output_aliases={3: 0},
        compiler_params=_cp("arbitrary"),
        name="moe_dispatch",
    )(d1, d2, hp, zeros)


def _unpack_rows(words):
    lo = pltpu.bitcast(words << 16, F32)
    hi = pltpu.bitcast(words & jnp.uint32(0xFFFF0000), F32)
    return jnp.concatenate([lo, hi], axis=1).astype(BF16)


def _expert_up_body(te_ref, hs_ref, wg_ref, wu_ref, a_ref):
    del te_ref
    h = _unpack_rows(hs_ref[...])
    g = jnp.dot(h, wg_ref[...], preferred_element_type=F32)
    u = jnp.dot(h, wu_ref[...], preferred_element_type=F32)
    a_ref[...] = (_silu(g) * u).astype(BF16)


def _expert_up(tile_expert, hs, wg, wu, tme):
    P, Dh = hs.shape
    _, D, F = wg.shape
    tn = F // 2 if (F // 2) % LANE == 0 else F
    n_tiles = P // tme
    return pl.pallas_call(
        _expert_up_body,
        grid_spec=pltpu.PrefetchScalarGridSpec(
            num_scalar_prefetch=1,
            grid=(F // tn, n_tiles),
            in_specs=[
                pl.BlockSpec((tme, Dh), lambda j, i, te: (i, 0)),
                pl.BlockSpec((None, D, tn), lambda j, i, te: (te[i], 0, j)),
                pl.BlockSpec((None, D, tn), lambda j, i, te: (te[i], 0, j)),
            ],
            out_specs=pl.BlockSpec((tme, tn), lambda j, i, te: (i, j)),
        ),
        out_shape=jax.ShapeDtypeStruct((P, F), BF16),
        compiler_params=_cp("arbitrary", "arbitrary"),
        name="moe_expert_up",
    )(tile_expert, hs, wg, wu)


def _expert_down_body(te_ref, a_ref, w_ref, y_ref):
    del te_ref
    y_ref[...] = jnp.dot(a_ref[...], w_ref[...], preferred_element_type=F32)


def _expert_down(tile_expert, act, wd, tme):
    P, F = act.shape
    D = wd.shape[2]
    return pl.pallas_call(
        _expert_down_body,
        grid_spec=pltpu.PrefetchScalarGridSpec(
            num_scalar_prefetch=1,
            grid=(P // tme,),
            in_specs=[
                pl.BlockSpec((tme, F), lambda i, te: (i, 0)),
                pl.BlockSpec((None, F, D), lambda i, te: (te[i], 0, 0)),
            ],
            out_specs=pl.BlockSpec((tme, D), lambda i, te: (i, 0)),
        ),
        out_shape=jax.ShapeDtypeStruct((P, D), F32),
        compiler_params=_cp("arbitrary"),
        name="moe_expert_down",
    )(tile_expert, act, wd)


def _combine_body(d1_ref, d2_ref, mf_ref, x_ref, fn_ref, y_ref, o_ref, ya, yb, sem, *, final_norm):
    n = x_ref.shape[0]

    def issue(r, carry):
        pltpu.make_async_copy(y_ref.at[pl.ds(d1_ref[r], 1)], ya.at[pl.ds(r, 1)], sem).start()
        pltpu.make_async_copy(y_ref.at[pl.ds(d2_ref[r], 1)], yb.at[pl.ds(r, 1)], sem).start()
        return carry

    def drain(r, carry):
        row = pltpu.make_async_copy(y_ref.at[pl.ds(0, 1)], ya.at[pl.ds(0, 1)], sem)
        row.wait()
        row.wait()
        return carry

    lax.fori_loop(0, n, issue, 0)
    lax.fori_loop(0, n, drain, 0)

    gates = mf_ref[...]
    eye = lax.broadcasted_iota(I32, (n, n), 0) == lax.broadcasted_iota(I32, (n, n), 1)
    g1 = jnp.sum(jnp.where(eye, gates[0:1, :], 0.0), axis=1, keepdims=True)
    g2 = jnp.sum(jnp.where(eye, gates[1:2, :], 0.0), axis=1, keepdims=True)
    out = x_ref[...] + (g1 * ya[...] + g2 * yb[...])
    if final_norm:
        out = _rms(out, fn_ref[...])
    o_ref[...] = out


def _combine(d1, d2, mf, x2, fnorm, y, final_norm):
    T, D = x2.shape
    tm = _tile(T, 256)
    return pl.pallas_call(
        functools.partial(_combine_body, final_norm=final_norm),
        grid=(T // tm,),
        in_specs=[
            pl.BlockSpec((tm,), lambda i: (i,), memory_space=pltpu.SMEM),
            pl.BlockSpec((tm,), lambda i: (i,), memory_space=pltpu.SMEM),
            pl.BlockSpec((N_EXPERTS, tm), lambda i: (0, i)),
            pl.BlockSpec((tm, D), lambda i: (i, 0)),
            pl.BlockSpec((1, D), lambda i: (0, 0)),
            pl.BlockSpec(memory_space=pl.ANY),
        ],
        out_specs=pl.BlockSpec((tm, D), lambda i: (i, 0)),
        out_shape=jax.ShapeDtypeStruct((T, D), F32),
        scratch_shapes=[pltpu.VMEM((tm, D), F32), pltpu.VMEM((tm, D), F32), pltpu.SemaphoreType.DMA(())],
        compiler_params=_cp("arbitrary"),
        name="moe_combine",
    )(d1, d2, mf, x2, fnorm, y)


def _pad_rows16(w_t):
    pad = jnp.zeros((BF16_SUBLANE_TILE - w_t.shape[0], w_t.shape[1]), F32)
    w = jnp.concatenate([w_t, pad], axis=0)
    hi = w.astype(BF16)
    lo = (w - hi.astype(F32)).astype(BF16)
    return hi, lo


def _attention_layer(x2, B, S, gnorm, w_in, b_forget, w_out):
    w_big = jnp.concatenate([w_in[:, :3 * HEADS_W], w_in[:, 3 * HEADS_W + N_HEADS:]], axis=1).astype(BF16)
    wf_hi, wf_lo = _pad_rows16(w_in[:, 3 * HEADS_W:3 * HEADS_W + N_HEADS].T)
    proj, f16 = _att_in_proj(x2, gnorm, w_big, wf_hi, wf_lo)
    c = _forget_cumsum(f16, b_forget.reshape(N_HEADS, 1), B, S)
    oa = _fox_attention(proj, c.reshape(B * N_HEADS, 1, S), B, S)
    ob = _sb_attention(proj, B, S)
    return _att_out_proj(oa, ob, w_out.astype(BF16), x2)


def _dense_ffn_layer(x2, gnorm, wg, wu, wd):
    a = _ffn_up(x2, gnorm, wg.astype(BF16), wu.astype(BF16))
    return _ffn_down(a, wd.astype(BF16), x2)


def _conv_layer(x2, S, gnorm, w_in, conv_w, w_out):
    gb, gg = _conv_in(x2, gnorm, w_in.astype(BF16))
    return _conv_out(gb, gg, conv_w, w_out.astype(BF16), x2, S)


def _moe_layer(x2, gnorm, w_router, wg, wu, wd, fnorm, final_norm):
    T, D = x2.shape
    tme = 512 if T >= 4096 else 128
    n_tiles = (2 * T) // tme + N_EXPERTS
    P = n_tiles * tme

    wr_hi, wr_lo = _pad_rows16(w_router.T)
    hp, mi, mf, cnt = _router(x2, gnorm, wr_hi, wr_lo)

    counts = cnt[:, 0].astype(I32)
    padded = ((counts + tme - 1) // tme) * tme
    ends = jnp.cumsum(padded)
    offs = ends - padded
    d1 = jnp.take(offs, mi[0]) + mi[2]
    d2 = jnp.take(offs, mi[1]) + mi[3]
    tile_start = jnp.arange(n_tiles, dtype=I32) * tme
    tile_expert = jnp.sum((tile_start[:, None] >= ends[None, :]).astype(I32), axis=1)
    tile_expert = jnp.minimum(tile_expert, N_EXPERTS - 1)

    hs = _dispatch(hp, d1, d2, P)
    act = _expert_up(tile_expert, hs, wg.astype(BF16), wu.astype(BF16), tme)
    y = _expert_down(tile_expert, act, wd.astype(BF16), tme)
    return _combine(d1, d2, mf, x2, fnorm, y, final_norm)


def kernel(x, mix_norm, ffn_norm, final_norm, w_in_att, b_forget, w_out_att, w_in_conv, conv_w,
           w_out_conv, w_gate_dense, w_up_dense, w_down_dense, w_router, w_gate_moe, w_up_moe,
           w_down_moe):
    B, S, D = x.shape
    depth = mix_norm.shape[0]
    assert depth % 2 == 0, "the final rmsnorm is fused into the last (routed) layer"
    x2 = x.reshape(B * S, D)
    fnorm = final_norm.reshape(1, D)
    for i in range(depth):
        j = i // 2
        mg = mix_norm[i].reshape(1, D)
        fg = ffn_norm[i].reshape(1, D)
        if i % 2 == 0:
            x2 = _attention_layer(x2, B, S, mg, w_in_att[j], b_forget[j], w_out_att[j])
            x2 = _dense_ffn_layer(x2, fg, w_gate_dense[j], w_up_dense[j], w_down_dense[j])
        else:
            x2 = _conv_layer(x2, S, mg, w_in_conv[j], conv_w[j], w_out_conv[j])
            x2 = _moe_layer(x2, fg, w_router[j], w_gate_moe[j], w_up_moe[j], w_down_moe[j],
                            fnorm, final_norm=(i == depth - 1))
    return x2.reshape(B, S, D)
```

```python
import functools
import math

import jax
import jax.numpy as jnp
from jax import lax
from jax.experimental import pallas as pl
from jax.experimental.pallas import tpu as pltpu

F32 = jnp.float32
BF16 = jnp.bfloat16
I32 = jnp.int32
U32 = jnp.uint32

HEAD_DIM = 128
N_HEADS = 8
HEADS_W = N_HEADS * HEAD_DIM
N_EXPERTS = 8
RMS_EPS = 1e-6
CONV_WIDTH = 3
LOG2E = math.log2(math.e)
QK_SCALE_LOG2 = LOG2E / math.sqrt(HEAD_DIM)

V7X_VMEM_LIMIT_BYTES = 56 * 1024 * 1024
LANE = 128
BF16_SUBLANE_TILE = 16
NEG_INF = float("-inf")

_NT = (((1,), (1,)), ((), ()))


def _cp(*sem):
    return pltpu.CompilerParams(dimension_semantics=sem, vmem_limit_bytes=V7X_VMEM_LIMIT_BYTES)


def _tile(n, pref, unit=LANE):
    if n <= pref:
        return n
    t = (pref // unit) * unit
    while t > unit and n % t:
        t -= unit
    assert n % t == 0, (n, pref)
    return t


def _rms(x, g):
    ms = jnp.mean(x * x, axis=-1, keepdims=True)
    return x * lax.rsqrt(ms + RMS_EPS) * g


def _split_bf16(v):
    hi = v.astype(BF16)
    lo = (v - hi.astype(F32)).astype(BF16)
    return hi, lo


def _log_sigmoid_pair(z):
    sp = jnp.log1p(jnp.exp(-jnp.abs(z)))
    return jnp.minimum(z, 0.0) - sp, -jnp.maximum(z, 0.0) - sp


def _silu(g):
    return g / (1.0 + jnp.exp(-g))


def _skinny_nt(wh_ref, wl_ref, hb, hl):
    wh = wh_ref[...]
    out = lax.dot_general(wh, hb, _NT, preferred_element_type=F32)
    out += lax.dot_general(wh, hl, _NT, preferred_element_type=F32)
    out += lax.dot_general(wl_ref[...], hb, _NT, preferred_element_type=F32)
    return out


def _att_in_body(x_ref, g_ref, w_ref, wv_ref, wfh_ref, wfl_ref, o_ref, vt_ref, f_ref, h_scr, *, q_blocks, n_main):
    j = pl.program_id(1)

    @pl.when(j == 0)
    def _():
        h = _rms(x_ref[...], g_ref[...])
        hb, hl = _split_bf16(h)
        h_scr[...] = hb
        f_ref[...] = _skinny_nt(wfh_ref, wfl_ref, hb, hl)

    @pl.when(j < n_main)
    def _():
        o_ref[...] = jnp.dot(h_scr[...], w_ref[...], preferred_element_type=F32).astype(BF16)

    @pl.when(j >= n_main)
    def _():
        acc = lax.dot_general(wv_ref[...], h_scr[...], _NT, preferred_element_type=F32)
        is_q = (j - n_main) < 2 * q_blocks
        vt_ref[...] = (acc * jnp.where(is_q, QK_SCALE_LOG2, 1.0)).astype(BF16)


def _att_in_proj(x2, gnorm, w_k, w_qvt, wf_hi, wf_lo):
    T, D = x2.shape
    N = w_k.shape[1]
    NV = w_qvt.shape[0]
    tm = _tile(T, 1024)
    tn = _tile(HEADS_W, 512)
    n_main = N // tn
    return pl.pallas_call(
        functools.partial(_att_in_body, q_blocks=HEADS_W // tn, n_main=n_main),
        grid=(T // tm, n_main + NV // tn),
        in_specs=[
            pl.BlockSpec((tm, D), lambda i, j: (i, 0)),
            pl.BlockSpec((1, D), lambda i, j: (0, 0)),
            pl.BlockSpec((D, tn), lambda i, j: (0, jnp.minimum(j, n_main - 1))),
            pl.BlockSpec((tn, D), lambda i, j: (jnp.maximum(j - n_main, 0), 0)),
            pl.BlockSpec((BF16_SUBLANE_TILE, D), lambda i, j: (0, 0)),
            pl.BlockSpec((BF16_SUBLANE_TILE, D), lambda i, j: (0, 0)),
        ],
        out_specs=[
            pl.BlockSpec((tm, tn), lambda i, j: (i, jnp.minimum(j, n_main - 1))),
            pl.BlockSpec((tn, tm), lambda i, j: (jnp.maximum(j - n_main, 0), i)),
            pl.BlockSpec((BF16_SUBLANE_TILE, tm), lambda i, j: (0, i)),
        ],
        out_shape=[
            jax.ShapeDtypeStruct((T, N), BF16),
            jax.ShapeDtypeStruct((NV, T), BF16),
            jax.ShapeDtypeStruct((BF16_SUBLANE_TILE, T), F32),
        ],
        scratch_shapes=[pltpu.VMEM((tm, D), BF16)],
        compiler_params=_cp("parallel", "arbitrary"),
        name="att_in_proj",
    )(x2, gnorm, w_k, w_qvt, wf_hi, wf_lo)


def _forget_cumsum_body(f_ref, b_ref, cb_ref):
    z = f_ref[...] + b_ref[...]
    lf, _ = _log_sigmoid_pair(z)
    S = lf.shape[1]
    lane = lax.broadcasted_iota(I32, lf.shape, 1)
    c = lf
    sh = 1
    while sh < S:
        c = c + jnp.where(lane >= sh, pltpu.roll(c, sh, axis=1), 0.0)
        sh *= 2
    c2 = c * LOG2E
    hi = c2.astype(BF16).astype(F32)
    r1 = c2 - hi
    mid = r1.astype(BF16).astype(F32)
    lo = (r1 - mid).astype(BF16).astype(F32)
    pad = jnp.zeros((LANE - 3 * N_HEADS, S), F32)
    cb_ref[...] = jnp.concatenate([hi, mid, lo, pad], axis=0).T.astype(BF16)


def _forget_cumsum(f16, b_col, B, S):
    return pl.pallas_call(
        _forget_cumsum_body,
        grid=(B,),
        in_specs=[
            pl.BlockSpec((N_HEADS, S), lambda b: (0, b)),
            pl.BlockSpec((N_HEADS, 1), lambda b: (0, 0)),
        ],
        out_specs=pl.BlockSpec((S, LANE), lambda b: (b, 0)),
        out_shape=jax.ShapeDtypeStruct((B * S, LANE), BF16),
        compiler_params=_cp("parallel"),
        name="forget_cumsum",
    )(f16, b_col)


def _head_cols(g):
    return slice(g * HEAD_DIM, (g + 1) * HEAD_DIM)


def _fox_body(qt_ref, k_ref, vt_ref, cb_ref, o_ref, *, tq, heads):
    hg = pl.program_id(1)
    i = pl.program_id(2)
    sub = lax.broadcasted_iota(I32, (LANE, tq), 0)
    qs = []
    for g in range(heads):
        h = hg * heads + g
        pick = (sub == h) | (sub == N_HEADS + h) | (sub == 2 * N_HEADS + h)
        qs.append(jnp.concatenate([qt_ref[_head_cols(g), :], jnp.where(pick, -1.0, 0.0).astype(BF16)], axis=0))
    key = lax.broadcasted_iota(I32, (tq, tq), 0)
    qry = lax.broadcasted_iota(I32, (tq, tq), 1)

    def scores(kt):
        ks = pl.multiple_of(kt * tq, tq)
        cb = cb_ref[pl.ds(ks, tq), :]
        out = []
        for g in range(heads):
            k_aug = jnp.concatenate([k_ref[pl.ds(ks, tq), _head_cols(g)], cb], axis=1)
            out.append(jnp.dot(k_aug, qs[g], preferred_element_type=F32))
        return tuple(out)

    def finish(kt, sc, carry, masked):
        ks = pl.multiple_of(kt * tq, tq)
        probs = []
        for g in range(heads):
            m, l, _ = carry[g]
            s = sc[g]
            if masked:
                s = jnp.where(key <= qry, s, NEG_INF)
            m_new = jnp.maximum(m, jnp.max(s, axis=0, keepdims=True))
            alpha = jnp.exp2(m - m_new)
            p = jnp.exp2(s - m_new)
            l = alpha * l + jnp.sum(p, axis=0, keepdims=True)
            probs.append((m_new, l, alpha, p.astype(BF16)))
        out = []
        for g in range(heads):
            m_new, l, alpha, p = probs[g]
            vt = vt_ref[_head_cols(g), pl.ds(ks, tq)]
            acc = alpha * carry[g][2] + jnp.dot(vt, p, preferred_element_type=F32)
            out.append((m_new, l, acc))
        return tuple(out)

    def step(kt, state):
        sc, carry = state
        nxt = scores(kt + 1)
        return nxt, finish(kt, sc, carry, False)

    init = tuple((jnp.full((1, tq), NEG_INF, F32), jnp.zeros((1, tq), F32),
                  jnp.zeros((HEAD_DIM, tq), F32)) for _ in range(heads))
    sc, carry = lax.fori_loop(0, i, step, (scores(0), init))
    carry = finish(i, sc, carry, True)
    for g in range(heads):
        _, l, acc = carry[g]
        o_ref[:, _head_cols(g)] = (acc / l).T.astype(BF16)


ATT_TILE = 256
ATT_HEADS_PER_STEP = 4


def _fox_attention(keys, qvt, cb, B, S):
    T = B * S
    tq = _tile(S, ATT_TILE)
    nq = S // tq
    hp = ATT_HEADS_PER_STEP
    ng = N_HEADS // hp
    w = hp * HEAD_DIM
    return pl.pallas_call(
        functools.partial(_fox_body, tq=tq, heads=hp),
        grid=(B, ng, nq),
        in_specs=[
            pl.BlockSpec((w, tq), lambda b, h, i: (h, b * nq + i)),
            pl.BlockSpec((S, w), lambda b, h, i: (b, h)),
            pl.BlockSpec((w, S), lambda b, h, i: (2 * ng + h, b)),
            pl.BlockSpec((S, LANE), lambda b, h, i: (b, 0)),
        ],
        out_specs=pl.BlockSpec((tq, w), lambda b, h, i: (b * nq + i, h)),
        out_shape=jax.ShapeDtypeStruct((T, HEADS_W), BF16),
        compiler_params=_cp("parallel", "parallel", "arbitrary"),
        name="fox_attention",
    )(qvt, keys, qvt, cb)


F32_EXP2_UNDERFLOW = -150.0


def _sb_body(qt_ref, k_ref, vt_ref, ut_ref, o_ref, *, tq, heads):
    i = pl.program_id(2)
    qs = [qt_ref[_head_cols(g), :] for g in range(heads)]
    ut = ut_ref[...]
    key = lax.broadcasted_iota(I32, (tq, tq), 0)
    qry = lax.broadcasted_iota(I32, (tq, tq), 1)
    strict = key < qry

    def scores(kt):
        ks = pl.multiple_of(kt * tq, tq)
        return tuple(jnp.dot(k_ref[pl.ds(ks, tq), _head_cols(g)], qs[g], preferred_element_type=F32)
                     for g in range(heads))

    def finish(kt, sc, carry, masked):
        ks = pl.multiple_of(kt * tq, tq)
        mid = []
        for g in range(heads):
            z = sc[g]
            log_beta = jnp.minimum(z, 0.0) - jnp.log2(1.0 + jnp.exp2(-jnp.abs(z)))
            log_om = log_beta - z
            if masked:
                log_om = jnp.where(strict, log_om, 0.0)
            hi, lo = _split_bf16(log_om)
            e = jnp.dot(ut, hi, preferred_element_type=F32) + jnp.dot(ut, lo, preferred_element_type=F32)
            mid.append((log_beta, log_om, e))
        out = []
        for g in range(heads):
            r_sum, acc = carry[g]
            log_beta, log_om, e = mid[g]
            a = jnp.exp2(log_beta + e + r_sum)
            if masked:
                a = jnp.where(strict, a, 0.0)
            vt = vt_ref[_head_cols(g), pl.ds(ks, tq)]
            acc = acc + jnp.dot(vt, a.astype(BF16), preferred_element_type=F32)
            out.append((r_sum + jnp.sum(log_om, axis=0, keepdims=True), acc))
        return tuple(out)

    def live(carry):
        top = carry[0][0]
        for g in range(1, heads):
            top = jnp.maximum(top, carry[g][0])
        return (jnp.max(top) > F32_EXP2_UNDERFLOW).astype(I32)

    init = tuple((jnp.zeros((1, tq), F32), jnp.zeros((HEAD_DIM, tq), F32)) for _ in range(heads))
    carry = finish(i, scores(i), init, True)

    def step(state):
        n, _, sc, carry = state
        kt = i - 1 - n
        nxt = scores(jnp.maximum(kt - 1, 0))
        carry = finish(kt, sc, carry, False)
        return n + 1, live(carry), nxt, carry

    state = (jnp.int32(0), live(carry), scores(jnp.maximum(i - 1, 0)), carry)
    _, _, _, carry = lax.while_loop(lambda st: (st[0] < i) & (st[1] > 0), step, state)
    for g in range(heads):
        o_ref[:, _head_cols(g)] = carry[g][1].T.astype(BF16)


def _sb_attention(keys, qvt, B, S):
    T = B * S
    tq = _tile(S, ATT_TILE)
    nq = S // tq
    hp = ATT_HEADS_PER_STEP
    ng = N_HEADS // hp
    w = hp * HEAD_DIM
    r = lax.broadcasted_iota(I32, (tq, tq), 0)
    c = lax.broadcasted_iota(I32, (tq, tq), 1)
    ut = (c > r).astype(BF16)
    return pl.pallas_call(
        functools.partial(_sb_body, tq=tq, heads=hp),
        grid=(B, ng, nq),
        in_specs=[
            pl.BlockSpec((w, tq), lambda b, h, i: (ng + h, b * nq + i)),
            pl.BlockSpec((S, w), lambda b, h, i: (b, ng + h)),
            pl.BlockSpec((w, S), lambda b, h, i: (3 * ng + h, b)),
            pl.BlockSpec((tq, tq), lambda b, h, i: (0, 0)),
        ],
        out_specs=pl.BlockSpec((tq, w), lambda b, h, i: (b * nq + i, h)),
        out_shape=jax.ShapeDtypeStruct((T, HEADS_W), BF16),
        compiler_params=_cp("parallel", "parallel", "arbitrary"),
        name="sb_attention",
    )(qvt, keys, qvt, ut)


def _att_out_body(oa_ref, ob_ref, wa_ref, wb_ref, x_ref, o_ref):
    acc = jnp.dot(oa_ref[...], wa_ref[...], preferred_element_type=F32)
    acc += jnp.dot(ob_ref[...], wb_ref[...], preferred_element_type=F32)
    o_ref[...] = x_ref[...] + acc


def _att_out_proj(oa, ob, w_out, x2):
    T, D = x2.shape
    tm = _tile(T, 1024)
    tn = _tile(D, 512)
    return pl.pallas_call(
        _att_out_body,
        grid=(T // tm, D // tn),
        in_specs=[
            pl.BlockSpec((tm, HEADS_W), lambda i, j: (i, 0)),
            pl.BlockSpec((tm, HEADS_W), lambda i, j: (i, 0)),
            pl.BlockSpec((HEADS_W, tn), lambda i, j: (0, j)),
            pl.BlockSpec((HEADS_W, tn), lambda i, j: (1, j)),
            pl.BlockSpec((tm, tn), lambda i, j: (i, j)),
        ],
        out_specs=pl.BlockSpec((tm, tn), lambda i, j: (i, j)),
        out_shape=jax.ShapeDtypeStruct((T, D), F32),
        compiler_params=_cp("parallel", "parallel"),
        name="att_out_proj",
    )(oa, ob, w_out, w_out, x2)


def _ffn_up_body(x_ref, g_ref, wg_ref, wu_ref, a_ref, h_scr):
    @pl.when(pl.program_id(1) == 0)
    def _():
        h_scr[...] = _rms(x_ref[...], g_ref[...]).astype(BF16)

    h = h_scr[...]
    g = jnp.dot(h, wg_ref[...], preferred_element_type=F32)
    u = jnp.dot(h, wu_ref[...], preferred_element_type=F32)
    a_ref[...] = (_silu(g) * u).astype(BF16)


def _ffn_up(x2, gnorm, wg, wu):
    T, D = x2.shape
    F = wg.shape[1]
    tm = _tile(T, 1024)
    tn = _tile(F, 512)
    return pl.pallas_call(
        _ffn_up_body,
        grid=(T // tm, F // tn),
        in_specs=[
            pl.BlockSpec((tm, D), lambda i, j: (i, 0)),
            pl.BlockSpec((1, D), lambda i, j: (0, 0)),
            pl.BlockSpec((D, tn), lambda i, j: (0, j)),
            pl.BlockSpec((D, tn), lambda i, j: (0, j)),
        ],
        out_specs=pl.BlockSpec((tm, tn), lambda i, j: (i, j)),
        out_shape=jax.ShapeDtypeStruct((T, F), BF16),
        scratch_shapes=[pltpu.VMEM((tm, D), BF16)],
        compiler_params=_cp("parallel", "arbitrary"),
        name="ffn_up",
    )(x2, gnorm, wg, wu)


def _ffn_down_body(a_ref, w_ref, x_ref, o_ref):
    o_ref[...] = x_ref[...] + jnp.dot(a_ref[...], w_ref[...], preferred_element_type=F32)


def _ffn_down(a, wd, x2):
    T, D = x2.shape
    F = a.shape[1]
    tm = _tile(T, 1024)
    tn = _tile(D, 256)
    return pl.pallas_call(
        _ffn_down_body,
        grid=(T // tm, D // tn),
        in_specs=[
            pl.BlockSpec((tm, F), lambda i, j: (i, 0)),
            pl.BlockSpec((F, tn), lambda i, j: (0, j)),
            pl.BlockSpec((tm, tn), lambda i, j: (i, j)),
        ],
        out_specs=pl.BlockSpec((tm, tn), lambda i, j: (i, j)),
        out_shape=jax.ShapeDtypeStruct((T, D), F32),
        compiler_params=_cp("parallel", "parallel"),
        name="ffn_down",
    )(a, wd, x2)


def _conv_in_body(x_ref, g_ref, wb_ref, wc_ref, wu_ref, gb_ref, gg_ref, h_scr):
    @pl.when(pl.program_id(1) == 0)
    def _():
        h_scr[...] = _rms(x_ref[...], g_ref[...]).astype(BF16)

    h = h_scr[...]
    gb_ref[...] = jnp.dot(h, wb_ref[...], preferred_element_type=F32).astype(BF16)
    c = jnp.dot(h, wc_ref[...], preferred_element_type=F32)
    u = jnp.dot(h, wu_ref[...], preferred_element_type=F32)
    gg_ref[...] = (c * u).astype(BF16)


def _conv_in(x2, gnorm, w_in):
    T, D = x2.shape
    tm = _tile(T, 1024)
    tn = _tile(D, 512)
    nd = D // tn
    return pl.pallas_call(
        _conv_in_body,
        grid=(T // tm, nd),
        in_specs=[
            pl.BlockSpec((tm, D), lambda i, j: (i, 0)),
            pl.BlockSpec((1, D), lambda i, j: (0, 0)),
            pl.BlockSpec((D, tn), lambda i, j: (0, j)),
            pl.BlockSpec((D, tn), lambda i, j: (0, nd + j)),
            pl.BlockSpec((D, tn), lambda i, j: (0, 2 * nd + j)),
        ],
        out_specs=[
            pl.BlockSpec((tm, tn), lambda i, j: (i, j)),
            pl.BlockSpec((tm, tn), lambda i, j: (i, j)),
        ],
        out_shape=[jax.ShapeDtypeStruct((T, D), BF16), jax.ShapeDtypeStruct((T, D), BF16)],
        scratch_shapes=[pltpu.VMEM((tm, D), BF16)],
        compiler_params=_cp("parallel", "arbitrary"),
        name="conv_in",
    )(x2, gnorm, w_in, w_in, w_in)


def _conv_out_body(gb_ref, g_ref, gp_ref, cw_ref, w_ref, x_ref, o_ref, *, tiles_per_seq):
    i = pl.program_id(0)
    g = g_ref[...].astype(F32)
    tm = g.shape[0]
    keep = jnp.where(i % tiles_per_seq == 0, 0.0, 1.0)
    prev = gp_ref[...].astype(F32) * keep
    p1 = prev[BF16_SUBLANE_TILE - 1:BF16_SUBLANE_TILE, :]
    p2 = prev[BF16_SUBLANE_TILE - 2:BF16_SUBLANE_TILE - 1, :]
    row = lax.broadcasted_iota(I32, g.shape, 0)
    g1 = jnp.where(row == 0, p1, pltpu.roll(g, 1, axis=0))
    g2 = jnp.where(row == 0, p2, jnp.where(row == 1, p1, pltpu.roll(g, 2, axis=0)))
    cw = cw_ref[...]
    conv = g2 * cw[0:1, :] + g1 * cw[1:2, :] + g * cw[2:3, :]
    y = (gb_ref[...].astype(F32) * conv).astype(BF16)
    o_ref[...] = x_ref[...] + jnp.dot(y, w_ref[...], preferred_element_type=F32)


def _conv_out(gb, gg, conv_w, w_out, x2, S):
    T, D = x2.shape
    tm = _tile(S, 256)
    pt = BF16_SUBLANE_TILE
    return pl.pallas_call(
        functools.partial(_conv_out_body, tiles_per_seq=S // tm),
        grid=(T // tm,),
        in_specs=[
            pl.BlockSpec((tm, D), lambda i: (i, 0)),
            pl.BlockSpec((tm, D), lambda i: (i, 0)),
            pl.BlockSpec((pt, D), lambda i: (jnp.maximum(i * (tm // pt) - 1, 0), 0)),
            pl.BlockSpec((CONV_WIDTH, D), lambda i: (0, 0)),
            pl.BlockSpec((D, D), lambda i: (0, 0)),
            pl.BlockSpec((tm, D), lambda i: (i, 0)),
        ],
        out_specs=pl.BlockSpec((tm, D), lambda i: (i, 0)),
        out_shape=jax.ShapeDtypeStruct((T, D), F32),
        compiler_params=_cp("parallel"),
        name="conv_out",
    )(gb, gg, gg, conv_w, w_out, x2)


def _router_body(x_ref, g_ref, wrh_ref, wrl_ref, tri_ref, hp_ref, mi_ref, mf_ref, cnt_ref, carry):
    @pl.when(pl.program_id(0) == 0)
    def _():
        carry[...] = jnp.zeros_like(carry)

    h = _rms(x_ref[...], g_ref[...])
    hb, hl = _split_bf16(h)
    bits = pltpu.bitcast(hb.astype(F32), U32)
    half = bits.shape[1] // 2
    hp_ref[...] = (bits[:, :half] >> 16) | bits[:, half:]

    logits = _skinny_nt(wrh_ref, wrl_ref, hb, hl)[:N_EXPERTS]
    eidx = lax.broadcasted_iota(I32, logits.shape, 0).astype(F32)
    ne = float(N_EXPERTS)
    m1 = jnp.max(logits, axis=0, keepdims=True)
    i1 = jnp.min(jnp.where(logits == m1, eidx, ne), axis=0, keepdims=True)
    rest = jnp.where(eidx == i1, NEG_INF, logits)
    m2 = jnp.max(rest, axis=0, keepdims=True)
    i2 = jnp.min(jnp.where(rest == m2, eidx, ne), axis=0, keepdims=True)
    e21 = jnp.exp(m2 - m1)
    g1 = 1.0 / (1.0 + e21)
    g2 = e21 * g1

    sel = jnp.where((eidx == i1) | (eidx == i2), 1.0, 0.0)
    incl = jnp.dot(sel, tri_ref[...], preferred_element_type=F32)
    pos = carry[...] + incl - sel
    carry[...] = carry[...] + jnp.sum(sel, axis=1, keepdims=True)
    p1 = jnp.sum(jnp.where(eidx == i1, pos, 0.0), axis=0, keepdims=True)
    p2 = jnp.sum(jnp.where(eidx == i2, pos, 0.0), axis=0, keepdims=True)
    meta = jnp.where(eidx == 0, i1, jnp.where(eidx == 1, i2, jnp.where(eidx == 2, p1, jnp.where(eidx == 3, p2, 0.0))))
    mi_ref[...] = meta.astype(I32)
    mf_ref[...] = jnp.where(eidx == 0, g1, jnp.where(eidx == 1, g2, 0.0))
    cnt_ref[...] = jnp.broadcast_to(carry[...], cnt_ref.shape)


def _router(x2, gnorm, wr_hi, wr_lo):
    T, D = x2.shape
    tm = _tile(T, 512)
    r = lax.broadcasted_iota(I32, (tm, tm), 0)
    c = lax.broadcasted_iota(I32, (tm, tm), 1)
    tri = (r <= c).astype(F32)
    return pl.pallas_call(
        _router_body,
        grid=(T // tm,),
        in_specs=[
            pl.BlockSpec((tm, D), lambda i: (i, 0)),
            pl.BlockSpec((1, D), lambda i: (0, 0)),
            pl.BlockSpec((BF16_SUBLANE_TILE, D), lambda i: (0, 0)),
            pl.BlockSpec((BF16_SUBLANE_TILE, D), lambda i: (0, 0)),
            pl.BlockSpec((tm, tm), lambda i: (0, 0)),
        ],
        out_specs=[
            pl.BlockSpec((tm, D // 2), lambda i: (i, 0)),
            pl.BlockSpec((N_EXPERTS, tm), lambda i: (0, i)),
            pl.BlockSpec((N_EXPERTS, tm), lambda i: (0, i)),
            pl.BlockSpec((N_EXPERTS, LANE), lambda i: (0, 0)),
        ],
        out_shape=[
            jax.ShapeDtypeStruct((T, D // 2), U32),
            jax.ShapeDtypeStruct((N_EXPERTS, T), I32),
            jax.ShapeDtypeStruct((N_EXPERTS, T), F32),
            jax.ShapeDtypeStruct((N_EXPERTS, LANE), F32),
        ],
        scratch_shapes=[pltpu.VMEM((N_EXPERTS, 1), F32)],
        compiler_params=_cp("arbitrary"),
        name="moe_router",
    )(x2, gnorm, wr_hi, wr_lo, tri)


def _dispatch_body(d1_ref, d2_ref, hp_ref, zero_ref, hs_ref, sem):
    del zero_ref
    n = hp_ref.shape[0]

    def issue(r, carry):
        src = hp_ref.at[pl.ds(r, 1)]
        pltpu.make_async_copy(src, hs_ref.at[pl.ds(d1_ref[r], 1)], sem).start()
        pltpu.make_async_copy(src, hs_ref.at[pl.ds(d2_ref[r], 1)], sem).start()
        return carry

    def drain(r, carry):
        row = pltpu.make_async_copy(hp_ref.at[pl.ds(0, 1)], hs_ref.at[pl.ds(0, 1)], sem)
        row.wait()
        row.wait()
        return carry

    lax.fori_loop(0, n, issue, 0)
    lax.fori_loop(0, n, drain, 0)


def _dispatch(hp, d1, d2, P):
    T, Dh = hp.shape
    tm = _tile(T, 256)
    zeros = jnp.zeros((P, Dh), U32)
    return pl.pallas_call(
        _dispatch_body,
        grid=(T // tm,),
        in_specs=[
            pl.BlockSpec((tm,), lambda i: (i,), memory_space=pltpu.SMEM),
            pl.BlockSpec((tm,), lambda i: (i,), memory_space=pltpu.SMEM),
            pl.BlockSpec((tm, Dh), lambda i: (i, 0)),
            pl.BlockSpec(memory_space=pl.ANY),
        ],
        out_specs=pl.BlockSpec(memory_space=pl.ANY),
        out_shape=jax.ShapeDtypeStruct((P, Dh), U32),
        scratch_shapes=[pltpu.SemaphoreType.DMA(())],
        input_output_aliases={3: 0},
        compiler_params=_cp("arbitrary"),
        name="moe_dispatch",
    )(d1, d2, hp, zeros)


def _unpack_rows(words):
    lo = pltpu.bitcast(words << 16, F32)
    hi = pltpu.bitcast(words & jnp.uint32(0xFFFF0000), F32)
    return jnp.concatenate([lo, hi], axis=1).astype(BF16)


def _expert_up_body(te_ref, hs_ref, wg_ref, wu_ref, a_ref, *, n_tiles):
    @pl.when(pl.program_id(1) < te_ref[n_tiles])
    def _():
        h = _unpack_rows(hs_ref[...])
        g = jnp.dot(h, wg_ref[...], preferred_element_type=F32)
        u = jnp.dot(h, wu_ref[...], preferred_element_type=F32)
        a_ref[...] = (_silu(g) * u).astype(BF16)

    @pl.when(pl.program_id(1) >= te_ref[n_tiles])
    def _():
        a_ref[...] = jnp.zeros_like(a_ref)


def _expert_up(tile_table, hs, wg, wu, tme):
    P, Dh = hs.shape
    _, D, F = wg.shape
    tn = F // 2 if (F // 2) % LANE == 0 else F
    n_tiles = P // tme
    live = lambda i, te: jnp.minimum(i, te[n_tiles] - 1)
    return pl.pallas_call(
        functools.partial(_expert_up_body, n_tiles=n_tiles),
        grid_spec=pltpu.PrefetchScalarGridSpec(
            num_scalar_prefetch=1,
            grid=(F // tn, n_tiles),
            in_specs=[
                pl.BlockSpec((tme, Dh), lambda j, i, te: (live(i, te), 0)),
                pl.BlockSpec((None, D, tn), lambda j, i, te: (te[i], 0, j)),
                pl.BlockSpec((None, D, tn), lambda j, i, te: (te[i], 0, j)),
            ],
            out_specs=pl.BlockSpec((tme, tn), lambda j, i, te: (i, j)),
        ),
        out_shape=jax.ShapeDtypeStruct((P, F), BF16),
        compiler_params=_cp("arbitrary", "arbitrary"),
        name="moe_expert_up",
    )(tile_table, hs, wg, wu)


def _expert_down_body(te_ref, a_ref, w_ref, y_ref, *, n_tiles):
    @pl.when(pl.program_id(0) < te_ref[n_tiles])
    def _():
        y_ref[...] = jnp.dot(a_ref[...], w_ref[...], preferred_element_type=F32)

    @pl.when(pl.program_id(0) >= te_ref[n_tiles])
    def _():
        y_ref[...] = jnp.zeros_like(y_ref)


def _expert_down(tile_table, act, wd, tme):
    P, F = act.shape
    D = wd.shape[2]
    n_tiles = P // tme
    return pl.pallas_call(
        functools.partial(_expert_down_body, n_tiles=n_tiles),
        grid_spec=pltpu.PrefetchScalarGridSpec(
            num_scalar_prefetch=1,
            grid=(n_tiles,),
            in_specs=[
                pl.BlockSpec((tme, F), lambda i, te: (jnp.minimum(i, te[n_tiles] - 1), 0)),
                pl.BlockSpec((None, F, D), lambda i, te: (te[i], 0, 0)),
            ],
            out_specs=pl.BlockSpec((tme, D), lambda i, te: (i, 0)),
        ),
        out_shape=jax.ShapeDtypeStruct((P, D), F32),
        compiler_params=_cp("arbitrary"),
        name="moe_expert_down",
    )(tile_table, act, wd)


def _combine_body(d1_ref, d2_ref, mf_ref, x_ref, fn_ref, y_ref, o_ref, ya, yb, sem, *, final_norm):
    n = x_ref.shape[0]

    def issue(r, carry):
        pltpu.make_async_copy(y_ref.at[pl.ds(d1_ref[r], 1)], ya.at[pl.ds(r, 1)], sem).start()
        pltpu.make_async_copy(y_ref.at[pl.ds(d2_ref[r], 1)], yb.at[pl.ds(r, 1)], sem).start()
        return carry

    def drain(r, carry):
        row = pltpu.make_async_copy(y_ref.at[pl.ds(0, 1)], ya.at[pl.ds(0, 1)], sem)
        row.wait()
        row.wait()
        return carry

    lax.fori_loop(0, n, issue, 0)
    lax.fori_loop(0, n, drain, 0)

    gates = mf_ref[...]
    eye = lax.broadcasted_iota(I32, (n, n), 0) == lax.broadcasted_iota(I32, (n, n), 1)
    g1 = jnp.sum(jnp.where(eye, gates[0:1, :], 0.0), axis=1, keepdims=True)
    g2 = jnp.sum(jnp.where(eye, gates[1:2, :], 0.0), axis=1, keepdims=True)
    out = x_ref[...] + (g1 * ya[...] + g2 * yb[...])
    if final_norm:
        out = _rms(out, fn_ref[...])
    o_ref[...] = out


def _combine(d1, d2, mf, x2, fnorm, y, final_norm):
    T, D = x2.shape
    tm = _tile(T, 256)
    return pl.pallas_call(
        functools.partial(_combine_body, final_norm=final_norm),
        grid=(T // tm,),
        in_specs=[
            pl.BlockSpec((tm,), lambda i: (i,), memory_space=pltpu.SMEM),
            pl.BlockSpec((tm,), lambda i: (i,), memory_space=pltpu.SMEM),
            pl.BlockSpec((N_EXPERTS, tm), lambda i: (0, i)),
            pl.BlockSpec((tm, D), lambda i: (i, 0)),
            pl.BlockSpec((1, D), lambda i: (0, 0)),
            pl.BlockSpec(memory_space=pl.ANY),
        ],
        out_specs=pl.BlockSpec((tm, D), lambda i: (i, 0)),
        out_shape=jax.ShapeDtypeStruct((T, D), F32),
        scratch_shapes=[pltpu.VMEM((tm, D), F32), pltpu.VMEM((tm, D), F32), pltpu.SemaphoreType.DMA(())],
        compiler_params=_cp("arbitrary"),
        name="moe_combine",
    )(d1, d2, mf, x2, fnorm, y)


def _pad_rows16(w_t):
    pad = jnp.zeros((BF16_SUBLANE_TILE - w_t.shape[0], w_t.shape[1]), F32)
    w = jnp.concatenate([w_t, pad], axis=0)
    hi = w.astype(BF16)
    lo = (w - hi.astype(F32)).astype(BF16)
    return hi, lo


def _attention_layer(x2, B, S, gnorm, w_in, b_forget, w_out):
    sb0 = 3 * HEADS_W + N_HEADS
    col = lambda part: w_in[:, part * HEADS_W:(part + 1) * HEADS_W]
    sbc = lambda part: w_in[:, sb0 + part * HEADS_W:sb0 + (part + 1) * HEADS_W]
    w_k = jnp.concatenate([col(1), sbc(1)], axis=1).astype(BF16)
    w_qvt = jnp.concatenate([col(0), sbc(0), col(2), sbc(2)], axis=1).T.astype(BF16)
    wf_hi, wf_lo = _pad_rows16(w_in[:, 3 * HEADS_W:sb0].T)
    keys, qvt, f16 = _att_in_proj(x2, gnorm, w_k, w_qvt, wf_hi, wf_lo)
    cb = _forget_cumsum(f16, b_forget.reshape(N_HEADS, 1), B, S)
    oa = _fox_attention(keys, qvt, cb, B, S)
    ob = _sb_attention(keys, qvt, B, S)
    return _att_out_proj(oa, ob, w_out.astype(BF16), x2)


def _dense_ffn_layer(x2, gnorm, wg, wu, wd):
    a = _ffn_up(x2, gnorm, wg.astype(BF16), wu.astype(BF16))
    return _ffn_down(a, wd.astype(BF16), x2)


def _conv_layer(x2, S, gnorm, w_in, conv_w, w_out):
    gb, gg = _conv_in(x2, gnorm, w_in.astype(BF16))
    return _conv_out(gb, gg, conv_w, w_out.astype(BF16), x2, S)


def _moe_layer(x2, gnorm, w_router, wg, wu, wd, fnorm, final_norm):
    T, D = x2.shape
    tme = 512 if T >= 4096 else 128
    n_tiles = (2 * T) // tme + N_EXPERTS
    P = n_tiles * tme

    wr_hi, wr_lo = _pad_rows16(w_router.T)
    hp, mi, mf, cnt = _router(x2, gnorm, wr_hi, wr_lo)

    counts = cnt[:, 0].astype(I32)
    padded = ((counts + tme - 1) // tme) * tme
    ends = jnp.cumsum(padded)
    offs = ends - padded
    d1 = jnp.take(offs, mi[0]) + mi[2]
    d2 = jnp.take(offs, mi[1]) + mi[3]
    tile_start = jnp.arange(n_tiles, dtype=I32) * tme
    tile_expert = jnp.sum((tile_start[:, None] >= ends[None, :]).astype(I32), axis=1)
    tile_expert = jnp.minimum(tile_expert, N_EXPERTS - 1)
    tile_table = jnp.concatenate([tile_expert, (ends[-1:] // tme).astype(I32)])

    hs = _dispatch(hp, d1, d2, P)
    act = _expert_up(tile_table, hs, wg.astype(BF16), wu.astype(BF16), tme)
    y = _expert_down(tile_table, act, wd.astype(BF16), tme)
    return _combine(d1, d2, mf, x2, fnorm, y, final_norm)


def kernel(x, mix_norm, ffn_norm, final_norm, w_in_att, b_forget, w_out_att, w_in_conv, conv_w,
           w_out_conv, w_gate_dense, w_up_dense, w_down_dense, w_router, w_gate_moe, w_up_moe,
           w_down_moe):
    B, S, D = x.shape
    depth = mix_norm.shape[0]
    assert depth % 2 == 0, "the final rmsnorm is fused into the last (routed) layer"
    x2 = x.reshape(B * S, D)
    fnorm = final_norm.reshape(1, D)
    for i in range(depth):
        j = i // 2
        mg = mix_norm[i].reshape(1, D)
        fg = ffn_norm[i].reshape(1, D)
        if i % 2 == 0:
            x2 = _attention_layer(x2, B, S, mg, w_in_att[j], b_forget[j], w_out_att[j])
            x2 = _dense_ffn_layer(x2, fg, w_gate_dense[j], w_up_dense[j], w_down_dense[j])
        else:
            x2 = _conv_layer(x2, S, mg, w_in_conv[j], conv_w[j], w_out_conv[j])
            x2 = _moe_layer(x2, fg, w_router[j], w_gate_moe[j], w_up_moe[j], w_down_moe[j],
                            fnorm, final_norm=(i == depth - 1))
    return x2.reshape(B, S, D)
```

```python
import functools
import math

import jax
import jax.numpy as jnp
from jax import lax
from jax.experimental import pallas as pl
from jax.experimental.pallas import tpu as pltpu

F32 = jnp.float32
BF16 = jnp.bfloat16
I32 = jnp.int32
U32 = jnp.uint32

HEAD_DIM = 128
N_HEADS = 8
HEADS_W = N_HEADS * HEAD_DIM
N_EXPERTS = 8
RMS_EPS = 1e-6
CONV_WIDTH = 3
LOG2E = math.log2(math.e)
QK_SCALE_LOG2 = LOG2E / math.sqrt(HEAD_DIM)

V7X_VMEM_LIMIT_BYTES = 56 * 1024 * 1024
LANE = 128
BF16_SUBLANE_TILE = 16
NEG_INF = float("-inf")

_NT = (((1,), (1,)), ((), ()))


def _cp(*sem):
    return pltpu.CompilerParams(dimension_semantics=sem, vmem_limit_bytes=V7X_VMEM_LIMIT_BYTES)


def _tile(n, pref, unit=LANE):
    if n <= pref:
        return n
    t = (pref // unit) * unit
    while t > unit and n % t:
        t -= unit
    assert n % t == 0, (n, pref)
    return t


def _rms(x, g):
    ms = jnp.mean(x * x, axis=-1, keepdims=True)
    return x * lax.rsqrt(ms + RMS_EPS) * g


def _split_bf16(v):
    hi = v.astype(BF16)
    lo = (v - hi.astype(F32)).astype(BF16)
    return hi, lo


def _log_sigmoid_pair(z):
    sp = jnp.log1p(jnp.exp(-jnp.abs(z)))
    return jnp.minimum(z, 0.0) - sp, -jnp.maximum(z, 0.0) - sp


def _silu(g):
    return g / (1.0 + jnp.exp(-g))


def _skinny_nt(wh_ref, wl_ref, hb, hl):
    wh = wh_ref[...]
    out = lax.dot_general(wh, hb, _NT, preferred_element_type=F32)
    out += lax.dot_general(wh, hl, _NT, preferred_element_type=F32)
    out += lax.dot_general(wl_ref[...], hb, _NT, preferred_element_type=F32)
    return out


def _att_in_body(x_ref, g_ref, w_ref, wv_ref, wfh_ref, wfl_ref, o_ref, vt_ref, f_ref, h_scr, *, q_blocks, n_main):
    j = pl.program_id(1)

    @pl.when(j == 0)
    def _():
        h = _rms(x_ref[...], g_ref[...])
        hb, hl = _split_bf16(h)
        h_scr[...] = hb
        f_ref[...] = _skinny_nt(wfh_ref, wfl_ref, hb, hl)

    @pl.when(j < n_main)
    def _():
        o_ref[...] = jnp.dot(h_scr[...], w_ref[...], preferred_element_type=F32).astype(BF16)

    @pl.when(j >= n_main)
    def _():
        acc = lax.dot_general(wv_ref[...], h_scr[...], _NT, preferred_element_type=F32)
        is_q = (j - n_main) < 2 * q_blocks
        vt_ref[...] = (acc * jnp.where(is_q, QK_SCALE_LOG2, 1.0)).astype(BF16)


def _att_in_proj(x2, gnorm, w_k, w_qvt, wf_hi, wf_lo):
    T, D = x2.shape
    N = w_k.shape[1]
    NV = w_qvt.shape[0]
    tm = _tile(T, 1024)
    tn = _tile(HEADS_W, 512)
    n_main = N // tn
    return pl.pallas_call(
        functools.partial(_att_in_body, q_blocks=HEADS_W // tn, n_main=n_main),
        grid=(T // tm, n_main + NV // tn),
        in_specs=[
            pl.BlockSpec((tm, D), lambda i, j: (i, 0)),
            pl.BlockSpec((1, D), lambda i, j: (0, 0)),
            pl.BlockSpec((D, tn), lambda i, j: (0, jnp.minimum(j, n_main - 1))),
            pl.BlockSpec((tn, D), lambda i, j: (jnp.maximum(j - n_main, 0), 0)),
            pl.BlockSpec((BF16_SUBLANE_TILE, D), lambda i, j: (0, 0)),
            pl.BlockSpec((BF16_SUBLANE_TILE, D), lambda i, j: (0, 0)),
        ],
        out_specs=[
            pl.BlockSpec((tm, tn), lambda i, j: (i, jnp.minimum(j, n_main - 1))),
            pl.BlockSpec((tn, tm), lambda i, j: (jnp.maximum(j - n_main, 0), i)),
            pl.BlockSpec((BF16_SUBLANE_TILE, tm), lambda i, j: (0, i)),
        ],
        out_shape=[
            jax.ShapeDtypeStruct((T, N), BF16),
            jax.ShapeDtypeStruct((NV, T), BF16),
            jax.ShapeDtypeStruct((BF16_SUBLANE_TILE, T), F32),
        ],
        scratch_shapes=[pltpu.VMEM((tm, D), BF16)],
        compiler_params=_cp("parallel", "arbitrary"),
        name="att_in_proj",
    )(x2, gnorm, w_k, w_qvt, wf_hi, wf_lo)


def _forget_cumsum_body(f_ref, b_ref, cb_ref):
    z = f_ref[...] + b_ref[...]
    lf, _ = _log_sigmoid_pair(z)
    S = lf.shape[1]
    lane = lax.broadcasted_iota(I32, lf.shape, 1)
    c = lf
    sh = 1
    while sh < S:
        c = c + jnp.where(lane >= sh, pltpu.roll(c, sh, axis=1), 0.0)
        sh *= 2
    c2 = c * LOG2E
    hi = c2.astype(BF16).astype(F32)
    r1 = c2 - hi
    mid = r1.astype(BF16).astype(F32)
    lo = (r1 - mid).astype(BF16).astype(F32)
    pad = jnp.zeros((LANE - 3 * N_HEADS, S), F32)
    cb_ref[...] = jnp.concatenate([hi, mid, lo, pad], axis=0).T.astype(BF16)


def _forget_cumsum(f16, b_col, B, S):
    return pl.pallas_call(
        _forget_cumsum_body,
        grid=(B,),
        in_specs=[
            pl.BlockSpec((N_HEADS, S), lambda b: (0, b)),
            pl.BlockSpec((N_HEADS, 1), lambda b: (0, 0)),
        ],
        out_specs=pl.BlockSpec((S, LANE), lambda b: (b, 0)),
        out_shape=jax.ShapeDtypeStruct((B * S, LANE), BF16),
        compiler_params=_cp("parallel"),
        name="forget_cumsum",
    )(f16, b_col)


def _head_cols(g):
    return slice(g * HEAD_DIM, (g + 1) * HEAD_DIM)


def _fox_body(qt_ref, k_ref, vt_ref, cb_ref, o_ref, *, tq, heads):
    hg = pl.program_id(1)
    i = pl.program_id(2)
    sub = lax.broadcasted_iota(I32, (LANE, tq), 0)
    qs = []
    for g in range(heads):
        h = hg * heads + g
        pick = (sub == h) | (sub == N_HEADS + h) | (sub == 2 * N_HEADS + h)
        qs.append(jnp.concatenate([qt_ref[_head_cols(g), :], jnp.where(pick, -1.0, 0.0).astype(BF16)], axis=0))
    key = lax.broadcasted_iota(I32, (tq, tq), 0)
    qry = lax.broadcasted_iota(I32, (tq, tq), 1)

    def scores(kt):
        ks = pl.multiple_of(kt * tq, tq)
        cb = cb_ref[pl.ds(ks, tq), :]
        out = []
        for g in range(heads):
            k_aug = jnp.concatenate([k_ref[pl.ds(ks, tq), _head_cols(g)], cb], axis=1)
            out.append(jnp.dot(k_aug, qs[g], preferred_element_type=F32))
        return tuple(out)

    def finish(kt, sc, carry, masked):
        ks = pl.multiple_of(kt * tq, tq)
        probs = []
        for g in range(heads):
            m, l, _ = carry[g]
            s = sc[g]
            if masked:
                s = jnp.where(key <= qry, s, NEG_INF)
            m_new = jnp.maximum(m, jnp.max(s, axis=0, keepdims=True))
            alpha = jnp.exp2(m - m_new)
            p = jnp.exp2(s - m_new)
            l = alpha * l + jnp.sum(p, axis=0, keepdims=True)
            probs.append((m_new, l, alpha, p.astype(BF16)))
        out = []
        for g in range(heads):
            m_new, l, alpha, p = probs[g]
            vt = vt_ref[_head_cols(g), pl.ds(ks, tq)]
            acc = alpha * carry[g][2] + jnp.dot(vt, p, preferred_element_type=F32)
            out.append((m_new, l, acc))
        return tuple(out)

    def step(kt, state):
        sc, carry = state
        nxt = scores(kt + 1)
        return nxt, finish(kt, sc, carry, False)

    init = tuple((jnp.full((1, tq), NEG_INF, F32), jnp.zeros((1, tq), F32),
                  jnp.zeros((HEAD_DIM, tq), F32)) for _ in range(heads))
    sc, carry = lax.fori_loop(0, i, step, (scores(0), init))
    carry = finish(i, sc, carry, True)
    for g in range(heads):
        _, l, acc = carry[g]
        o_ref[:, _head_cols(g)] = (acc / l).T.astype(BF16)


ATT_TILE = 256
ATT_HEADS_PER_STEP = 4


def _fox_attention(keys, qvt, cb, B, S):
    T = B * S
    tq = _tile(S, ATT_TILE)
    nq = S // tq
    hp = ATT_HEADS_PER_STEP
    ng = N_HEADS // hp
    w = hp * HEAD_DIM
    return pl.pallas_call(
        functools.partial(_fox_body, tq=tq, heads=hp),
        grid=(B, ng, nq),
        in_specs=[
            pl.BlockSpec((w, tq), lambda b, h, i: (h, b * nq + i)),
            pl.BlockSpec((S, w), lambda b, h, i: (b, h)),
            pl.BlockSpec((w, S), lambda b, h, i: (2 * ng + h, b)),
            pl.BlockSpec((S, LANE), lambda b, h, i: (b, 0)),
        ],
        out_specs=pl.BlockSpec((tq, w), lambda b, h, i: (b * nq + i, h)),
        out_shape=jax.ShapeDtypeStruct((T, HEADS_W), BF16),
        compiler_params=_cp("parallel", "parallel", "arbitrary"),
        name="fox_attention",
    )(qvt, keys, qvt, cb)


F32_EXP2_UNDERFLOW = -150.0


def _sb_body(qt_ref, k_ref, vt_ref, ut_ref, o_ref, *, tq, heads):
    i = pl.program_id(2)
    qs = [qt_ref[_head_cols(g), :] for g in range(heads)]
    ut = ut_ref[...]
    key = lax.broadcasted_iota(I32, (tq, tq), 0)
    qry = lax.broadcasted_iota(I32, (tq, tq), 1)
    strict = key < qry

    def scores(kt):
        ks = pl.multiple_of(kt * tq, tq)
        return tuple(jnp.dot(k_ref[pl.ds(ks, tq), _head_cols(g)], qs[g], preferred_element_type=F32)
                     for g in range(heads))

    def finish(kt, sc, carry, masked):
        ks = pl.multiple_of(kt * tq, tq)
        mid = []
        for g in range(heads):
            z = sc[g]
            log_beta = jnp.minimum(z, 0.0) - jnp.log2(1.0 + jnp.exp2(-jnp.abs(z)))
            log_om = log_beta - z
            if masked:
                log_om = jnp.where(strict, log_om, 0.0)
            hi, lo = _split_bf16(log_om)
            e = jnp.dot(ut, hi, preferred_element_type=F32) + jnp.dot(ut, lo, preferred_element_type=F32)
            mid.append((log_beta, log_om, e))
        out = []
        for g in range(heads):
            r_sum, acc = carry[g]
            log_beta, log_om, e = mid[g]
            a = jnp.exp2(log_beta + e + r_sum)
            if masked:
                a = jnp.where(strict, a, 0.0)
            vt = vt_ref[_head_cols(g), pl.ds(ks, tq)]
            acc = acc + jnp.dot(vt, a.astype(BF16), preferred_element_type=F32)
            out.append((r_sum + jnp.sum(log_om, axis=0, keepdims=True), acc))
        return tuple(out)

    def live(carry):
        top = carry[0][0]
        for g in range(1, heads):
            top = jnp.maximum(top, carry[g][0])
        return (jnp.max(top) > F32_EXP2_UNDERFLOW).astype(I32)

    init = tuple((jnp.zeros((1, tq), F32), jnp.zeros((HEAD_DIM, tq), F32)) for _ in range(heads))
    carry = finish(i, scores(i), init, True)

    def step(state):
        n, _, sc, carry = state
        kt = i - 1 - n
        nxt = scores(jnp.maximum(kt - 1, 0))
        carry = finish(kt, sc, carry, False)
        return n + 1, live(carry), nxt, carry

    state = (jnp.int32(0), live(carry), scores(jnp.maximum(i - 1, 0)), carry)
    _, _, _, carry = lax.while_loop(lambda st: (st[0] < i) & (st[1] > 0), step, state)
    for g in range(heads):
        o_ref[:, _head_cols(g)] = carry[g][1].T.astype(BF16)


def _sb_attention(keys, qvt, B, S):
    T = B * S
    tq = _tile(S, ATT_TILE)
    nq = S // tq
    hp = ATT_HEADS_PER_STEP
    ng = N_HEADS // hp
    w = hp * HEAD_DIM
    r = lax.broadcasted_iota(I32, (tq, tq), 0)
    c = lax.broadcasted_iota(I32, (tq, tq), 1)
    ut = (c > r).astype(BF16)
    return pl.pallas_call(
        functools.partial(_sb_body, tq=tq, heads=hp),
        grid=(B, ng, nq),
        in_specs=[
            pl.BlockSpec((w, tq), lambda b, h, i: (ng + h, b * nq + i)),
            pl.BlockSpec((S, w), lambda b, h, i: (b, ng + h)),
            pl.BlockSpec((w, S), lambda b, h, i: (3 * ng + h, b)),
            pl.BlockSpec((tq, tq), lambda b, h, i: (0, 0)),
        ],
        out_specs=pl.BlockSpec((tq, w), lambda b, h, i: (b * nq + i, h)),
        out_shape=jax.ShapeDtypeStruct((T, HEADS_W), BF16),
        compiler_params=_cp("parallel", "parallel", "arbitrary"),
        name="sb_attention",
    )(qvt, keys, qvt, ut)


def _att_out_body(oa_ref, ob_ref, wa_ref, wb_ref, x_ref, o_ref):
    acc = jnp.dot(oa_ref[...], wa_ref[...].astype(BF16), preferred_element_type=F32)
    acc += jnp.dot(ob_ref[...], wb_ref[...].astype(BF16), preferred_element_type=F32)
    o_ref[...] = x_ref[...] + acc


def _att_out_proj(oa, ob, w_out, x2):
    T, D = x2.shape
    tm = _tile(T, 1024)
    tn = _tile(D, 512)
    return pl.pallas_call(
        _att_out_body,
        grid=(T // tm, D // tn),
        in_specs=[
            pl.BlockSpec((tm, HEADS_W), lambda i, j: (i, 0)),
            pl.BlockSpec((tm, HEADS_W), lambda i, j: (i, 0)),
            pl.BlockSpec((HEADS_W, tn), lambda i, j: (0, j)),
            pl.BlockSpec((HEADS_W, tn), lambda i, j: (1, j)),
            pl.BlockSpec((tm, tn), lambda i, j: (i, j)),
        ],
        out_specs=pl.BlockSpec((tm, tn), lambda i, j: (i, j)),
        out_shape=jax.ShapeDtypeStruct((T, D), F32),
        compiler_params=_cp("parallel", "parallel"),
        name="att_out_proj",
    )(oa, ob, w_out, w_out, x2)


def _ffn_up_body(x_ref, g_ref, wg_ref, wu_ref, a_ref, h_scr):
    @pl.when(pl.program_id(1) == 0)
    def _():
        h_scr[...] = _rms(x_ref[...], g_ref[...]).astype(BF16)

    h = h_scr[...]
    g = jnp.dot(h, wg_ref[...].astype(BF16), preferred_element_type=F32)
    u = jnp.dot(h, wu_ref[...].astype(BF16), preferred_element_type=F32)
    a_ref[...] = (_silu(g) * u).astype(BF16)


def _ffn_up(x2, gnorm, wg, wu):
    T, D = x2.shape
    F = wg.shape[1]
    tm = _tile(T, 1024)
    tn = _tile(F, 512)
    return pl.pallas_call(
        _ffn_up_body,
        grid=(T // tm, F // tn),
        in_specs=[
            pl.BlockSpec((tm, D), lambda i, j: (i, 0)),
            pl.BlockSpec((1, D), lambda i, j: (0, 0)),
            pl.BlockSpec((D, tn), lambda i, j: (0, j)),
            pl.BlockSpec((D, tn), lambda i, j: (0, j)),
        ],
        out_specs=pl.BlockSpec((tm, tn), lambda i, j: (i, j)),
        out_shape=jax.ShapeDtypeStruct((T, F), BF16),
        scratch_shapes=[pltpu.VMEM((tm, D), BF16)],
        compiler_params=_cp("parallel", "arbitrary"),
        name="ffn_up",
    )(x2, gnorm, wg, wu)


def _ffn_down_body(a_ref, w_ref, x_ref, o_ref):
    o_ref[...] = x_ref[...] + jnp.dot(a_ref[...], w_ref[...], preferred_element_type=F32)


def _ffn_down(a, wd, x2):
    T, D = x2.shape
    F = a.shape[1]
    tm = _tile(T, 1024)
    tn = _tile(D, 512)
    return pl.pallas_call(
        _ffn_down_body,
        grid=(T // tm, D // tn),
        in_specs=[
            pl.BlockSpec((tm, F), lambda i, j: (i, 0)),
            pl.BlockSpec((F, tn), lambda i, j: (0, j)),
            pl.BlockSpec((tm, tn), lambda i, j: (i, j)),
        ],
        out_specs=pl.BlockSpec((tm, tn), lambda i, j: (i, j)),
        out_shape=jax.ShapeDtypeStruct((T, D), F32),
        compiler_params=_cp("parallel", "parallel"),
        name="ffn_down",
    )(a, wd, x2)


def _conv_in_body(x_ref, g_ref, wb_ref, wc_ref, wu_ref, gb_ref, gg_ref, h_scr):
    @pl.when(pl.program_id(1) == 0)
    def _():
        h_scr[...] = _rms(x_ref[...], g_ref[...]).astype(BF16)

    h = h_scr[...]
    gb_ref[...] = jnp.dot(h, wb_ref[...], preferred_element_type=F32).astype(BF16)
    c = jnp.dot(h, wc_ref[...], preferred_element_type=F32)
    u = jnp.dot(h, wu_ref[...], preferred_element_type=F32)
    gg_ref[...] = (c * u).astype(BF16)


def _conv_in(x2, gnorm, w_in):
    T, D = x2.shape
    tm = _tile(T, 1024)
    tn = _tile(D, 512)
    nd = D // tn
    return pl.pallas_call(
        _conv_in_body,
        grid=(T // tm, nd),
        in_specs=[
            pl.BlockSpec((tm, D), lambda i, j: (i, 0)),
            pl.BlockSpec((1, D), lambda i, j: (0, 0)),
            pl.BlockSpec((D, tn), lambda i, j: (0, j)),
            pl.BlockSpec((D, tn), lambda i, j: (0, nd + j)),
            pl.BlockSpec((D, tn), lambda i, j: (0, 2 * nd + j)),
        ],
        out_specs=[
            pl.BlockSpec((tm, tn), lambda i, j: (i, j)),
            pl.BlockSpec((tm, tn), lambda i, j: (i, j)),
        ],
        out_shape=[jax.ShapeDtypeStruct((T, D), BF16), jax.ShapeDtypeStruct((T, D), BF16)],
        scratch_shapes=[pltpu.VMEM((tm, D), BF16)],
        compiler_params=_cp("parallel", "arbitrary"),
        name="conv_in",
    )(x2, gnorm, w_in, w_in, w_in)


def _conv_out_body(gb_ref, g_ref, gp_ref, cw_ref, w_ref, x_ref, o_ref, *, tiles_per_seq):
    i = pl.program_id(0)
    g = g_ref[...].astype(F32)
    tm = g.shape[0]
    keep = jnp.where(i % tiles_per_seq == 0, 0.0, 1.0)
    prev = gp_ref[...].astype(F32) * keep
    p1 = prev[BF16_SUBLANE_TILE - 1:BF16_SUBLANE_TILE, :]
    p2 = prev[BF16_SUBLANE_TILE - 2:BF16_SUBLANE_TILE - 1, :]
    row = lax.broadcasted_iota(I32, g.shape, 0)
    g1 = jnp.where(row == 0, p1, pltpu.roll(g, 1, axis=0))
    g2 = jnp.where(row == 0, p2, jnp.where(row == 1, p1, pltpu.roll(g, 2, axis=0)))
    cw = cw_ref[...]
    conv = g2 * cw[0:1, :] + g1 * cw[1:2, :] + g * cw[2:3, :]
    y = (gb_ref[...].astype(F32) * conv).astype(BF16)
    o_ref[...] = x_ref[...] + jnp.dot(y, w_ref[...], preferred_element_type=F32)


def _conv_out(gb, gg, conv_w, w_out, x2, S):
    T, D = x2.shape
    tm = _tile(S, 256)
    pt = BF16_SUBLANE_TILE
    return pl.pallas_call(
        functools.partial(_conv_out_body, tiles_per_seq=S // tm),
        grid=(T // tm,),
        in_specs=[
            pl.BlockSpec((tm, D), lambda i: (i, 0)),
            pl.BlockSpec((tm, D), lambda i: (i, 0)),
            pl.BlockSpec((pt, D), lambda i: (jnp.maximum(i * (tm // pt) - 1, 0), 0)),
            pl.BlockSpec((CONV_WIDTH, D), lambda i: (0, 0)),
            pl.BlockSpec((D, D), lambda i: (0, 0)),
            pl.BlockSpec((tm, D), lambda i: (i, 0)),
        ],
        out_specs=pl.BlockSpec((tm, D), lambda i: (i, 0)),
        out_shape=jax.ShapeDtypeStruct((T, D), F32),
        compiler_params=_cp("parallel"),
        name="conv_out",
    )(gb, gg, gg, conv_w, w_out, x2)


def _router_body(x_ref, g_ref, wrh_ref, wrl_ref, tri_ref, hp_ref, mi_ref, mf_ref, cnt_ref, carry):
    @pl.when(pl.program_id(0) == 0)
    def _():
        carry[...] = jnp.zeros_like(carry)

    h = _rms(x_ref[...], g_ref[...])
    hb, hl = _split_bf16(h)
    bits = pltpu.bitcast(hb.astype(F32), U32)
    half = bits.shape[1] // 2
    hp_ref[...] = (bits[:, :half] >> 16) | bits[:, half:]

    logits = _skinny_nt(wrh_ref, wrl_ref, hb, hl)[:N_EXPERTS]
    eidx = lax.broadcasted_iota(I32, logits.shape, 0).astype(F32)
    ne = float(N_EXPERTS)
    m1 = jnp.max(logits, axis=0, keepdims=True)
    i1 = jnp.min(jnp.where(logits == m1, eidx, ne), axis=0, keepdims=True)
    rest = jnp.where(eidx == i1, NEG_INF, logits)
    m2 = jnp.max(rest, axis=0, keepdims=True)
    i2 = jnp.min(jnp.where(rest == m2, eidx, ne), axis=0, keepdims=True)
    e21 = jnp.exp(m2 - m1)
    g1 = 1.0 / (1.0 + e21)
    g2 = e21 * g1

    sel = jnp.where((eidx == i1) | (eidx == i2), 1.0, 0.0)
    incl = jnp.dot(sel, tri_ref[...], preferred_element_type=F32)
    pos = carry[...] + incl - sel
    carry[...] = carry[...] + jnp.sum(sel, axis=1, keepdims=True)
    p1 = jnp.sum(jnp.where(eidx == i1, pos, 0.0), axis=0, keepdims=True)
    p2 = jnp.sum(jnp.where(eidx == i2, pos, 0.0), axis=0, keepdims=True)
    meta = jnp.where(eidx == 0, i1, jnp.where(eidx == 1, i2, jnp.where(eidx == 2, p1, jnp.where(eidx == 3, p2, 0.0))))
    mi_ref[...] = meta.astype(I32)
    mf_ref[...] = jnp.where(eidx == 0, g1, jnp.where(eidx == 1, g2, 0.0))
    cnt_ref[...] = jnp.broadcast_to(carry[...], cnt_ref.shape)


def _router(x2, gnorm, wr_hi, wr_lo):
    T, D = x2.shape
    tm = _tile(T, 512)
    r = lax.broadcasted_iota(I32, (tm, tm), 0)
    c = lax.broadcasted_iota(I32, (tm, tm), 1)
    tri = (r <= c).astype(F32)
    return pl.pallas_call(
        _router_body,
        grid=(T // tm,),
        in_specs=[
            pl.BlockSpec((tm, D), lambda i: (i, 0)),
            pl.BlockSpec((1, D), lambda i: (0, 0)),
            pl.BlockSpec((BF16_SUBLANE_TILE, D), lambda i: (0, 0)),
            pl.BlockSpec((BF16_SUBLANE_TILE, D), lambda i: (0, 0)),
            pl.BlockSpec((tm, tm), lambda i: (0, 0)),
        ],
        out_specs=[
            pl.BlockSpec((tm, D // 2), lambda i: (i, 0)),
            pl.BlockSpec((N_EXPERTS, tm), lambda i: (0, i)),
            pl.BlockSpec((N_EXPERTS, tm), lambda i: (0, i)),
            pl.BlockSpec((N_EXPERTS, LANE), lambda i: (0, 0)),
        ],
        out_shape=[
            jax.ShapeDtypeStruct((T, D // 2), U32),
            jax.ShapeDtypeStruct((N_EXPERTS, T), I32),
            jax.ShapeDtypeStruct((N_EXPERTS, T), F32),
            jax.ShapeDtypeStruct((N_EXPERTS, LANE), F32),
        ],
        scratch_shapes=[pltpu.VMEM((N_EXPERTS, 1), F32)],
        compiler_params=_cp("arbitrary"),
        name="moe_router",
    )(x2, gnorm, wr_hi, wr_lo, tri)


ROW_DMA_UNROLL = 8


def _dispatch_body(d1_ref, d2_ref, hp_ref, zero_ref, hs_ref, stage, sems):
    del zero_ref
    i = pl.program_id(0)
    n = hp_ref.shape[0]
    slot = i % 2
    stage[slot] = hp_ref[...]

    def issue(blk, carry):
        for u in range(ROW_DMA_UNROLL):
            r = blk * ROW_DMA_UNROLL + u
            src = stage.at[slot, pl.ds(r, 1)]
            pltpu.make_async_copy(src, hs_ref.at[pl.ds(d1_ref[r], 1)], sems.at[slot]).start()
            pltpu.make_async_copy(src, hs_ref.at[pl.ds(d2_ref[r], 1)], sems.at[slot]).start()
        return carry

    def drain(which):
        def body(blk, carry):
            row = pltpu.make_async_copy(stage.at[which, pl.ds(0, 1)], hs_ref.at[pl.ds(0, 1)], sems.at[which])
            for _ in range(2 * ROW_DMA_UNROLL):
                row.wait()
            return carry
        lax.fori_loop(0, n // ROW_DMA_UNROLL, body, 0)

    lax.fori_loop(0, n // ROW_DMA_UNROLL, issue, 0)

    @pl.when(i > 0)
    def _():
        drain(1 - slot)

    @pl.when(i == pl.num_programs(0) - 1)
    def _():
        drain(slot)


def _dispatch(hp, d1, d2, P):
    T, Dh = hp.shape
    tm = _tile(T, 256)
    zeros = jnp.zeros((P, Dh), U32)
    return pl.pallas_call(
        _dispatch_body,
        grid=(T // tm,),
        in_specs=[
            pl.BlockSpec((tm,), lambda i: (i,), memory_space=pltpu.SMEM),
            pl.BlockSpec((tm,), lambda i: (i,), memory_space=pltpu.SMEM),
            pl.BlockSpec((tm, Dh), lambda i: (i, 0)),
            pl.BlockSpec(memory_space=pl.ANY),
        ],
        out_specs=pl.BlockSpec(memory_space=pl.ANY),
        out_shape=jax.ShapeDtypeStruct((P, Dh), U32),
        scratch_shapes=[pltpu.VMEM((2, tm, Dh), U32), pltpu.SemaphoreType.DMA((2,))],
        input_output_aliases={3: 0},
        compiler_params=_cp("arbitrary"),
        name="moe_dispatch",
    )(d1, d2, hp, zeros)


def _unpack_rows(words):
    lo = pltpu.bitcast(words << 16, F32)
    hi = pltpu.bitcast(words & jnp.uint32(0xFFFF0000), F32)
    return jnp.concatenate([lo, hi], axis=1).astype(BF16)


def _expert_up_body(te_ref, hs_ref, wg_ref, wu_ref, a_ref, *, n_tiles):
    @pl.when(pl.program_id(1) < te_ref[n_tiles])
    def _():
        h = _unpack_rows(hs_ref[...])
        g = jnp.dot(h, wg_ref[...], preferred_element_type=F32)
        u = jnp.dot(h, wu_ref[...], preferred_element_type=F32)
        a_ref[...] = (_silu(g) * u).astype(BF16)

    @pl.when(pl.program_id(1) >= te_ref[n_tiles])
    def _():
        a_ref[...] = jnp.zeros_like(a_ref)


def _expert_up(tile_table, hs, wg, wu, tme):
    P, Dh = hs.shape
    _, D, F = wg.shape
    tn = F // 2 if (F // 2) % LANE == 0 else F
    n_tiles = P // tme
    live = lambda i, te: jnp.minimum(i, te[n_tiles] - 1)
    return pl.pallas_call(
        functools.partial(_expert_up_body, n_tiles=n_tiles),
        grid_spec=pltpu.PrefetchScalarGridSpec(
            num_scalar_prefetch=1,
            grid=(F // tn, n_tiles),
            in_specs=[
                pl.BlockSpec((tme, Dh), lambda j, i, te: (live(i, te), 0)),
                pl.BlockSpec((None, D, tn), lambda j, i, te: (te[i], 0, j)),
                pl.BlockSpec((None, D, tn), lambda j, i, te: (te[i], 0, j)),
            ],
            out_specs=pl.BlockSpec((tme, tn), lambda j, i, te: (i, j)),
        ),
        out_shape=jax.ShapeDtypeStruct((P, F), BF16),
        compiler_params=_cp("arbitrary", "arbitrary"),
        name="moe_expert_up",
    )(tile_table, hs, wg, wu)


def _expert_down_body(te_ref, a_ref, w_ref, y_ref, *, n_tiles):
    @pl.when(pl.program_id(0) < te_ref[n_tiles])
    def _():
        y_ref[...] = jnp.dot(a_ref[...], w_ref[...], preferred_element_type=F32)

    @pl.when(pl.program_id(0) >= te_ref[n_tiles])
    def _():
        y_ref[...] = jnp.zeros_like(y_ref)


def _expert_down(tile_table, act, wd, tme):
    P, F = act.shape
    D = wd.shape[2]
    n_tiles = P // tme
    return pl.pallas_call(
        functools.partial(_expert_down_body, n_tiles=n_tiles),
        grid_spec=pltpu.PrefetchScalarGridSpec(
            num_scalar_prefetch=1,
            grid=(n_tiles,),
            in_specs=[
                pl.BlockSpec((tme, F), lambda i, te: (jnp.minimum(i, te[n_tiles] - 1), 0)),
                pl.BlockSpec((None, F, D), lambda i, te: (te[i], 0, 0)),
            ],
            out_specs=pl.BlockSpec((tme, D), lambda i, te: (i, 0)),
        ),
        out_shape=jax.ShapeDtypeStruct((P, D), F32),
        compiler_params=_cp("arbitrary"),
        name="moe_expert_down",
    )(tile_table, act, wd)


def _combine_body(d1_ref, d2_ref, d1n_ref, d2n_ref, mf_ref, x_ref, fn_ref, y_ref, o_ref, ya, yb, sems,
                  *, final_norm):
    i = pl.program_id(0)
    n = x_ref.shape[0]
    slot = i % 2

    def fetch(ia_ref, ib_ref, to):
        def issue(blk, carry):
            for u in range(ROW_DMA_UNROLL):
                r = blk * ROW_DMA_UNROLL + u
                pltpu.make_async_copy(y_ref.at[pl.ds(ia_ref[r], 1)], ya.at[to, pl.ds(r, 1)], sems.at[to]).start()
                pltpu.make_async_copy(y_ref.at[pl.ds(ib_ref[r], 1)], yb.at[to, pl.ds(r, 1)], sems.at[to]).start()
            return carry
        lax.fori_loop(0, n // ROW_DMA_UNROLL, issue, 0)

    @pl.when(i == 0)
    def _():
        fetch(d1_ref, d2_ref, 0)

    @pl.when(i < pl.num_programs(0) - 1)
    def _():
        fetch(d1n_ref, d2n_ref, 1 - slot)

    def drain(blk, carry):
        row = pltpu.make_async_copy(y_ref.at[pl.ds(0, 1)], ya.at[slot, pl.ds(0, 1)], sems.at[slot])
        for _ in range(2 * ROW_DMA_UNROLL):
            row.wait()
        return carry

    lax.fori_loop(0, n // ROW_DMA_UNROLL, drain, 0)

    gates = mf_ref[...]
    eye = lax.broadcasted_iota(I32, (n, n), 0) == lax.broadcasted_iota(I32, (n, n), 1)
    g1 = jnp.sum(jnp.where(eye, gates[0:1, :], 0.0), axis=1, keepdims=True)
    g2 = jnp.sum(jnp.where(eye, gates[1:2, :], 0.0), axis=1, keepdims=True)
    out = x_ref[...] + (g1 * ya[slot] + g2 * yb[slot])
    if final_norm:
        out = _rms(out, fn_ref[...])
    o_ref[...] = out


def _combine(d1, d2, mf, x2, fnorm, y, final_norm):
    T, D = x2.shape
    tm = _tile(T, 256)
    last = T // tm - 1
    cur = lambda i: (i,)
    nxt = lambda i: (jnp.minimum(i + 1, last),)
    return pl.pallas_call(
        functools.partial(_combine_body, final_norm=final_norm),
        grid=(T // tm,),
        in_specs=[
            pl.BlockSpec((tm,), cur, memory_space=pltpu.SMEM),
            pl.BlockSpec((tm,), cur, memory_space=pltpu.SMEM),
            pl.BlockSpec((tm,), nxt, memory_space=pltpu.SMEM),
            pl.BlockSpec((tm,), nxt, memory_space=pltpu.SMEM),
            pl.BlockSpec((N_EXPERTS, tm), lambda i: (0, i)),
            pl.BlockSpec((tm, D), lambda i: (i, 0)),
            pl.BlockSpec((1, D), lambda i: (0, 0)),
            pl.BlockSpec(memory_space=pl.ANY),
        ],
        out_specs=pl.BlockSpec((tm, D), lambda i: (i, 0)),
        out_shape=jax.ShapeDtypeStruct((T, D), F32),
        scratch_shapes=[pltpu.VMEM((2, tm, D), F32), pltpu.VMEM((2, tm, D), F32),
                        pltpu.SemaphoreType.DMA((2,))],
        compiler_params=_cp("arbitrary"),
        name="moe_combine",
    )(d1, d2, d1, d2, mf, x2, fnorm, y)


def _pad_rows16(w_t):
    pad = jnp.zeros((BF16_SUBLANE_TILE - w_t.shape[0], w_t.shape[1]), F32)
    w = jnp.concatenate([w_t, pad], axis=0)
    hi = w.astype(BF16)
    lo = (w - hi.astype(F32)).astype(BF16)
    return hi, lo


def _attention_layer(x2, B, S, gnorm, w_in, b_forget, w_out):
    sb0 = 3 * HEADS_W + N_HEADS
    col = lambda part: w_in[:, part * HEADS_W:(part + 1) * HEADS_W]
    sbc = lambda part: w_in[:, sb0 + part * HEADS_W:sb0 + (part + 1) * HEADS_W]
    w_k = jnp.concatenate([col(1), sbc(1)], axis=1).astype(BF16)
    w_qvt = jnp.concatenate([col(0), sbc(0), col(2), sbc(2)], axis=1).T.astype(BF16)
    wf_hi, wf_lo = _pad_rows16(w_in[:, 3 * HEADS_W:sb0].T)
    keys, qvt, f16 = _att_in_proj(x2, gnorm, w_k, w_qvt, wf_hi, wf_lo)
    cb = _forget_cumsum(f16, b_forget.reshape(N_HEADS, 1), B, S)
    oa = _fox_attention(keys, qvt, cb, B, S)
    ob = _sb_attention(keys, qvt, B, S)
    return _att_out_proj(oa, ob, w_out, x2)


def _dense_ffn_layer(x2, gnorm, wg, wu, wd):
    a = _ffn_up(x2, gnorm, wg, wu)
    return _ffn_down(a, wd.astype(BF16), x2)


def _conv_layer(x2, S, gnorm, w_in, conv_w, w_out):
    gb, gg = _conv_in(x2, gnorm, w_in.astype(BF16))
    return _conv_out(gb, gg, conv_w, w_out.astype(BF16), x2, S)


def _moe_layer(x2, gnorm, w_router, wg, wu, wd, fnorm, final_norm):
    T, D = x2.shape
    tme = 512 if T >= 4096 else 128
    n_tiles = (2 * T) // tme + N_EXPERTS
    P = n_tiles * tme

    wr_hi, wr_lo = _pad_rows16(w_router.T)
    hp, mi, mf, cnt = _router(x2, gnorm, wr_hi, wr_lo)

    counts = cnt[:, 0].astype(I32)
    padded = ((counts + tme - 1) // tme) * tme
    ends = jnp.cumsum(padded)
    offs = ends - padded
    d1 = jnp.take(offs, mi[0]) + mi[2]
    d2 = jnp.take(offs, mi[1]) + mi[3]
    tile_start = jnp.arange(n_tiles, dtype=I32) * tme
    tile_expert = jnp.sum((tile_start[:, None] >= ends[None, :]).astype(I32), axis=1)
    tile_expert = jnp.minimum(tile_expert, N_EXPERTS - 1)
    tile_table = jnp.concatenate([tile_expert, (ends[-1:] // tme).astype(I32)])

    hs = _dispatch(hp, d1, d2, P)
    act = _expert_up(tile_table, hs, wg.astype(BF16), wu.astype(BF16), tme)
    y = _expert_down(tile_table, act, wd.astype(BF16), tme)
    return _combine(d1, d2, mf, x2, fnorm, y, final_norm)


def kernel(x, mix_norm, ffn_norm, final_norm, w_in_att, b_forget, w_out_att, w_in_conv, conv_w,
           w_out_conv, w_gate_dense, w_up_dense, w_down_dense, w_router, w_gate_moe, w_up_moe,
           w_down_moe):
    B, S, D = x.shape
    depth = mix_norm.shape[0]
    assert depth % 2 == 0, "the final rmsnorm is fused into the last (routed) layer"
    x2 = x.reshape(B * S, D)
    fnorm = final_norm.reshape(1, D)
    for i in range(depth):
        j = i // 2
        mg = mix_norm[i].reshape(1, D)
        fg = ffn_norm[i].reshape(1, D)
        if i % 2 == 0:
            x2 = _attention_layer(x2, B, S, mg, w_in_att[j], b_forget[j], w_out_att[j])
            x2 = _dense_ffn_layer(x2, fg, w_gate_dense[j], w_up_dense[j], w_down_dense[j])
        else:
            x2 = _conv_layer(x2, S, mg, w_in_conv[j], conv_w[j], w_out_conv[j])
            x2 = _moe_layer(x2, fg, w_router[j], w_gate_moe[j], w_up_moe[j], w_down_moe[j],
                            fnorm, final_norm=(i == depth - 1))
    return x2.reshape(B, S, D)
```

```python
import functools
import math

import jax
import jax.numpy as jnp
from jax import lax
from jax.experimental import pallas as pl
from jax.experimental.pallas import tpu as pltpu

F32 = jnp.float32
BF16 = jnp.bfloat16
I32 = jnp.int32
U32 = jnp.uint32

HEAD_DIM = 128
N_HEADS = 8
HEADS_W = N_HEADS * HEAD_DIM
N_EXPERTS = 8
RMS_EPS = 1e-6
CONV_WIDTH = 3
LOG2E = math.log2(math.e)
QK_SCALE_LOG2 = LOG2E / math.sqrt(HEAD_DIM)

V7X_VMEM_LIMIT_BYTES = 56 * 1024 * 1024
LANE = 128
BF16_SUBLANE_TILE = 16
NEG_INF = float("-inf")

_NT = (((1,), (1,)), ((), ()))


def _cp(*sem):
    return pltpu.CompilerParams(dimension_semantics=sem, vmem_limit_bytes=V7X_VMEM_LIMIT_BYTES)


def _tile(n, pref, unit=LANE):
    if n <= pref:
        return n
    t = (pref // unit) * unit
    while t > unit and n % t:
        t -= unit
    assert n % t == 0, (n, pref)
    return t


def _rms(x, g):
    ms = jnp.mean(x * x, axis=-1, keepdims=True)
    return x * lax.rsqrt(ms + RMS_EPS) * g


def _split_bf16(v):
    hi = v.astype(BF16)
    lo = (v - hi.astype(F32)).astype(BF16)
    return hi, lo


def _log_sigmoid_pair(z):
    sp = jnp.log1p(jnp.exp(-jnp.abs(z)))
    return jnp.minimum(z, 0.0) - sp, -jnp.maximum(z, 0.0) - sp


def _silu(g):
    return g / (1.0 + jnp.exp(-g))


def _skinny_nt(wh_ref, wl_ref, hb, hl):
    wh = wh_ref[...]
    out = lax.dot_general(wh, hb, _NT, preferred_element_type=F32)
    out += lax.dot_general(wh, hl, _NT, preferred_element_type=F32)
    out += lax.dot_general(wl_ref[...], hb, _NT, preferred_element_type=F32)
    return out


def _att_in_body(x_ref, g_ref, w_ref, wv_ref, wfh_ref, wfl_ref, o_ref, vt_ref, f_ref, h_scr, *, q_blocks, n_main):
    j = pl.program_id(1)

    @pl.when(j == 0)
    def _():
        h = _rms(x_ref[...], g_ref[...])
        hb, hl = _split_bf16(h)
        h_scr[...] = hb
        f_ref[...] = _skinny_nt(wfh_ref, wfl_ref, hb, hl)

    @pl.when(j < n_main)
    def _():
        o_ref[...] = jnp.dot(h_scr[...], w_ref[...], preferred_element_type=F32).astype(BF16)

    @pl.when(j >= n_main)
    def _():
        acc = lax.dot_general(wv_ref[...], h_scr[...], _NT, preferred_element_type=F32)
        is_q = (j - n_main) < 2 * q_blocks
        vt_ref[...] = (acc * jnp.where(is_q, QK_SCALE_LOG2, 1.0)).astype(BF16)


def _att_in_proj(x2, gnorm, w_k, w_qvt, wf_hi, wf_lo):
    T, D = x2.shape
    N = w_k.shape[1]
    NV = w_qvt.shape[0]
    tm = _tile(T, 1024)
    tn = _tile(HEADS_W, 1024)
    n_main = N // tn
    return pl.pallas_call(
        functools.partial(_att_in_body, q_blocks=HEADS_W // tn, n_main=n_main),
        grid=(T // tm, n_main + NV // tn),
        in_specs=[
            pl.BlockSpec((tm, D), lambda i, j: (i, 0)),
            pl.BlockSpec((1, D), lambda i, j: (0, 0)),
            pl.BlockSpec((D, tn), lambda i, j: (0, jnp.minimum(j, n_main - 1))),
            pl.BlockSpec((tn, D), lambda i, j: (jnp.maximum(j - n_main, 0), 0)),
            pl.BlockSpec((BF16_SUBLANE_TILE, D), lambda i, j: (0, 0)),
            pl.BlockSpec((BF16_SUBLANE_TILE, D), lambda i, j: (0, 0)),
        ],
        out_specs=[
            pl.BlockSpec((tm, tn), lambda i, j: (i, jnp.minimum(j, n_main - 1))),
            pl.BlockSpec((tn, tm), lambda i, j: (jnp.maximum(j - n_main, 0), i)),
            pl.BlockSpec((BF16_SUBLANE_TILE, tm), lambda i, j: (0, i)),
        ],
        out_shape=[
            jax.ShapeDtypeStruct((T, N), BF16),
            jax.ShapeDtypeStruct((NV, T), BF16),
            jax.ShapeDtypeStruct((BF16_SUBLANE_TILE, T), F32),
        ],
        scratch_shapes=[pltpu.VMEM((tm, D), BF16)],
        compiler_params=_cp("parallel", "arbitrary"),
        name="att_in_proj",
    )(x2, gnorm, w_k, w_qvt, wf_hi, wf_lo)


def _forget_cumsum_body(f_ref, b_ref, cb_ref):
    z = f_ref[...] + b_ref[...]
    lf, _ = _log_sigmoid_pair(z)
    S = lf.shape[1]
    lane = lax.broadcasted_iota(I32, lf.shape, 1)
    c = lf
    sh = 1
    while sh < S:
        c = c + jnp.where(lane >= sh, pltpu.roll(c, sh, axis=1), 0.0)
        sh *= 2
    c2 = c * LOG2E
    hi = c2.astype(BF16).astype(F32)
    r1 = c2 - hi
    mid = r1.astype(BF16).astype(F32)
    lo = (r1 - mid).astype(BF16).astype(F32)
    pad = jnp.zeros((LANE - 3 * N_HEADS, S), F32)
    cb_ref[...] = jnp.concatenate([hi, mid, lo, pad], axis=0).T.astype(BF16)


def _forget_cumsum(f16, b_col, B, S):
    return pl.pallas_call(
        _forget_cumsum_body,
        grid=(B,),
        in_specs=[
            pl.BlockSpec((N_HEADS, S), lambda b: (0, b)),
            pl.BlockSpec((N_HEADS, 1), lambda b: (0, 0)),
        ],
        out_specs=pl.BlockSpec((S, LANE), lambda b: (b, 0)),
        out_shape=jax.ShapeDtypeStruct((B * S, LANE), BF16),
        compiler_params=_cp("parallel"),
        name="forget_cumsum",
    )(f16, b_col)


def _head_cols(g):
    return slice(g * HEAD_DIM, (g + 1) * HEAD_DIM)


def _fox_body(qt_ref, k_ref, vt_ref, cb_ref, o_ref, *, tq, heads):
    hg = pl.program_id(1)
    i = pl.program_id(2)
    sub = lax.broadcasted_iota(I32, (LANE, tq), 0)
    qs = []
    for g in range(heads):
        h = hg * heads + g
        pick = (sub == h) | (sub == N_HEADS + h) | (sub == 2 * N_HEADS + h)
        qs.append(jnp.concatenate([qt_ref[_head_cols(g), :], jnp.where(pick, -1.0, 0.0).astype(BF16)], axis=0))
    key = lax.broadcasted_iota(I32, (tq, tq), 0)
    qry = lax.broadcasted_iota(I32, (tq, tq), 1)

    def scores(kt):
        ks = pl.multiple_of(kt * tq, tq)
        cb = cb_ref[pl.ds(ks, tq), :]
        out = []
        for g in range(heads):
            k_aug = jnp.concatenate([k_ref[pl.ds(ks, tq), _head_cols(g)], cb], axis=1)
            out.append(jnp.dot(k_aug, qs[g], preferred_element_type=F32))
        return tuple(out)

    def finish(kt, sc, carry, masked):
        ks = pl.multiple_of(kt * tq, tq)
        probs = []
        for g in range(heads):
            m, l, _ = carry[g]
            s = sc[g]
            if masked:
                s = jnp.where(key <= qry, s, NEG_INF)
            m_new = jnp.maximum(m, jnp.max(s, axis=0, keepdims=True))
            alpha = jnp.exp2(m - m_new)
            p = jnp.exp2(s - m_new)
            l = alpha * l + jnp.sum(p, axis=0, keepdims=True)
            probs.append((m_new, l, alpha, p.astype(BF16)))
        out = []
        for g in range(heads):
            m_new, l, alpha, p = probs[g]
            vt = vt_ref[_head_cols(g), pl.ds(ks, tq)]
            acc = alpha * carry[g][2] + jnp.dot(vt, p, preferred_element_type=F32)
            out.append((m_new, l, acc))
        return tuple(out)

    def step(kt, state):
        sc, carry = state
        nxt = scores(kt + 1)
        return nxt, finish(kt, sc, carry, False)

    init = tuple((jnp.full((1, tq), NEG_INF, F32), jnp.zeros((1, tq), F32),
                  jnp.zeros((HEAD_DIM, tq), F32)) for _ in range(heads))
    sc, carry = lax.fori_loop(0, i, step, (scores(0), init))
    carry = finish(i, sc, carry, True)
    for g in range(heads):
        _, l, acc = carry[g]
        o_ref[:, _head_cols(g)] = (acc / l).T.astype(BF16)


ATT_TILE = 256
ATT_HEADS_PER_STEP = 4


def _fox_attention(keys, qvt, cb, B, S):
    T = B * S
    tq = _tile(S, ATT_TILE)
    nq = S // tq
    hp = ATT_HEADS_PER_STEP
    ng = N_HEADS // hp
    w = hp * HEAD_DIM
    return pl.pallas_call(
        functools.partial(_fox_body, tq=tq, heads=hp),
        grid=(B, ng, nq),
        in_specs=[
            pl.BlockSpec((w, tq), lambda b, h, i: (h, b * nq + i)),
            pl.BlockSpec((S, w), lambda b, h, i: (b, h)),
            pl.BlockSpec((w, S), lambda b, h, i: (2 * ng + h, b)),
            pl.BlockSpec((S, LANE), lambda b, h, i: (b, 0)),
        ],
        out_specs=pl.BlockSpec((tq, w), lambda b, h, i: (b * nq + i, h)),
        out_shape=jax.ShapeDtypeStruct((T, HEADS_W), BF16),
        compiler_params=_cp("parallel", "parallel", "arbitrary"),
        name="fox_attention",
    )(qvt, keys, qvt, cb)


F32_EXP2_UNDERFLOW = -150.0


def _sb_body(qt_ref, k_ref, vt_ref, ut_ref, o_ref, *, tq, heads):
    i = pl.program_id(2)
    qs = [qt_ref[_head_cols(g), :] for g in range(heads)]
    ut = ut_ref[...]
    key = lax.broadcasted_iota(I32, (tq, tq), 0)
    qry = lax.broadcasted_iota(I32, (tq, tq), 1)
    strict = key < qry

    def scores(kt):
        ks = pl.multiple_of(kt * tq, tq)
        return tuple(jnp.dot(k_ref[pl.ds(ks, tq), _head_cols(g)], qs[g], preferred_element_type=F32)
                     for g in range(heads))

    def finish(kt, sc, carry, masked):
        ks = pl.multiple_of(kt * tq, tq)
        mid = []
        for g in range(heads):
            z = sc[g]
            log_beta = jnp.minimum(z, 0.0) - jnp.log2(1.0 + jnp.exp2(-jnp.abs(z)))
            log_om = log_beta - z
            if masked:
                log_om = jnp.where(strict, log_om, 0.0)
            hi, lo = _split_bf16(log_om)
            e = jnp.dot(ut, hi, preferred_element_type=F32) + jnp.dot(ut, lo, preferred_element_type=F32)
            mid.append((log_beta, log_om, e))
        out = []
        for g in range(heads):
            r_sum, acc = carry[g]
            log_beta, log_om, e = mid[g]
            a = jnp.exp2(log_beta + e + r_sum)
            if masked:
                a = jnp.where(strict, a, 0.0)
            vt = vt_ref[_head_cols(g), pl.ds(ks, tq)]
            acc = acc + jnp.dot(vt, a.astype(BF16), preferred_element_type=F32)
            out.append((r_sum + jnp.sum(log_om, axis=0, keepdims=True), acc))
        return tuple(out)

    def live(carry):
        top = carry[0][0]
        for g in range(1, heads):
            top = jnp.maximum(top, carry[g][0])
        return (jnp.max(top) > F32_EXP2_UNDERFLOW).astype(I32)

    init = tuple((jnp.zeros((1, tq), F32), jnp.zeros((HEAD_DIM, tq), F32)) for _ in range(heads))
    carry = finish(i, scores(i), init, True)

    def step(state):
        n, _, sc, carry = state
        kt = i - 1 - n
        nxt = scores(jnp.maximum(kt - 1, 0))
        carry = finish(kt, sc, carry, False)
        return n + 1, live(carry), nxt, carry

    state = (jnp.int32(0), live(carry), scores(jnp.maximum(i - 1, 0)), carry)
    _, _, _, carry = lax.while_loop(lambda st: (st[0] < i) & (st[1] > 0), step, state)
    for g in range(heads):
        o_ref[:, _head_cols(g)] = carry[g][1].T.astype(BF16)


def _sb_attention(keys, qvt, B, S):
    T = B * S
    tq = _tile(S, ATT_TILE)
    nq = S // tq
    hp = ATT_HEADS_PER_STEP
    ng = N_HEADS // hp
    w = hp * HEAD_DIM
    r = lax.broadcasted_iota(I32, (tq, tq), 0)
    c = lax.broadcasted_iota(I32, (tq, tq), 1)
    ut = (c > r).astype(BF16)
    return pl.pallas_call(
        functools.partial(_sb_body, tq=tq, heads=hp),
        grid=(B, ng, nq),
        in_specs=[
            pl.BlockSpec((w, tq), lambda b, h, i: (ng + h, b * nq + i)),
            pl.BlockSpec((S, w), lambda b, h, i: (b, ng + h)),
            pl.BlockSpec((w, S), lambda b, h, i: (3 * ng + h, b)),
            pl.BlockSpec((tq, tq), lambda b, h, i: (0, 0)),
        ],
        out_specs=pl.BlockSpec((tq, w), lambda b, h, i: (b * nq + i, h)),
        out_shape=jax.ShapeDtypeStruct((T, HEADS_W), BF16),
        compiler_params=_cp("parallel", "parallel", "arbitrary"),
        name="sb_attention",
    )(qvt, keys, qvt, ut)


def _att_out_body(oa_ref, ob_ref, wa_ref, wb_ref, x_ref, o_ref):
    acc = jnp.dot(oa_ref[...], wa_ref[...], preferred_element_type=F32)
    acc += jnp.dot(ob_ref[...], wb_ref[...], preferred_element_type=F32)
    o_ref[...] = x_ref[...] + acc


def _att_out_proj(oa, ob, w_out, layer, x2):
    T, D = x2.shape
    tm = _tile(T, 1024)
    tn = _tile(D, 512)
    return pl.pallas_call(
        _att_out_body,
        grid=(T // tm, D // tn),
        in_specs=[
            pl.BlockSpec((tm, HEADS_W), lambda i, j: (i, 0)),
            pl.BlockSpec((tm, HEADS_W), lambda i, j: (i, 0)),
            pl.BlockSpec((None, HEADS_W, tn), lambda i, j: (layer, 0, j)),
            pl.BlockSpec((None, HEADS_W, tn), lambda i, j: (layer, 1, j)),
            pl.BlockSpec((tm, tn), lambda i, j: (i, j)),
        ],
        out_specs=pl.BlockSpec((tm, tn), lambda i, j: (i, j)),
        out_shape=jax.ShapeDtypeStruct((T, D), F32),
        compiler_params=_cp("parallel", "parallel"),
        name="att_out_proj",
    )(oa, ob, w_out, w_out, x2)


def _ffn_up_body(x_ref, g_ref, wg_ref, wu_ref, a_ref, h_scr):
    @pl.when(pl.program_id(1) == 0)
    def _():
        h_scr[...] = _rms(x_ref[...], g_ref[...]).astype(BF16)

    h = h_scr[...]
    g = jnp.dot(h, wg_ref[...], preferred_element_type=F32)
    u = jnp.dot(h, wu_ref[...], preferred_element_type=F32)
    a_ref[...] = (_silu(g) * u).astype(BF16)


def _ffn_up(x2, gnorm, wg, wu, layer):
    T, D = x2.shape
    F = wg.shape[2]
    tm = _tile(T, 1024)
    tn = _tile(F, 512)
    return pl.pallas_call(
        _ffn_up_body,
        grid=(T // tm, F // tn),
        in_specs=[
            pl.BlockSpec((tm, D), lambda i, j: (i, 0)),
            pl.BlockSpec((1, D), lambda i, j: (0, 0)),
            pl.BlockSpec((None, D, tn), lambda i, j: (layer, 0, j)),
            pl.BlockSpec((None, D, tn), lambda i, j: (layer, 0, j)),
        ],
        out_specs=pl.BlockSpec((tm, tn), lambda i, j: (i, j)),
        out_shape=jax.ShapeDtypeStruct((T, F), BF16),
        scratch_shapes=[pltpu.VMEM((tm, D), BF16)],
        compiler_params=_cp("parallel", "arbitrary"),
        name="ffn_up",
    )(x2, gnorm, wg, wu)


def _ffn_down_body(a_ref, w_ref, x_ref, o_ref):
    o_ref[...] = x_ref[...] + jnp.dot(a_ref[...], w_ref[...], preferred_element_type=F32)


def _ffn_down(a, wd, layer, x2):
    T, D = x2.shape
    F = a.shape[1]
    tm = _tile(T, 1024)
    tn = _tile(D, 512)
    return pl.pallas_call(
        _ffn_down_body,
        grid=(T // tm, D // tn),
        in_specs=[
            pl.BlockSpec((tm, F), lambda i, j: (i, 0)),
            pl.BlockSpec((None, F, tn), lambda i, j: (layer, 0, j)),
            pl.BlockSpec((tm, tn), lambda i, j: (i, j)),
        ],
        out_specs=pl.BlockSpec((tm, tn), lambda i, j: (i, j)),
        out_shape=jax.ShapeDtypeStruct((T, D), F32),
        compiler_params=_cp("parallel", "parallel"),
        name="ffn_down",
    )(a, wd, x2)


def _conv_in_body(x_ref, g_ref, wb_ref, wc_ref, wu_ref, gb_ref, gg_ref, h_scr):
    @pl.when(pl.program_id(1) == 0)
    def _():
        h_scr[...] = _rms(x_ref[...], g_ref[...]).astype(BF16)

    h = h_scr[...]
    gb_ref[...] = jnp.dot(h, wb_ref[...], preferred_element_type=F32).astype(BF16)
    c = jnp.dot(h, wc_ref[...], preferred_element_type=F32)
    u = jnp.dot(h, wu_ref[...], preferred_element_type=F32)
    gg_ref[...] = (c * u).astype(BF16)


def _conv_in(x2, gnorm, w_in, layer):
    T, D = x2.shape
    tm = _tile(T, 1024)
    tn = _tile(D, 512)
    nd = D // tn
    return pl.pallas_call(
        _conv_in_body,
        grid=(T // tm, nd),
        in_specs=[
            pl.BlockSpec((tm, D), lambda i, j: (i, 0)),
            pl.BlockSpec((1, D), lambda i, j: (0, 0)),
            pl.BlockSpec((None, D, tn), lambda i, j: (layer, 0, j)),
            pl.BlockSpec((None, D, tn), lambda i, j: (layer, 0, nd + j)),
            pl.BlockSpec((None, D, tn), lambda i, j: (layer, 0, 2 * nd + j)),
        ],
        out_specs=[
            pl.BlockSpec((tm, tn), lambda i, j: (i, j)),
            pl.BlockSpec((tm, tn), lambda i, j: (i, j)),
        ],
        out_shape=[jax.ShapeDtypeStruct((T, D), BF16), jax.ShapeDtypeStruct((T, D), BF16)],
        scratch_shapes=[pltpu.VMEM((tm, D), BF16)],
        compiler_params=_cp("parallel", "arbitrary"),
        name="conv_in",
    )(x2, gnorm, w_in, w_in, w_in)


def _conv_out_body(gb_ref, g_ref, gp_ref, cw_ref, w_ref, x_ref, o_ref, *, tiles_per_seq):
    i = pl.program_id(0)
    g = g_ref[...].astype(F32)
    tm = g.shape[0]
    keep = jnp.where(i % tiles_per_seq == 0, 0.0, 1.0)
    prev = gp_ref[...].astype(F32) * keep
    p1 = prev[BF16_SUBLANE_TILE - 1:BF16_SUBLANE_TILE, :]
    p2 = prev[BF16_SUBLANE_TILE - 2:BF16_SUBLANE_TILE - 1, :]
    row = lax.broadcasted_iota(I32, g.shape, 0)
    g1 = jnp.where(row == 0, p1, pltpu.roll(g, 1, axis=0))
    g2 = jnp.where(row == 0, p2, jnp.where(row == 1, p1, pltpu.roll(g, 2, axis=0)))
    cw = cw_ref[...]
    conv = g2 * cw[0:1, :] + g1 * cw[1:2, :] + g * cw[2:3, :]
    y = (gb_ref[...].astype(F32) * conv).astype(BF16)
    o_ref[...] = x_ref[...] + jnp.dot(y, w_ref[...], preferred_element_type=F32)


def _conv_out(gb, gg, conv_w, w_out, layer, x2, S):
    T, D = x2.shape
    tm = _tile(S, 256)
    pt = BF16_SUBLANE_TILE
    return pl.pallas_call(
        functools.partial(_conv_out_body, tiles_per_seq=S // tm),
        grid=(T // tm,),
        in_specs=[
            pl.BlockSpec((tm, D), lambda i: (i, 0)),
            pl.BlockSpec((tm, D), lambda i: (i, 0)),
            pl.BlockSpec((pt, D), lambda i: (jnp.maximum(i * (tm // pt) - 1, 0), 0)),
            pl.BlockSpec((CONV_WIDTH, D), lambda i: (0, 0)),
            pl.BlockSpec((None, D, D), lambda i: (layer, 0, 0)),
            pl.BlockSpec((tm, D), lambda i: (i, 0)),
        ],
        out_specs=pl.BlockSpec((tm, D), lambda i: (i, 0)),
        out_shape=jax.ShapeDtypeStruct((T, D), F32),
        compiler_params=_cp("parallel"),
        name="conv_out",
    )(gb, gg, gg, conv_w, w_out, x2)


def _router_body(x_ref, g_ref, wrh_ref, wrl_ref, tri_ref, hp_ref, mi_ref, mf_ref, cnt_ref, carry):
    @pl.when(pl.program_id(0) == 0)
    def _():
        carry[...] = jnp.zeros_like(carry)

    h = _rms(x_ref[...], g_ref[...])
    hb, hl = _split_bf16(h)
    bits = pltpu.bitcast(hb.astype(F32), U32)
    half = bits.shape[1] // 2
    hp_ref[...] = (bits[:, :half] >> 16) | bits[:, half:]

    logits = _skinny_nt(wrh_ref, wrl_ref, hb, hl)[:N_EXPERTS]
    eidx = lax.broadcasted_iota(I32, logits.shape, 0).astype(F32)
    ne = float(N_EXPERTS)
    m1 = jnp.max(logits, axis=0, keepdims=True)
    i1 = jnp.min(jnp.where(logits == m1, eidx, ne), axis=0, keepdims=True)
    rest = jnp.where(eidx == i1, NEG_INF, logits)
    m2 = jnp.max(rest, axis=0, keepdims=True)
    i2 = jnp.min(jnp.where(rest == m2, eidx, ne), axis=0, keepdims=True)
    e21 = jnp.exp(m2 - m1)
    g1 = 1.0 / (1.0 + e21)
    g2 = e21 * g1

    sel = jnp.where((eidx == i1) | (eidx == i2), 1.0, 0.0)
    incl = jnp.dot(sel, tri_ref[...], preferred_element_type=F32)
    pos = carry[...] + incl - sel
    carry[...] = carry[...] + jnp.sum(sel, axis=1, keepdims=True)
    p1 = jnp.sum(jnp.where(eidx == i1, pos, 0.0), axis=0, keepdims=True)
    p2 = jnp.sum(jnp.where(eidx == i2, pos, 0.0), axis=0, keepdims=True)
    meta = jnp.where(eidx == 0, i1, jnp.where(eidx == 1, i2, jnp.where(eidx == 2, p1, jnp.where(eidx == 3, p2, 0.0))))
    mi_ref[...] = meta.astype(I32)
    mf_ref[...] = jnp.where(eidx == 0, g1, jnp.where(eidx == 1, g2, 0.0))
    cnt_ref[...] = jnp.broadcast_to(carry[...], cnt_ref.shape)


def _router(x2, gnorm, wr_hi, wr_lo):
    T, D = x2.shape
    tm = _tile(T, 512)
    r = lax.broadcasted_iota(I32, (tm, tm), 0)
    c = lax.broadcasted_iota(I32, (tm, tm), 1)
    tri = (r <= c).astype(F32)
    return pl.pallas_call(
        _router_body,
        grid=(T // tm,),
        in_specs=[
            pl.BlockSpec((tm, D), lambda i: (i, 0)),
            pl.BlockSpec((1, D), lambda i: (0, 0)),
            pl.BlockSpec((BF16_SUBLANE_TILE, D), lambda i: (0, 0)),
            pl.BlockSpec((BF16_SUBLANE_TILE, D), lambda i: (0, 0)),
            pl.BlockSpec((tm, tm), lambda i: (0, 0)),
        ],
        out_specs=[
            pl.BlockSpec((tm, D // 2), lambda i: (i, 0)),
            pl.BlockSpec((N_EXPERTS, tm), lambda i: (0, i)),
            pl.BlockSpec((N_EXPERTS, tm), lambda i: (0, i)),
            pl.BlockSpec((N_EXPERTS, LANE), lambda i: (0, 0)),
        ],
        out_shape=[
            jax.ShapeDtypeStruct((T, D // 2), U32),
            jax.ShapeDtypeStruct((N_EXPERTS, T), I32),
            jax.ShapeDtypeStruct((N_EXPERTS, T), F32),
            jax.ShapeDtypeStruct((N_EXPERTS, LANE), F32),
        ],
        scratch_shapes=[pltpu.VMEM((N_EXPERTS, 1), F32)],
        compiler_params=_cp("arbitrary"),
        name="moe_router",
    )(x2, gnorm, wr_hi, wr_lo, tri)


ROW_DMA_UNROLL = 8


def _dispatch_body(d1_ref, d2_ref, hp_ref, zero_ref, hs_ref, stage, sems):
    del zero_ref
    i = pl.program_id(0)
    n = hp_ref.shape[0]
    slot = i % 2
    stage[slot] = hp_ref[...]

    def issue(blk, carry):
        for u in range(ROW_DMA_UNROLL):
            r = blk * ROW_DMA_UNROLL + u
            src = stage.at[slot, pl.ds(r, 1)]
            pltpu.make_async_copy(src, hs_ref.at[pl.ds(d1_ref[r], 1)], sems.at[slot]).start()
            pltpu.make_async_copy(src, hs_ref.at[pl.ds(d2_ref[r], 1)], sems.at[slot]).start()
        return carry

    def drain(which):
        def body(blk, carry):
            row = pltpu.make_async_copy(stage.at[which, pl.ds(0, 1)], hs_ref.at[pl.ds(0, 1)], sems.at[which])
            for _ in range(2 * ROW_DMA_UNROLL):
                row.wait()
            return carry
        lax.fori_loop(0, n // ROW_DMA_UNROLL, body, 0)

    lax.fori_loop(0, n // ROW_DMA_UNROLL, issue, 0)

    @pl.when(i > 0)
    def _():
        drain(1 - slot)

    @pl.when(i == pl.num_programs(0) - 1)
    def _():
        drain(slot)


def _dispatch(hp, d1, d2, P):
    T, Dh = hp.shape
    tm = _tile(T, 256)
    zeros = jnp.zeros((P, Dh), U32)
    return pl.pallas_call(
        _dispatch_body,
        grid=(T // tm,),
        in_specs=[
            pl.BlockSpec((tm,), lambda i: (i,), memory_space=pltpu.SMEM),
            pl.BlockSpec((tm,), lambda i: (i,), memory_space=pltpu.SMEM),
            pl.BlockSpec((tm, Dh), lambda i: (i, 0)),
            pl.BlockSpec(memory_space=pl.ANY),
        ],
        out_specs=pl.BlockSpec(memory_space=pl.ANY),
        out_shape=jax.ShapeDtypeStruct((P, Dh), U32),
        scratch_shapes=[pltpu.VMEM((2, tm, Dh), U32), pltpu.SemaphoreType.DMA((2,))],
        input_output_aliases={3: 0},
        compiler_params=_cp("arbitrary"),
        name="moe_dispatch",
    )(d1, d2, hp, zeros)


def _unpack_rows(words):
    lo = pltpu.bitcast(words << 16, F32)
    hi = pltpu.bitcast(words & jnp.uint32(0xFFFF0000), F32)
    return jnp.concatenate([lo, hi], axis=1).astype(BF16)


def _expert_up_body(te_ref, hs_ref, wg_ref, wu_ref, a_ref, *, n_tiles):
    @pl.when(pl.program_id(1) < te_ref[n_tiles])
    def _():
        h = _unpack_rows(hs_ref[...])
        g = jnp.dot(h, wg_ref[...], preferred_element_type=F32)
        u = jnp.dot(h, wu_ref[...], preferred_element_type=F32)
        a_ref[...] = (_silu(g) * u).astype(BF16)

    @pl.when(pl.program_id(1) >= te_ref[n_tiles])
    def _():
        a_ref[...] = jnp.zeros_like(a_ref)


def _expert_up(tile_table, hs, wg, wu, layer, tme):
    P, Dh = hs.shape
    _, _, D, F = wg.shape
    tn = F // 2 if (F // 2) % LANE == 0 else F
    n_tiles = P // tme
    live = lambda i, te: jnp.minimum(i, te[n_tiles] - 1)
    return pl.pallas_call(
        functools.partial(_expert_up_body, n_tiles=n_tiles),
        grid_spec=pltpu.PrefetchScalarGridSpec(
            num_scalar_prefetch=1,
            grid=(F // tn, n_tiles),
            in_specs=[
                pl.BlockSpec((tme, Dh), lambda j, i, te: (live(i, te), 0)),
                pl.BlockSpec((None, None, D, tn), lambda j, i, te: (layer, te[i], 0, j)),
                pl.BlockSpec((None, None, D, tn), lambda j, i, te: (layer, te[i], 0, j)),
            ],
            out_specs=pl.BlockSpec((tme, tn), lambda j, i, te: (i, j)),
        ),
        out_shape=jax.ShapeDtypeStruct((P, F), BF16),
        compiler_params=_cp("arbitrary", "arbitrary"),
        name="moe_expert_up",
    )(tile_table, hs, wg, wu)


def _expert_down_body(te_ref, a_ref, w_ref, y_ref, *, n_tiles):
    @pl.when(pl.program_id(0) < te_ref[n_tiles])
    def _():
        y_ref[...] = jnp.dot(a_ref[...], w_ref[...], preferred_element_type=F32)

    @pl.when(pl.program_id(0) >= te_ref[n_tiles])
    def _():
        y_ref[...] = jnp.zeros_like(y_ref)


def _expert_down(tile_table, act, wd, layer, tme):
    P, F = act.shape
    D = wd.shape[3]
    n_tiles = P // tme
    return pl.pallas_call(
        functools.partial(_expert_down_body, n_tiles=n_tiles),
        grid_spec=pltpu.PrefetchScalarGridSpec(
            num_scalar_prefetch=1,
            grid=(n_tiles,),
            in_specs=[
                pl.BlockSpec((tme, F), lambda i, te: (jnp.minimum(i, te[n_tiles] - 1), 0)),
                pl.BlockSpec((None, None, F, D), lambda i, te: (layer, te[i], 0, 0)),
            ],
            out_specs=pl.BlockSpec((tme, D), lambda i, te: (i, 0)),
        ),
        out_shape=jax.ShapeDtypeStruct((P, D), F32),
        compiler_params=_cp("arbitrary"),
        name="moe_expert_down",
    )(tile_table, act, wd)


def _combine_body(d1_ref, d2_ref, d1n_ref, d2n_ref, mf_ref, x_ref, fn_ref, y_ref, o_ref, ya, yb, sems,
                  *, final_norm):
    i = pl.program_id(0)
    n = x_ref.shape[0]
    slot = i % 2

    def fetch(ia_ref, ib_ref, to):
        def issue(blk, carry):
            for u in range(ROW_DMA_UNROLL):
                r = blk * ROW_DMA_UNROLL + u
                pltpu.make_async_copy(y_ref.at[pl.ds(ia_ref[r], 1)], ya.at[to, pl.ds(r, 1)], sems.at[to]).start()
                pltpu.make_async_copy(y_ref.at[pl.ds(ib_ref[r], 1)], yb.at[to, pl.ds(r, 1)], sems.at[to]).start()
            return carry
        lax.fori_loop(0, n // ROW_DMA_UNROLL, issue, 0)

    @pl.when(i == 0)
    def _():
        fetch(d1_ref, d2_ref, 0)

    @pl.when(i < pl.num_programs(0) - 1)
    def _():
        fetch(d1n_ref, d2n_ref, 1 - slot)

    def drain(blk, carry):
        row = pltpu.make_async_copy(y_ref.at[pl.ds(0, 1)], ya.at[slot, pl.ds(0, 1)], sems.at[slot])
        for _ in range(2 * ROW_DMA_UNROLL):
            row.wait()
        return carry

    lax.fori_loop(0, n // ROW_DMA_UNROLL, drain, 0)

    gates = mf_ref[...]
    eye = lax.broadcasted_iota(I32, (n, n), 0) == lax.broadcasted_iota(I32, (n, n), 1)
    g1 = jnp.sum(jnp.where(eye, gates[0:1, :], 0.0), axis=1, keepdims=True)
    g2 = jnp.sum(jnp.where(eye, gates[1:2, :], 0.0), axis=1, keepdims=True)
    out = x_ref[...] + (g1 * ya[slot] + g2 * yb[slot])
    if final_norm:
        out = _rms(out, fn_ref[...])
    o_ref[...] = out


def _combine(d1, d2, mf, x2, fnorm, y, final_norm):
    T, D = x2.shape
    tm = _tile(T, 256)
    last = T // tm - 1
    cur = lambda i: (i,)
    nxt = lambda i: (jnp.minimum(i + 1, last),)
    return pl.pallas_call(
        functools.partial(_combine_body, final_norm=final_norm),
        grid=(T // tm,),
        in_specs=[
            pl.BlockSpec((tm,), cur, memory_space=pltpu.SMEM),
            pl.BlockSpec((tm,), cur, memory_space=pltpu.SMEM),
            pl.BlockSpec((tm,), nxt, memory_space=pltpu.SMEM),
            pl.BlockSpec((tm,), nxt, memory_space=pltpu.SMEM),
            pl.BlockSpec((N_EXPERTS, tm), lambda i: (0, i)),
            pl.BlockSpec((tm, D), lambda i: (i, 0)),
            pl.BlockSpec((1, D), lambda i: (0, 0)),
            pl.BlockSpec(memory_space=pl.ANY),
        ],
        out_specs=pl.BlockSpec((tm, D), lambda i: (i, 0)),
        out_shape=jax.ShapeDtypeStruct((T, D), F32),
        scratch_shapes=[pltpu.VMEM((2, tm, D), F32), pltpu.VMEM((2, tm, D), F32),
                        pltpu.SemaphoreType.DMA((2,))],
        compiler_params=_cp("arbitrary"),
        name="moe_combine",
    )(d1, d2, d1, d2, mf, x2, fnorm, y)


def _pad_rows16(w_t):
    pad = jnp.zeros((BF16_SUBLANE_TILE - w_t.shape[0], w_t.shape[1]), F32)
    w = jnp.concatenate([w_t, pad], axis=0)
    hi = w.astype(BF16)
    lo = (w - hi.astype(F32)).astype(BF16)
    return hi, lo


def _attention_layer(x2, B, S, gnorm, w_in_all, layer, b_forget, w_out):
    sb0 = 3 * HEADS_W + N_HEADS
    col = lambda part: w_in_all[layer, :, part * HEADS_W:(part + 1) * HEADS_W]
    sbc = lambda part: w_in_all[layer, :, sb0 + part * HEADS_W:sb0 + (part + 1) * HEADS_W]
    w_k = jnp.concatenate([col(1), sbc(1)], axis=1).astype(BF16)
    w_qvt = jnp.concatenate([col(0), sbc(0), col(2), sbc(2)], axis=1).T.astype(BF16)
    wf_hi, wf_lo = _pad_rows16(w_in_all[layer, :, 3 * HEADS_W:sb0].T)
    keys, qvt, f16 = _att_in_proj(x2, gnorm, w_k, w_qvt, wf_hi, wf_lo)
    cb = _forget_cumsum(f16, b_forget.reshape(N_HEADS, 1), B, S)
    oa = _fox_attention(keys, qvt, cb, B, S)
    ob = _sb_attention(keys, qvt, B, S)
    return _att_out_proj(oa, ob, w_out, layer, x2)


def _dense_ffn_layer(x2, gnorm, wg, wu, wd, layer):
    a = _ffn_up(x2, gnorm, wg, wu, layer)
    return _ffn_down(a, wd, layer, x2)


def _conv_layer(x2, S, gnorm, w_in, conv_w, w_out, layer):
    gb, gg = _conv_in(x2, gnorm, w_in, layer)
    return _conv_out(gb, gg, conv_w, w_out, layer, x2, S)


def _moe_layer(x2, gnorm, w_router, wg, wu, wd, layer, fnorm, final_norm):
    T, D = x2.shape
    tme = 512 if T >= 4096 else 128
    n_tiles = (2 * T) // tme + N_EXPERTS
    P = n_tiles * tme

    wr_hi, wr_lo = _pad_rows16(w_router.T)
    hp, mi, mf, cnt = _router(x2, gnorm, wr_hi, wr_lo)

    counts = cnt[:, 0].astype(I32)
    padded = ((counts + tme - 1) // tme) * tme
    ends = jnp.cumsum(padded)
    offs = ends - padded
    d1 = jnp.take(offs, mi[0]) + mi[2]
    d2 = jnp.take(offs, mi[1]) + mi[3]
    tile_start = jnp.arange(n_tiles, dtype=I32) * tme
    tile_expert = jnp.sum((tile_start[:, None] >= ends[None, :]).astype(I32), axis=1)
    tile_expert = jnp.minimum(tile_expert, N_EXPERTS - 1)
    tile_table = jnp.concatenate([tile_expert, (ends[-1:] // tme).astype(I32)])

    hs = _dispatch(hp, d1, d2, P)
    act = _expert_up(tile_table, hs, wg, wu, layer, tme)
    y = _expert_down(tile_table, act, wd, layer, tme)
    return _combine(d1, d2, mf, x2, fnorm, y, final_norm)


def kernel(x, mix_norm, ffn_norm, final_norm, w_in_att, b_forget, w_out_att, w_in_conv, conv_w,
           w_out_conv, w_gate_dense, w_up_dense, w_down_dense, w_router, w_gate_moe, w_up_moe,
           w_down_moe):
    B, S, D = x.shape
    depth = mix_norm.shape[0]
    assert depth % 2 == 0, "the final rmsnorm is fused into the last (routed) layer"
    x2 = x.reshape(B * S, D)
    fnorm = final_norm.reshape(1, D)
    w_out_att, w_in_conv, w_out_conv, w_gate_dense, w_up_dense, w_down_dense, w_gate_moe, w_up_moe, w_down_moe = (
        w.astype(BF16) for w in (w_out_att, w_in_conv, w_out_conv, w_gate_dense, w_up_dense, w_down_dense,
                                 w_gate_moe, w_up_moe, w_down_moe))
    for i in range(depth):
        j = i // 2
        mg = mix_norm[i].reshape(1, D)
        fg = ffn_norm[i].reshape(1, D)
        if i % 2 == 0:
            x2 = _attention_layer(x2, B, S, mg, w_in_att, j, b_forget[j], w_out_att)
            x2 = _dense_ffn_layer(x2, fg, w_gate_dense, w_up_dense, w_down_dense, j)
        else:
            x2 = _conv_layer(x2, S, mg, w_in_conv, conv_w[j], w_out_conv, j)
            x2 = _moe_layer(x2, fg, w_router[j], w_gate_moe, w_up_moe, w_down_moe, j,
                            fnorm, final_norm=(i == depth - 1))
    return x2.reshape(B, S, D)
```

```python
import functools
import math

import jax
import jax.numpy as jnp
from jax import lax
from jax.experimental import pallas as pl
from jax.experimental.pallas import tpu as pltpu

F32 = jnp.float32
BF16 = jnp.bfloat16
I32 = jnp.int32
U32 = jnp.uint32

HEAD_DIM = 128
N_HEADS = 8
HEADS_W = N_HEADS * HEAD_DIM
N_EXPERTS = 8
RMS_EPS = 1e-6
CONV_WIDTH = 3
LOG2E = math.log2(math.e)
QK_SCALE_LOG2 = LOG2E / math.sqrt(HEAD_DIM)

V7X_VMEM_LIMIT_BYTES = 56 * 1024 * 1024
LANE = 128
BF16_SUBLANE_TILE = 16
NEG_INF = float("-inf")

_NT = (((1,), (1,)), ((), ()))


def _cp(*sem):
    return pltpu.CompilerParams(dimension_semantics=sem, vmem_limit_bytes=V7X_VMEM_LIMIT_BYTES)


def _tile(n, pref, unit=LANE):
    if n <= pref:
        return n
    t = (pref // unit) * unit
    while t > unit and n % t:
        t -= unit
    assert n % t == 0, (n, pref)
    return t


def _rms(x, g):
    ms = jnp.mean(x * x, axis=-1, keepdims=True)
    return x * lax.rsqrt(ms + RMS_EPS) * g


def _split_bf16(v):
    hi = v.astype(BF16)
    lo = (v - hi.astype(F32)).astype(BF16)
    return hi, lo


def _log_sigmoid_pair(z):
    sp = jnp.log1p(jnp.exp(-jnp.abs(z)))
    return jnp.minimum(z, 0.0) - sp, -jnp.maximum(z, 0.0) - sp


def _silu(g):
    return g / (1.0 + jnp.exp(-g))


def _skinny_nt(wh_ref, wl_ref, hb, hl):
    wh = wh_ref[...]
    out = lax.dot_general(wh, hb, _NT, preferred_element_type=F32)
    out += lax.dot_general(wh, hl, _NT, preferred_element_type=F32)
    out += lax.dot_general(wl_ref[...], hb, _NT, preferred_element_type=F32)
    return out


def _att_in_body(x_ref, g_ref, w_ref, wv_ref, wfh_ref, wfl_ref, o_ref, vt_ref, f_ref, h_scr, *, q_blocks, n_main):
    j = pl.program_id(1)

    @pl.when(j == 0)
    def _():
        h = _rms(x_ref[...], g_ref[...])
        hb, hl = _split_bf16(h)
        h_scr[...] = hb
        f_ref[...] = _skinny_nt(wfh_ref, wfl_ref, hb, hl)

    @pl.when(j < n_main)
    def _():
        o_ref[...] = jnp.dot(h_scr[...], w_ref[...], preferred_element_type=F32).astype(BF16)

    @pl.when(j >= n_main)
    def _():
        acc = lax.dot_general(wv_ref[...], h_scr[...], _NT, preferred_element_type=F32)
        is_q = (j - n_main) < 2 * q_blocks
        vt_ref[...] = (acc * jnp.where(is_q, QK_SCALE_LOG2, 1.0)).astype(BF16)


def _att_in_proj(x2, gnorm, w_k, w_qvt, wf_hi, wf_lo):
    T, D = x2.shape
    N = w_k.shape[1]
    NV = w_qvt.shape[0]
    tm = _tile(T, 1024)
    tn = _tile(HEADS_W, 1024)
    n_main = N // tn
    return pl.pallas_call(
        functools.partial(_att_in_body, q_blocks=HEADS_W // tn, n_main=n_main),
        grid=(T // tm, n_main + NV // tn),
        in_specs=[
            pl.BlockSpec((tm, D), lambda i, j: (i, 0)),
            pl.BlockSpec((1, D), lambda i, j: (0, 0)),
            pl.BlockSpec((D, tn), lambda i, j: (0, jnp.minimum(j, n_main - 1))),
            pl.BlockSpec((tn, D), lambda i, j: (jnp.maximum(j - n_main, 0), 0)),
            pl.BlockSpec((BF16_SUBLANE_TILE, D), lambda i, j: (0, 0)),
            pl.BlockSpec((BF16_SUBLANE_TILE, D), lambda i, j: (0, 0)),
        ],
        out_specs=[
            pl.BlockSpec((tm, tn), lambda i, j: (i, jnp.minimum(j, n_main - 1))),
            pl.BlockSpec((tn, tm), lambda i, j: (jnp.maximum(j - n_main, 0), i)),
            pl.BlockSpec((BF16_SUBLANE_TILE, tm), lambda i, j: (0, i)),
        ],
        out_shape=[
            jax.ShapeDtypeStruct((T, N), BF16),
            jax.ShapeDtypeStruct((NV, T), BF16),
            jax.ShapeDtypeStruct((BF16_SUBLANE_TILE, T), F32),
        ],
        scratch_shapes=[pltpu.VMEM((tm, D), BF16)],
        compiler_params=_cp("parallel", "arbitrary"),
        name="att_in_proj",
    )(x2, gnorm, w_k, w_qvt, wf_hi, wf_lo)


def _forget_cumsum_body(f_ref, b_ref, cb_ref):
    z = f_ref[...] + b_ref[...]
    lf, _ = _log_sigmoid_pair(z)
    S = lf.shape[1]
    lane = lax.broadcasted_iota(I32, lf.shape, 1)
    c = lf
    sh = 1
    while sh < S:
        c = c + jnp.where(lane >= sh, pltpu.roll(c, sh, axis=1), 0.0)
        sh *= 2
    c2 = c * LOG2E
    hi = c2.astype(BF16).astype(F32)
    r1 = c2 - hi
    mid = r1.astype(BF16).astype(F32)
    lo = (r1 - mid).astype(BF16).astype(F32)
    pad = jnp.zeros((LANE - 3 * N_HEADS, S), F32)
    cb_ref[...] = jnp.concatenate([hi, mid, lo, pad], axis=0).T.astype(BF16)


def _forget_cumsum(f16, b_col, B, S):
    return pl.pallas_call(
        _forget_cumsum_body,
        grid=(B,),
        in_specs=[
            pl.BlockSpec((N_HEADS, S), lambda b: (0, b)),
            pl.BlockSpec((N_HEADS, 1), lambda b: (0, 0)),
        ],
        out_specs=pl.BlockSpec((S, LANE), lambda b: (b, 0)),
        out_shape=jax.ShapeDtypeStruct((B * S, LANE), BF16),
        compiler_params=_cp("parallel"),
        name="forget_cumsum",
    )(f16, b_col)


def _head_cols(g):
    return slice(g * HEAD_DIM, (g + 1) * HEAD_DIM)


def _fox_body(qt_ref, k_ref, vt_ref, cb_ref, o_ref, sc_a, sc_b, m_scr, l_scr, acc_scr, *, tq, heads):
    hg = pl.program_id(1)
    i = pl.program_id(2)
    sub = lax.broadcasted_iota(I32, (LANE, tq), 0)
    qs = []
    for g in range(heads):
        h = hg * heads + g
        pick = (sub == h) | (sub == N_HEADS + h) | (sub == 2 * N_HEADS + h)
        qs.append(jnp.concatenate([qt_ref[_head_cols(g), :], jnp.where(pick, -1.0, 0.0).astype(BF16)], axis=0))
    key = lax.broadcasted_iota(I32, (tq, tq), 0)
    qry = lax.broadcasted_iota(I32, (tq, tq), 1)

    def scores(kt, sc_scr):
        ks = pl.multiple_of(kt * tq, tq)
        cb = cb_ref[pl.ds(ks, tq), :]
        for g in range(heads):
            k_aug = jnp.concatenate([k_ref[pl.ds(ks, tq), _head_cols(g)], cb], axis=1)
            sc_scr[g] = jnp.dot(k_aug, qs[g], preferred_element_type=F32)

    def finish(kt, sc_scr, masked):
        ks = pl.multiple_of(kt * tq, tq)
        probs = []
        for g in range(heads):
            m = m_scr[g]
            s = sc_scr[g]
            if masked:
                s = jnp.where(key <= qry, s, NEG_INF)
            m_new = jnp.maximum(m, jnp.max(s, axis=0, keepdims=True))
            alpha = jnp.exp2(m - m_new)
            p = jnp.exp2(s - m_new)
            m_scr[g] = m_new
            l_scr[g] = alpha * l_scr[g] + jnp.sum(p, axis=0, keepdims=True)
            probs.append((alpha, p.astype(BF16)))
        for g in range(heads):
            alpha, p = probs[g]
            vt = vt_ref[_head_cols(g), pl.ds(ks, tq)]
            acc_scr[g] = alpha * acc_scr[g] + jnp.dot(vt, p, preferred_element_type=F32)

    def pair(k2, carry):
        scores(2 * k2 + 1, sc_b)
        finish(2 * k2, sc_a, False)
        scores(2 * k2 + 2, sc_a)
        finish(2 * k2 + 1, sc_b, False)
        return carry

    m_scr[...] = jnp.full(m_scr.shape, NEG_INF, F32)
    l_scr[...] = jnp.zeros(l_scr.shape, F32)
    acc_scr[...] = jnp.zeros(acc_scr.shape, F32)
    scores(0, sc_a)
    lax.fori_loop(0, i // 2, pair, 0)

    @pl.when(i % 2 == 0)
    def _():
        finish(i, sc_a, True)

    @pl.when(i % 2 == 1)
    def _():
        scores(i, sc_b)
        finish(i - 1, sc_a, False)
        finish(i, sc_b, True)

    for g in range(heads):
        o_ref[:, _head_cols(g)] = (acc_scr[g] / l_scr[g]).T.astype(BF16)


ATT_TILE = 256
ATT_HEADS_PER_STEP = 4


def _fox_attention(keys, qvt, cb, B, S):
    T = B * S
    tq = _tile(S, ATT_TILE)
    nq = S // tq
    hp = ATT_HEADS_PER_STEP
    ng = N_HEADS // hp
    w = hp * HEAD_DIM
    return pl.pallas_call(
        functools.partial(_fox_body, tq=tq, heads=hp),
        grid=(B, ng, nq),
        in_specs=[
            pl.BlockSpec((w, tq), lambda b, h, i: (h, b * nq + i)),
            pl.BlockSpec((S, w), lambda b, h, i: (b, h)),
            pl.BlockSpec((w, S), lambda b, h, i: (2 * ng + h, b)),
            pl.BlockSpec((S, LANE), lambda b, h, i: (b, 0)),
        ],
        out_specs=pl.BlockSpec((tq, w), lambda b, h, i: (b * nq + i, h)),
        out_shape=jax.ShapeDtypeStruct((T, HEADS_W), BF16),
        scratch_shapes=[pltpu.VMEM((hp, tq, tq), F32), pltpu.VMEM((hp, tq, tq), F32),
                        pltpu.VMEM((hp, 1, tq), F32), pltpu.VMEM((hp, 1, tq), F32),
                        pltpu.VMEM((hp, HEAD_DIM, tq), F32)],
        compiler_params=_cp("parallel", "parallel", "arbitrary"),
        name="fox_attention",
    )(qvt, keys, qvt, cb)


F32_EXP2_UNDERFLOW = -150.0


def _sb_body(qt_ref, k_ref, vt_ref, ut_ref, o_ref, r_scr, acc_scr, *, tq, heads):
    i = pl.program_id(2)
    qs = [qt_ref[_head_cols(g), :] for g in range(heads)]
    ut = ut_ref[...]
    key = lax.broadcasted_iota(I32, (tq, tq), 0)
    qry = lax.broadcasted_iota(I32, (tq, tq), 1)
    strict = key < qry

    def tile(kt, masked):
        ks = pl.multiple_of(kt * tq, tq)
        zs = [jnp.dot(k_ref[pl.ds(ks, tq), _head_cols(g)], qs[g], preferred_element_type=F32)
              for g in range(heads)]
        mid = []
        for g in range(heads):
            z = zs[g]
            log_beta = jnp.minimum(z, 0.0) - jnp.log2(1.0 + jnp.exp2(-jnp.abs(z)))
            log_om = log_beta - z
            if masked:
                log_om = jnp.where(strict, log_om, 0.0)
            hi, lo = _split_bf16(log_om)
            e = jnp.dot(ut, hi, preferred_element_type=F32) + jnp.dot(ut, lo, preferred_element_type=F32)
            mid.append((log_beta, log_om, e))
        for g in range(heads):
            log_beta, log_om, e = mid[g]
            r_sum = r_scr[g]
            a = jnp.exp2(log_beta + e + r_sum)
            if masked:
                a = jnp.where(strict, a, 0.0)
            vt = vt_ref[_head_cols(g), pl.ds(ks, tq)]
            acc_scr[g] += jnp.dot(vt, a.astype(BF16), preferred_element_type=F32)
            r_scr[g] = r_sum + jnp.sum(log_om, axis=0, keepdims=True)

    def live():
        return (jnp.max(r_scr[...]) > F32_EXP2_UNDERFLOW).astype(I32)

    r_scr[...] = jnp.zeros(r_scr.shape, F32)
    acc_scr[...] = jnp.zeros(acc_scr.shape, F32)
    tile(i, True)

    def step(state):
        n, _ = state
        tile(i - 1 - n, False)
        return n + 1, live()

    lax.while_loop(lambda st: (st[0] < i) & (st[1] > 0), step, (jnp.int32(0), live()))
    for g in range(heads):
        o_ref[:, _head_cols(g)] = acc_scr[g].T.astype(BF16)


def _sb_attention(keys, qvt, B, S):
    T = B * S
    tq = _tile(S, ATT_TILE)
    nq = S // tq
    hp = ATT_HEADS_PER_STEP
    ng = N_HEADS // hp
    w = hp * HEAD_DIM
    r = lax.broadcasted_iota(I32, (tq, tq), 0)
    c = lax.broadcasted_iota(I32, (tq, tq), 1)
    ut = (c > r).astype(BF16)
    return pl.pallas_call(
        functools.partial(_sb_body, tq=tq, heads=hp),
        grid=(B, ng, nq),
        in_specs=[
            pl.BlockSpec((w, tq), lambda b, h, i: (ng + h, b * nq + i)),
            pl.BlockSpec((S, w), lambda b, h, i: (b, ng + h)),
            pl.BlockSpec((w, S), lambda b, h, i: (3 * ng + h, b)),
            pl.BlockSpec((tq, tq), lambda b, h, i: (0, 0)),
        ],
        out_specs=pl.BlockSpec((tq, w), lambda b, h, i: (b * nq + i, h)),
        out_shape=jax.ShapeDtypeStruct((T, HEADS_W), BF16),
        scratch_shapes=[pltpu.VMEM((hp, 1, tq), F32), pltpu.VMEM((hp, HEAD_DIM, tq), F32)],
        compiler_params=_cp("parallel", "parallel", "arbitrary"),
        name="sb_attention",
    )(qvt, keys, qvt, ut)


def _att_out_body(oa_ref, ob_ref, wa_ref, wb_ref, x_ref, o_ref):
    acc = jnp.dot(oa_ref[...], wa_ref[...], preferred_element_type=F32)
    acc += jnp.dot(ob_ref[...], wb_ref[...], preferred_element_type=F32)
    o_ref[...] = x_ref[...] + acc


def _att_out_proj(oa, ob, w_out, layer, x2):
    T, D = x2.shape
    tm = _tile(T, 1024)
    tn = _tile(D, 512)
    return pl.pallas_call(
        _att_out_body,
        grid=(T // tm, D // tn),
        in_specs=[
            pl.BlockSpec((tm, HEADS_W), lambda i, j: (i, 0)),
            pl.BlockSpec((tm, HEADS_W), lambda i, j: (i, 0)),
            pl.BlockSpec((None, HEADS_W, tn), lambda i, j: (layer, 0, j)),
            pl.BlockSpec((None, HEADS_W, tn), lambda i, j: (layer, 1, j)),
            pl.BlockSpec((tm, tn), lambda i, j: (i, j)),
        ],
        out_specs=pl.BlockSpec((tm, tn), lambda i, j: (i, j)),
        out_shape=jax.ShapeDtypeStruct((T, D), F32),
        compiler_params=_cp("parallel", "parallel"),
        name="att_out_proj",
    )(oa, ob, w_out, w_out, x2)


def _ffn_up_body(x_ref, g_ref, wg_ref, wu_ref, a_ref, h_scr):
    @pl.when(pl.program_id(1) == 0)
    def _():
        h_scr[...] = _rms(x_ref[...], g_ref[...]).astype(BF16)

    h = h_scr[...]
    g = jnp.dot(h, wg_ref[...], preferred_element_type=F32)
    u = jnp.dot(h, wu_ref[...], preferred_element_type=F32)
    a_ref[...] = (_silu(g) * u).astype(BF16)


def _ffn_up(x2, gnorm, wg, wu, layer):
    T, D = x2.shape
    F = wg.shape[2]
    tm = _tile(T, 1024)
    tn = _tile(F, 512)
    return pl.pallas_call(
        _ffn_up_body,
        grid=(T // tm, F // tn),
        in_specs=[
            pl.BlockSpec((tm, D), lambda i, j: (i, 0)),
            pl.BlockSpec((1, D), lambda i, j: (0, 0)),
            pl.BlockSpec((None, D, tn), lambda i, j: (layer, 0, j)),
            pl.BlockSpec((None, D, tn), lambda i, j: (layer, 0, j)),
        ],
        out_specs=pl.BlockSpec((tm, tn), lambda i, j: (i, j)),
        out_shape=jax.ShapeDtypeStruct((T, F), BF16),
        scratch_shapes=[pltpu.VMEM((tm, D), BF16)],
        compiler_params=_cp("parallel", "arbitrary"),
        name="ffn_up",
    )(x2, gnorm, wg, wu)


def _ffn_down_body(a_ref, w_ref, x_ref, o_ref):
    o_ref[...] = x_ref[...] + jnp.dot(a_ref[...], w_ref[...], preferred_element_type=F32)


def _ffn_down(a, wd, layer, x2):
    T, D = x2.shape
    F = a.shape[1]
    tm = _tile(T, 1024)
    tn = _tile(D, 512)
    return pl.pallas_call(
        _ffn_down_body,
        grid=(T // tm, D // tn),
        in_specs=[
            pl.BlockSpec((tm, F), lambda i, j: (i, 0)),
            pl.BlockSpec((None, F, tn), lambda i, j: (layer, 0, j)),
            pl.BlockSpec((tm, tn), lambda i, j: (i, j)),
        ],
        out_specs=pl.BlockSpec((tm, tn), lambda i, j: (i, j)),
        out_shape=jax.ShapeDtypeStruct((T, D), F32),
        compiler_params=_cp("parallel", "parallel"),
        name="ffn_down",
    )(a, wd, x2)


def _conv_in_body(x_ref, g_ref, wb_ref, wc_ref, wu_ref, gb_ref, gg_ref, h_scr):
    @pl.when(pl.program_id(1) == 0)
    def _():
        h_scr[...] = _rms(x_ref[...], g_ref[...]).astype(BF16)

    h = h_scr[...]
    gb_ref[...] = jnp.dot(h, wb_ref[...], preferred_element_type=F32).astype(BF16)
    c = jnp.dot(h, wc_ref[...], preferred_element_type=F32)
    u = jnp.dot(h, wu_ref[...], preferred_element_type=F32)
    gg_ref[...] = (c * u).astype(BF16)


def _conv_in(x2, gnorm, w_in, layer):
    T, D = x2.shape
    tm = _tile(T, 1024)
    tn = _tile(D, 512)
    nd = D // tn
    return pl.pallas_call(
        _conv_in_body,
        grid=(T // tm, nd),
        in_specs=[
            pl.BlockSpec((tm, D), lambda i, j: (i, 0)),
            pl.BlockSpec((1, D), lambda i, j: (0, 0)),
            pl.BlockSpec((None, D, tn), lambda i, j: (layer, 0, j)),
            pl.BlockSpec((None, D, tn), lambda i, j: (layer, 0, nd + j)),
            pl.BlockSpec((None, D, tn), lambda i, j: (layer, 0, 2 * nd + j)),
        ],
        out_specs=[
            pl.BlockSpec((tm, tn), lambda i, j: (i, j)),
            pl.BlockSpec((tm, tn), lambda i, j: (i, j)),
        ],
        out_shape=[jax.ShapeDtypeStruct((T, D), BF16), jax.ShapeDtypeStruct((T, D), BF16)],
        scratch_shapes=[pltpu.VMEM((tm, D), BF16)],
        compiler_params=_cp("parallel", "arbitrary"),
        name="conv_in",
    )(x2, gnorm, w_in, w_in, w_in)


def _conv_out_body(gb_ref, g_ref, gp_ref, cw_ref, w_ref, x_ref, o_ref, *, tiles_per_seq):
    i = pl.program_id(0)
    g = g_ref[...].astype(F32)
    tm = g.shape[0]
    keep = jnp.where(i % tiles_per_seq == 0, 0.0, 1.0)
    prev = gp_ref[...].astype(F32) * keep
    p1 = prev[BF16_SUBLANE_TILE - 1:BF16_SUBLANE_TILE, :]
    p2 = prev[BF16_SUBLANE_TILE - 2:BF16_SUBLANE_TILE - 1, :]
    row = lax.broadcasted_iota(I32, g.shape, 0)
    g1 = jnp.where(row == 0, p1, pltpu.roll(g, 1, axis=0))
    g2 = jnp.where(row == 0, p2, jnp.where(row == 1, p1, pltpu.roll(g, 2, axis=0)))
    cw = cw_ref[...]
    conv = g2 * cw[0:1, :] + g1 * cw[1:2, :] + g * cw[2:3, :]
    y = (gb_ref[...].astype(F32) * conv).astype(BF16)
    o_ref[...] = x_ref[...] + jnp.dot(y, w_ref[...], preferred_element_type=F32)


def _conv_out(gb, gg, conv_w, w_out, layer, x2, S):
    T, D = x2.shape
    tm = _tile(S, 256)
    pt = BF16_SUBLANE_TILE
    return pl.pallas_call(
        functools.partial(_conv_out_body, tiles_per_seq=S // tm),
        grid=(T // tm,),
        in_specs=[
            pl.BlockSpec((tm, D), lambda i: (i, 0)),
            pl.BlockSpec((tm, D), lambda i: (i, 0)),
            pl.BlockSpec((pt, D), lambda i: (jnp.maximum(i * (tm // pt) - 1, 0), 0)),
            pl.BlockSpec((CONV_WIDTH, D), lambda i: (0, 0)),
            pl.BlockSpec((None, D, D), lambda i: (layer, 0, 0)),
            pl.BlockSpec((tm, D), lambda i: (i, 0)),
        ],
        out_specs=pl.BlockSpec((tm, D), lambda i: (i, 0)),
        out_shape=jax.ShapeDtypeStruct((T, D), F32),
        compiler_params=_cp("parallel"),
        name="conv_out",
    )(gb, gg, gg, conv_w, w_out, x2)


def _router_body(x_ref, g_ref, wrh_ref, wrl_ref, tri_ref, hp_ref, mi_ref, mf_ref, cnt_ref, carry):
    @pl.when(pl.program_id(0) == 0)
    def _():
        carry[...] = jnp.zeros_like(carry)

    h = _rms(x_ref[...], g_ref[...])
    hb, hl = _split_bf16(h)
    bits = pltpu.bitcast(hb.astype(F32), U32)
    half = bits.shape[1] // 2
    hp_ref[...] = (bits[:, :half] >> 16) | bits[:, half:]

    logits = _skinny_nt(wrh_ref, wrl_ref, hb, hl)[:N_EXPERTS]
    eidx = lax.broadcasted_iota(I32, logits.shape, 0).astype(F32)
    ne = float(N_EXPERTS)
    m1 = jnp.max(logits, axis=0, keepdims=True)
    i1 = jnp.min(jnp.where(logits == m1, eidx, ne), axis=0, keepdims=True)
    rest = jnp.where(eidx == i1, NEG_INF, logits)
    m2 = jnp.max(rest, axis=0, keepdims=True)
    i2 = jnp.min(jnp.where(rest == m2, eidx, ne), axis=0, keepdims=True)
    e21 = jnp.exp(m2 - m1)
    g1 = 1.0 / (1.0 + e21)
    g2 = e21 * g1

    sel = jnp.where((eidx == i1) | (eidx == i2), 1.0, 0.0)
    incl = jnp.dot(sel, tri_ref[...], preferred_element_type=F32)
    pos = carry[...] + incl - sel
    carry[...] = carry[...] + jnp.sum(sel, axis=1, keepdims=True)
    p1 = jnp.sum(jnp.where(eidx == i1, pos, 0.0), axis=0, keepdims=True)
    p2 = jnp.sum(jnp.where(eidx == i2, pos, 0.0), axis=0, keepdims=True)
    meta = jnp.where(eidx == 0, i1, jnp.where(eidx == 1, i2, jnp.where(eidx == 2, p1, jnp.where(eidx == 3, p2, 0.0))))
    mi_ref[...] = meta.astype(I32)
    mf_ref[...] = jnp.where(eidx == 0, g1, jnp.where(eidx == 1, g2, 0.0))
    cnt_ref[...] = jnp.broadcast_to(carry[...], cnt_ref.shape)


def _router(x2, gnorm, wr_hi, wr_lo):
    T, D = x2.shape
    tm = _tile(T, 512)
    r = lax.broadcasted_iota(I32, (tm, tm), 0)
    c = lax.broadcasted_iota(I32, (tm, tm), 1)
    tri = (r <= c).astype(F32)
    return pl.pallas_call(
        _router_body,
        grid=(T // tm,),
        in_specs=[
            pl.BlockSpec((tm, D), lambda i: (i, 0)),
            pl.BlockSpec((1, D), lambda i: (0, 0)),
            pl.BlockSpec((BF16_SUBLANE_TILE, D), lambda i: (0, 0)),
            pl.BlockSpec((BF16_SUBLANE_TILE, D), lambda i: (0, 0)),
            pl.BlockSpec((tm, tm), lambda i: (0, 0)),
        ],
        out_specs=[
            pl.BlockSpec((tm, D // 2), lambda i: (i, 0)),
            pl.BlockSpec((N_EXPERTS, tm), lambda i: (0, i)),
            pl.BlockSpec((N_EXPERTS, tm), lambda i: (0, i)),
            pl.BlockSpec((N_EXPERTS, LANE), lambda i: (0, 0)),
        ],
        out_shape=[
            jax.ShapeDtypeStruct((T, D // 2), U32),
            jax.ShapeDtypeStruct((N_EXPERTS, T), I32),
            jax.ShapeDtypeStruct((N_EXPERTS, T), F32),
            jax.ShapeDtypeStruct((N_EXPERTS, LANE), F32),
        ],
        scratch_shapes=[pltpu.VMEM((N_EXPERTS, 1), F32)],
        compiler_params=_cp("arbitrary"),
        name="moe_router",
    )(x2, gnorm, wr_hi, wr_lo, tri)


ROW_DMA_UNROLL = 8


def _dispatch_body(d1_ref, d2_ref, hp_ref, zero_ref, hs_ref, stage, sems):
    del zero_ref
    i = pl.program_id(0)
    n = hp_ref.shape[0]
    slot = i % 2
    stage[slot] = hp_ref[...]

    def issue(blk, carry):
        for u in range(ROW_DMA_UNROLL):
            r = blk * ROW_DMA_UNROLL + u
            src = stage.at[slot, pl.ds(r, 1)]
            pltpu.make_async_copy(src, hs_ref.at[pl.ds(d1_ref[r], 1)], sems.at[slot]).start()
            pltpu.make_async_copy(src, hs_ref.at[pl.ds(d2_ref[r], 1)], sems.at[slot]).start()
        return carry

    def drain(which):
        def body(blk, carry):
            row = pltpu.make_async_copy(stage.at[which, pl.ds(0, 1)], hs_ref.at[pl.ds(0, 1)], sems.at[which])
            for _ in range(2 * ROW_DMA_UNROLL):
                row.wait()
            return carry
        lax.fori_loop(0, n // ROW_DMA_UNROLL, body, 0)

    lax.fori_loop(0, n // ROW_DMA_UNROLL, issue, 0)

    @pl.when(i > 0)
    def _():
        drain(1 - slot)

    @pl.when(i == pl.num_programs(0) - 1)
    def _():
        drain(slot)


def _dispatch(hp, d1, d2, P):
    T, Dh = hp.shape
    tm = _tile(T, 256)
    zeros = jnp.zeros((P, Dh), U32)
    return pl.pallas_call(
        _dispatch_body,
        grid=(T // tm,),
        in_specs=[
            pl.BlockSpec((tm,), lambda i: (i,), memory_space=pltpu.SMEM),
            pl.BlockSpec((tm,), lambda i: (i,), memory_space=pltpu.SMEM),
            pl.BlockSpec((tm, Dh), lambda i: (i, 0)),
            pl.BlockSpec(memory_space=pl.ANY),
        ],
        out_specs=pl.BlockSpec(memory_space=pl.ANY),
        out_shape=jax.ShapeDtypeStruct((P, Dh), U32),
        scratch_shapes=[pltpu.VMEM((2, tm, Dh), U32), pltpu.SemaphoreType.DMA((2,))],
        input_output_aliases={3: 0},
        compiler_params=_cp("arbitrary"),
        name="moe_dispatch",
    )(d1, d2, hp, zeros)


def _unpack_rows(words):
    lo = pltpu.bitcast(words << 16, F32)
    hi = pltpu.bitcast(words & jnp.uint32(0xFFFF0000), F32)
    return jnp.concatenate([lo, hi], axis=1).astype(BF16)


def _expert_up_body(te_ref, hs_ref, wg_ref, wu_ref, a_ref, *, n_tiles):
    @pl.when(pl.program_id(1) < te_ref[n_tiles])
    def _():
        h = _unpack_rows(hs_ref[...])
        g = jnp.dot(h, wg_ref[...], preferred_element_type=F32)
        u = jnp.dot(h, wu_ref[...], preferred_element_type=F32)
        a_ref[...] = (_silu(g) * u).astype(BF16)

    @pl.when(pl.program_id(1) >= te_ref[n_tiles])
    def _():
        a_ref[...] = jnp.zeros_like(a_ref)


def _expert_up(tile_table, hs, wg, wu, layer, tme):
    P, Dh = hs.shape
    _, _, D, F = wg.shape
    tn = F // 2 if (F // 2) % LANE == 0 else F
    n_tiles = P // tme
    live = lambda i, te: jnp.minimum(i, te[n_tiles] - 1)
    return pl.pallas_call(
        functools.partial(_expert_up_body, n_tiles=n_tiles),
        grid_spec=pltpu.PrefetchScalarGridSpec(
            num_scalar_prefetch=1,
            grid=(F // tn, n_tiles),
            in_specs=[
                pl.BlockSpec((tme, Dh), lambda j, i, te: (live(i, te), 0)),
                pl.BlockSpec((None, None, D, tn), lambda j, i, te: (layer, te[i], 0, j)),
                pl.BlockSpec((None, None, D, tn), lambda j, i, te: (layer, te[i], 0, j)),
            ],
            out_specs=pl.BlockSpec((tme, tn), lambda j, i, te: (i, j)),
        ),
        out_shape=jax.ShapeDtypeStruct((P, F), BF16),
        compiler_params=_cp("arbitrary", "arbitrary"),
        name="moe_expert_up",
    )(tile_table, hs, wg, wu)


def _expert_down_body(te_ref, a_ref, w_ref, y_ref, *, n_tiles):
    @pl.when(pl.program_id(0) < te_ref[n_tiles])
    def _():
        y_ref[...] = jnp.dot(a_ref[...], w_ref[...], preferred_element_type=F32)

    @pl.when(pl.program_id(0) >= te_ref[n_tiles])
    def _():
        y_ref[...] = jnp.zeros_like(y_ref)


def _expert_down(tile_table, act, wd, layer, tme):
    P, F = act.shape
    D = wd.shape[3]
    n_tiles = P // tme
    return pl.pallas_call(
        functools.partial(_expert_down_body, n_tiles=n_tiles),
        grid_spec=pltpu.PrefetchScalarGridSpec(
            num_scalar_prefetch=1,
            grid=(n_tiles,),
            in_specs=[
                pl.BlockSpec((tme, F), lambda i, te: (jnp.minimum(i, te[n_tiles] - 1), 0)),
                pl.BlockSpec((None, None, F, D), lambda i, te: (layer, te[i], 0, 0)),
            ],
            out_specs=pl.BlockSpec((tme, D), lambda i, te: (i, 0)),
        ),
        out_shape=jax.ShapeDtypeStruct((P, D), F32),
        compiler_params=_cp("arbitrary"),
        name="moe_expert_down",
    )(tile_table, act, wd)


def _combine_body(d1_ref, d2_ref, d1n_ref, d2n_ref, mf_ref, x_ref, fn_ref, y_ref, o_ref, ya, yb, sems,
                  *, final_norm):
    i = pl.program_id(0)
    n = x_ref.shape[0]
    slot = i % 2

    def fetch(ia_ref, ib_ref, to):
        def issue(blk, carry):
            for u in range(ROW_DMA_UNROLL):
                r = blk * ROW_DMA_UNROLL + u
                pltpu.make_async_copy(y_ref.at[pl.ds(ia_ref[r], 1)], ya.at[to, pl.ds(r, 1)], sems.at[to]).start()
                pltpu.make_async_copy(y_ref.at[pl.ds(ib_ref[r], 1)], yb.at[to, pl.ds(r, 1)], sems.at[to]).start()
            return carry
        lax.fori_loop(0, n // ROW_DMA_UNROLL, issue, 0)

    @pl.when(i == 0)
    def _():
        fetch(d1_ref, d2_ref, 0)

    @pl.when(i < pl.num_programs(0) - 1)
    def _():
        fetch(d1n_ref, d2n_ref, 1 - slot)

    def drain(blk, carry):
        row = pltpu.make_async_copy(y_ref.at[pl.ds(0, 1)], ya.at[slot, pl.ds(0, 1)], sems.at[slot])
        for _ in range(2 * ROW_DMA_UNROLL):
            row.wait()
        return carry

    lax.fori_loop(0, n // ROW_DMA_UNROLL, drain, 0)

    gates = mf_ref[...]
    eye = lax.broadcasted_iota(I32, (n, n), 0) == lax.broadcasted_iota(I32, (n, n), 1)
    g1 = jnp.sum(jnp.where(eye, gates[0:1, :], 0.0), axis=1, keepdims=True)
    g2 = jnp.sum(jnp.where(eye, gates[1:2, :], 0.0), axis=1, keepdims=True)
    out = x_ref[...] + (g1 * ya[slot] + g2 * yb[slot])
    if final_norm:
        out = _rms(out, fn_ref[...])
    o_ref[...] = out


def _combine(d1, d2, mf, x2, fnorm, y, final_norm):
    T, D = x2.shape
    tm = _tile(T, 256)
    last = T // tm - 1
    cur = lambda i: (i,)
    nxt = lambda i: (jnp.minimum(i + 1, last),)
    return pl.pallas_call(
        functools.partial(_combine_body, final_norm=final_norm),
        grid=(T // tm,),
        in_specs=[
            pl.BlockSpec((tm,), cur, memory_space=pltpu.SMEM),
            pl.BlockSpec((tm,), cur, memory_space=pltpu.SMEM),
            pl.BlockSpec((tm,), nxt, memory_space=pltpu.SMEM),
            pl.BlockSpec((tm,), nxt, memory_space=pltpu.SMEM),
            pl.BlockSpec((N_EXPERTS, tm), lambda i: (0, i)),
            pl.BlockSpec((tm, D), lambda i: (i, 0)),
            pl.BlockSpec((1, D), lambda i: (0, 0)),
            pl.BlockSpec(memory_space=pl.ANY),
        ],
        out_specs=pl.BlockSpec((tm, D), lambda i: (i, 0)),
        out_shape=jax.ShapeDtypeStruct((T, D), F32),
        scratch_shapes=[pltpu.VMEM((2, tm, D), F32), pltpu.VMEM((2, tm, D), F32),
                        pltpu.SemaphoreType.DMA((2,))],
        compiler_params=_cp("arbitrary"),
        name="moe_combine",
    )(d1, d2, d1, d2, mf, x2, fnorm, y)


def _pad_rows16(w_t):
    pad = jnp.zeros((BF16_SUBLANE_TILE - w_t.shape[0], w_t.shape[1]), F32)
    w = jnp.concatenate([w_t, pad], axis=0)
    hi = w.astype(BF16)
    lo = (w - hi.astype(F32)).astype(BF16)
    return hi, lo


def _attention_layer(x2, B, S, gnorm, w_in_all, layer, b_forget, w_out):
    sb0 = 3 * HEADS_W + N_HEADS
    col = lambda part: w_in_all[layer, :, part * HEADS_W:(part + 1) * HEADS_W]
    sbc = lambda part: w_in_all[layer, :, sb0 + part * HEADS_W:sb0 + (part + 1) * HEADS_W]
    w_k = jnp.concatenate([col(1), sbc(1)], axis=1).astype(BF16)
    w_qvt = jnp.concatenate([col(0), sbc(0), col(2), sbc(2)], axis=1).T.astype(BF16)
    wf_hi, wf_lo = _pad_rows16(w_in_all[layer, :, 3 * HEADS_W:sb0].T)
    keys, qvt, f16 = _att_in_proj(x2, gnorm, w_k, w_qvt, wf_hi, wf_lo)
    cb = _forget_cumsum(f16, b_forget.reshape(N_HEADS, 1), B, S)
    oa = _fox_attention(keys, qvt, cb, B, S)
    ob = _sb_attention(keys, qvt, B, S)
    return _att_out_proj(oa, ob, w_out, layer, x2)


def _dense_ffn_layer(x2, gnorm, wg, wu, wd, layer):
    a = _ffn_up(x2, gnorm, wg, wu, layer)
    return _ffn_down(a, wd, layer, x2)


def _conv_layer(x2, S, gnorm, w_in, conv_w, w_out, layer):
    gb, gg = _conv_in(x2, gnorm, w_in, layer)
    return _conv_out(gb, gg, conv_w, w_out, layer, x2, S)


def _moe_layer(x2, gnorm, w_router, wg, wu, wd, layer, fnorm, final_norm):
    T, D = x2.shape
    tme = 512 if T >= 4096 else 128
    n_tiles = (2 * T) // tme + N_EXPERTS
    P = n_tiles * tme

    wr_hi, wr_lo = _pad_rows16(w_router.T)
    hp, mi, mf, cnt = _router(x2, gnorm, wr_hi, wr_lo)

    counts = cnt[:, 0].astype(I32)
    padded = ((counts + tme - 1) // tme) * tme
    ends = jnp.cumsum(padded)
    offs = ends - padded
    d1 = jnp.take(offs, mi[0]) + mi[2]
    d2 = jnp.take(offs, mi[1]) + mi[3]
    tile_start = jnp.arange(n_tiles, dtype=I32) * tme
    tile_expert = jnp.sum((tile_start[:, None] >= ends[None, :]).astype(I32), axis=1)
    tile_expert = jnp.minimum(tile_expert, N_EXPERTS - 1)
    tile_table = jnp.concatenate([tile_expert, (ends[-1:] // tme).astype(I32)])

    hs = _dispatch(hp, d1, d2, P)
    act = _expert_up(tile_table, hs, wg, wu, layer, tme)
    y = _expert_down(tile_table, act, wd, layer, tme)
    return _combine(d1, d2, mf, x2, fnorm, y, final_norm)


def kernel(x, mix_norm, ffn_norm, final_norm, w_in_att, b_forget, w_out_att, w_in_conv, conv_w,
           w_out_conv, w_gate_dense, w_up_dense, w_down_dense, w_router, w_gate_moe, w_up_moe,
           w_down_moe):
    B, S, D = x.shape
    depth = mix_norm.shape[0]
    assert depth % 2 == 0, "the final rmsnorm is fused into the last (routed) layer"
    x2 = x.reshape(B * S, D)
    fnorm = final_norm.reshape(1, D)
    w_out_att, w_in_conv, w_out_conv, w_gate_dense, w_up_dense, w_down_dense, w_gate_moe, w_up_moe, w_down_moe = (
        w.astype(BF16) for w in (w_out_att, w_in_conv, w_out_conv, w_gate_dense, w_up_dense, w_down_dense,
                                 w_gate_moe, w_up_moe, w_down_moe))
    for i in range(depth):
        j = i // 2
        mg = mix_norm[i].reshape(1, D)
        fg = ffn_norm[i].reshape(1, D)
        if i % 2 == 0:
            x2 = _attention_layer(x2, B, S, mg, w_in_att, j, b_forget[j], w_out_att)
            x2 = _dense_ffn_layer(x2, fg, w_gate_dense, w_up_dense, w_down_dense, j)
        else:
            x2 = _conv_layer(x2, S, mg, w_in_conv, conv_w[j], w_out_conv, j)
            x2 = _moe_layer(x2, fg, w_router[j], w_gate_moe, w_up_moe, w_down_moe, j,
                            fnorm, final_norm=(i == depth - 1))
    return x2.reshape(B, S, D)
```

```python
import functools
import math

import jax
import jax.numpy as jnp
from jax import lax
from jax.experimental import pallas as pl
from jax.experimental.pallas import tpu as pltpu

F32 = jnp.float32
BF16 = jnp.bfloat16
I32 = jnp.int32
U32 = jnp.uint32

HEAD_DIM = 128
N_HEADS = 8
HEADS_W = N_HEADS * HEAD_DIM
N_EXPERTS = 8
RMS_EPS = 1e-6
CONV_WIDTH = 3
LOG2E = math.log2(math.e)
QK_SCALE_LOG2 = LOG2E / math.sqrt(HEAD_DIM)

V7X_VMEM_LIMIT_BYTES = 56 * 1024 * 1024
LANE = 128
BF16_SUBLANE_TILE = 16
NEG_INF = float("-inf")

_NT = (((1,), (1,)), ((), ()))


def _cp(*sem):
    return pltpu.CompilerParams(dimension_semantics=sem, vmem_limit_bytes=V7X_VMEM_LIMIT_BYTES)


def _tile(n, pref, unit=LANE):
    if n <= pref:
        return n
    t = (pref // unit) * unit
    while t > unit and n % t:
        t -= unit
    assert n % t == 0, (n, pref)
    return t


def _rms(x, g):
    ms = jnp.mean(x * x, axis=-1, keepdims=True)
    return x * lax.rsqrt(ms + RMS_EPS) * g


def _split_bf16(v):
    hi = v.astype(BF16)
    lo = (v - hi.astype(F32)).astype(BF16)
    return hi, lo


def _log_sigmoid_pair(z):
    sp = jnp.log1p(jnp.exp(-jnp.abs(z)))
    return jnp.minimum(z, 0.0) - sp, -jnp.maximum(z, 0.0) - sp


def _silu(g):
    return g / (1.0 + jnp.exp(-g))


def _skinny_nt(wh_ref, wl_ref, hb, hl):
    wh = wh_ref[...]
    out = lax.dot_general(wh, hb, _NT, preferred_element_type=F32)
    out += lax.dot_general(wh, hl, _NT, preferred_element_type=F32)
    out += lax.dot_general(wl_ref[...], hb, _NT, preferred_element_type=F32)
    return out


def _att_in_body(x_ref, g_ref, w_ref, wv_ref, wfh_ref, wfl_ref, o_ref, vt_ref, f_ref, h_scr, *, q_blocks, n_main):
    j = pl.program_id(1)

    @pl.when(j == 0)
    def _():
        h = _rms(x_ref[...], g_ref[...])
        hb, hl = _split_bf16(h)
        h_scr[...] = hb
        f_ref[...] = _skinny_nt(wfh_ref, wfl_ref, hb, hl)

    @pl.when(j < n_main)
    def _():
        o_ref[...] = jnp.dot(h_scr[...], w_ref[...], preferred_element_type=F32).astype(BF16)

    @pl.when(j >= n_main)
    def _():
        acc = lax.dot_general(wv_ref[...], h_scr[...], _NT, preferred_element_type=F32)
        is_q = (j - n_main) < 2 * q_blocks
        vt_ref[...] = (acc * jnp.where(is_q, QK_SCALE_LOG2, 1.0)).astype(BF16)


def _att_in_proj(x2, gnorm, w_k, w_qvt, wf_hi, wf_lo):
    T, D = x2.shape
    N = w_k.shape[1]
    NV = w_qvt.shape[0]
    tm = _tile(T, 1024)
    tn = _tile(HEADS_W, 1024)
    n_main = N // tn
    return pl.pallas_call(
        functools.partial(_att_in_body, q_blocks=HEADS_W // tn, n_main=n_main),
        grid=(T // tm, n_main + NV // tn),
        in_specs=[
            pl.BlockSpec((tm, D), lambda i, j: (i, 0)),
            pl.BlockSpec((1, D), lambda i, j: (0, 0)),
            pl.BlockSpec((D, tn), lambda i, j: (0, jnp.minimum(j, n_main - 1))),
            pl.BlockSpec((tn, D), lambda i, j: (jnp.maximum(j - n_main, 0), 0)),
            pl.BlockSpec((BF16_SUBLANE_TILE, D), lambda i, j: (0, 0)),
            pl.BlockSpec((BF16_SUBLANE_TILE, D), lambda i, j: (0, 0)),
        ],
        out_specs=[
            pl.BlockSpec((tm, tn), lambda i, j: (i, jnp.minimum(j, n_main - 1))),
            pl.BlockSpec((tn, tm), lambda i, j: (jnp.maximum(j - n_main, 0), i)),
            pl.BlockSpec((BF16_SUBLANE_TILE, tm), lambda i, j: (0, i)),
        ],
        out_shape=[
            jax.ShapeDtypeStruct((T, N), BF16),
            jax.ShapeDtypeStruct((NV, T), BF16),
            jax.ShapeDtypeStruct((BF16_SUBLANE_TILE, T), F32),
        ],
        scratch_shapes=[pltpu.VMEM((tm, D), BF16)],
        compiler_params=_cp("parallel", "arbitrary"),
        name="att_in_proj",
    )(x2, gnorm, w_k, w_qvt, wf_hi, wf_lo)


def _forget_cumsum_body(f_ref, b_ref, cb_ref):
    z = f_ref[...] + b_ref[...]
    lf, _ = _log_sigmoid_pair(z)
    S = lf.shape[1]
    lane = lax.broadcasted_iota(I32, lf.shape, 1)
    c = lf
    sh = 1
    while sh < S:
        c = c + jnp.where(lane >= sh, pltpu.roll(c, sh, axis=1), 0.0)
        sh *= 2
    c2 = c * LOG2E
    hi = c2.astype(BF16).astype(F32)
    r1 = c2 - hi
    mid = r1.astype(BF16).astype(F32)
    lo = (r1 - mid).astype(BF16).astype(F32)
    pad = jnp.zeros((LANE - 3 * N_HEADS, S), F32)
    cb_ref[...] = jnp.concatenate([hi, mid, lo, pad], axis=0).T.astype(BF16)


def _forget_cumsum(f16, b_col, B, S):
    return pl.pallas_call(
        _forget_cumsum_body,
        grid=(B,),
        in_specs=[
            pl.BlockSpec((N_HEADS, S), lambda b: (0, b)),
            pl.BlockSpec((N_HEADS, 1), lambda b: (0, 0)),
        ],
        out_specs=pl.BlockSpec((S, LANE), lambda b: (b, 0)),
        out_shape=jax.ShapeDtypeStruct((B * S, LANE), BF16),
        compiler_params=_cp("parallel"),
        name="forget_cumsum",
    )(f16, b_col)


def _head_cols(g):
    return slice(g * HEAD_DIM, (g + 1) * HEAD_DIM)


def _fox_body(qt_ref, k_ref, vt_ref, cb_ref, o_ref, sc_a, sc_b, m_scr, l_scr, acc_scr, *, tq, heads):
    hg = pl.program_id(1)
    i = pl.program_id(2)
    sub = lax.broadcasted_iota(I32, (LANE, tq), 0)
    qs = []
    for g in range(heads):
        h = hg * heads + g
        pick = (sub == h) | (sub == N_HEADS + h) | (sub == 2 * N_HEADS + h)
        qs.append(jnp.concatenate([qt_ref[_head_cols(g), :], jnp.where(pick, -1.0, 0.0).astype(BF16)], axis=0))
    key = lax.broadcasted_iota(I32, (tq, tq), 0)
    qry = lax.broadcasted_iota(I32, (tq, tq), 1)

    def scores(kt, sc_scr):
        ks = pl.multiple_of(kt * tq, tq)
        cb = cb_ref[pl.ds(ks, tq), :]
        for g in range(heads):
            k_aug = jnp.concatenate([k_ref[pl.ds(ks, tq), _head_cols(g)], cb], axis=1)
            sc_scr[g] = jnp.dot(k_aug, qs[g], preferred_element_type=F32)

    def finish(kt, sc_scr, masked):
        ks = pl.multiple_of(kt * tq, tq)
        probs = []
        for g in range(heads):
            m = m_scr[g]
            s = sc_scr[g]
            if masked:
                s = jnp.where(key <= qry, s, NEG_INF)
            m_new = jnp.maximum(m, jnp.max(s, axis=0, keepdims=True))
            alpha = jnp.exp2(m - m_new)
            p = jnp.exp2(s - m_new)
            m_scr[g] = m_new
            l_scr[g] = alpha * l_scr[g] + jnp.sum(p, axis=0, keepdims=True)
            probs.append((alpha, p.astype(BF16)))
        for g in range(heads):
            alpha, p = probs[g]
            vt = vt_ref[_head_cols(g), pl.ds(ks, tq)]
            acc_scr[g] = alpha * acc_scr[g] + jnp.dot(vt, p, preferred_element_type=F32)

    def pair(k2, carry):
        scores(2 * k2 + 1, sc_b)
        finish(2 * k2, sc_a, False)
        scores(2 * k2 + 2, sc_a)
        finish(2 * k2 + 1, sc_b, False)
        return carry

    m_scr[...] = jnp.full(m_scr.shape, NEG_INF, F32)
    l_scr[...] = jnp.zeros(l_scr.shape, F32)
    acc_scr[...] = jnp.zeros(acc_scr.shape, F32)
    scores(0, sc_a)
    lax.fori_loop(0, i // 2, pair, 0)

    @pl.when(i % 2 == 0)
    def _():
        finish(i, sc_a, True)

    @pl.when(i % 2 == 1)
    def _():
        scores(i, sc_b)
        finish(i - 1, sc_a, False)
        finish(i, sc_b, True)

    for g in range(heads):
        o_ref[:, _head_cols(g)] = (acc_scr[g] / l_scr[g]).T.astype(BF16)


ATT_TILE = 256
ATT_HEADS_PER_STEP = 4


def _fox_attention(keys, qvt, cb, B, S):
    T = B * S
    tq = _tile(S, ATT_TILE)
    nq = S // tq
    hp = ATT_HEADS_PER_STEP
    ng = N_HEADS // hp
    w = hp * HEAD_DIM
    return pl.pallas_call(
        functools.partial(_fox_body, tq=tq, heads=hp),
        grid=(B, ng, nq),
        in_specs=[
            pl.BlockSpec((w, tq), lambda b, h, i: (h, b * nq + i)),
            pl.BlockSpec((S, w), lambda b, h, i: (b, h)),
            pl.BlockSpec((w, S), lambda b, h, i: (2 * ng + h, b)),
            pl.BlockSpec((S, LANE), lambda b, h, i: (b, 0)),
        ],
        out_specs=pl.BlockSpec((tq, w), lambda b, h, i: (b * nq + i, h)),
        out_shape=jax.ShapeDtypeStruct((T, HEADS_W), BF16),
        scratch_shapes=[pltpu.VMEM((hp, tq, tq), F32), pltpu.VMEM((hp, tq, tq), F32),
                        pltpu.VMEM((hp, 1, tq), F32), pltpu.VMEM((hp, 1, tq), F32),
                        pltpu.VMEM((hp, HEAD_DIM, tq), F32)],
        compiler_params=_cp("parallel", "parallel", "arbitrary"),
        name="fox_attention",
    )(qvt, keys, qvt, cb)


F32_EXP2_UNDERFLOW = -150.0


def _sb_body(qt_ref, k_ref, vt_ref, ut_ref, o_ref, r_scr, acc_scr, *, tq, heads):
    i = pl.program_id(2)
    qs = [qt_ref[_head_cols(g), :] for g in range(heads)]
    ut = ut_ref[...]
    key = lax.broadcasted_iota(I32, (tq, tq), 0)
    qry = lax.broadcasted_iota(I32, (tq, tq), 1)
    strict = key < qry

    def tile(kt, masked):
        ks = pl.multiple_of(kt * tq, tq)
        zs = [jnp.dot(k_ref[pl.ds(ks, tq), _head_cols(g)], qs[g], preferred_element_type=F32)
              for g in range(heads)]
        mid = []
        for g in range(heads):
            z = zs[g]
            log_beta = jnp.minimum(z, 0.0) - jnp.log2(1.0 + jnp.exp2(-jnp.abs(z)))
            log_om = log_beta - z
            if masked:
                log_om = jnp.where(strict, log_om, 0.0)
            hi, lo = _split_bf16(log_om)
            e = jnp.dot(ut, jnp.concatenate([hi, lo], axis=0), preferred_element_type=F32)
            mid.append((log_beta, log_om, e))
        for g in range(heads):
            log_beta, log_om, e = mid[g]
            r_sum = r_scr[g]
            a = jnp.exp2(log_beta + e + r_sum)
            if masked:
                a = jnp.where(strict, a, 0.0)
            vt = vt_ref[_head_cols(g), pl.ds(ks, tq)]
            acc_scr[g] += jnp.dot(vt, a.astype(BF16), preferred_element_type=F32)
            r_scr[g] = r_sum + jnp.sum(log_om, axis=0, keepdims=True)

    def live():
        return (jnp.max(r_scr[...]) > F32_EXP2_UNDERFLOW).astype(I32)

    r_scr[...] = jnp.zeros(r_scr.shape, F32)
    acc_scr[...] = jnp.zeros(acc_scr.shape, F32)
    tile(i, True)

    def step(state):
        n, _ = state
        tile(i - 1 - n, False)
        return n + 1, live()

    lax.while_loop(lambda st: (st[0] < i) & (st[1] > 0), step, (jnp.int32(0), live()))
    for g in range(heads):
        o_ref[:, _head_cols(g)] = acc_scr[g].T.astype(BF16)


def _sb_attention(keys, qvt, B, S):
    T = B * S
    tq = _tile(S, ATT_TILE)
    nq = S // tq
    hp = ATT_HEADS_PER_STEP
    ng = N_HEADS // hp
    w = hp * HEAD_DIM
    r = lax.broadcasted_iota(I32, (tq, tq), 0)
    c = lax.broadcasted_iota(I32, (tq, tq), 1)
    ut = (c > r).astype(BF16)
    ut = jnp.concatenate([ut, ut], axis=1)
    return pl.pallas_call(
        functools.partial(_sb_body, tq=tq, heads=hp),
        grid=(B, ng, nq),
        in_specs=[
            pl.BlockSpec((w, tq), lambda b, h, i: (ng + h, b * nq + i)),
            pl.BlockSpec((S, w), lambda b, h, i: (b, ng + h)),
            pl.BlockSpec((w, S), lambda b, h, i: (3 * ng + h, b)),
            pl.BlockSpec((tq, 2 * tq), lambda b, h, i: (0, 0)),
        ],
        out_specs=pl.BlockSpec((tq, w), lambda b, h, i: (b * nq + i, h)),
        out_shape=jax.ShapeDtypeStruct((T, HEADS_W), BF16),
        scratch_shapes=[pltpu.VMEM((hp, 1, tq), F32), pltpu.VMEM((hp, HEAD_DIM, tq), F32)],
        compiler_params=_cp("parallel", "parallel", "arbitrary"),
        name="sb_attention",
    )(qvt, keys, qvt, ut)


def _att_out_body(oa_ref, ob_ref, wa_ref, wb_ref, x_ref, o_ref):
    acc = jnp.dot(oa_ref[...], wa_ref[...], preferred_element_type=F32)
    acc += jnp.dot(ob_ref[...], wb_ref[...], preferred_element_type=F32)
    o_ref[...] = x_ref[...] + acc


def _att_out_proj(oa, ob, w_out, layer, x2):
    T, D = x2.shape
    tm = _tile(T, 512)
    return pl.pallas_call(
        _att_out_body,
        grid=(T // tm,),
        in_specs=[
            pl.BlockSpec((tm, HEADS_W), lambda i: (i, 0)),
            pl.BlockSpec((tm, HEADS_W), lambda i: (i, 0)),
            pl.BlockSpec((None, HEADS_W, D), lambda i: (layer, 0, 0)),
            pl.BlockSpec((None, HEADS_W, D), lambda i: (layer, 1, 0)),
            pl.BlockSpec((tm, D), lambda i: (i, 0)),
        ],
        out_specs=pl.BlockSpec((tm, D), lambda i: (i, 0)),
        out_shape=jax.ShapeDtypeStruct((T, D), F32),
        compiler_params=_cp("parallel"),
        name="att_out_proj",
    )(oa, ob, w_out, w_out, x2)


def _ffn_up_body(x_ref, g_ref, wg_ref, wu_ref, a_ref, h_scr):
    @pl.when(pl.program_id(1) == 0)
    def _():
        h_scr[...] = _rms(x_ref[...], g_ref[...]).astype(BF16)

    h = h_scr[...]
    g = jnp.dot(h, wg_ref[...], preferred_element_type=F32)
    u = jnp.dot(h, wu_ref[...], preferred_element_type=F32)
    a_ref[...] = (_silu(g) * u).astype(BF16)


def _ffn_up(x2, gnorm, wg, wu, layer):
    T, D = x2.shape
    F = wg.shape[2]
    tm = _tile(T, 1024)
    tn = _tile(F, 512)
    return pl.pallas_call(
        _ffn_up_body,
        grid=(T // tm, F // tn),
        in_specs=[
            pl.BlockSpec((tm, D), lambda i, j: (i, 0)),
            pl.BlockSpec((1, D), lambda i, j: (0, 0)),
            pl.BlockSpec((None, D, tn), lambda i, j: (layer, 0, j)),
            pl.BlockSpec((None, D, tn), lambda i, j: (layer, 0, j)),
        ],
        out_specs=pl.BlockSpec((tm, tn), lambda i, j: (i, j)),
        out_shape=jax.ShapeDtypeStruct((T, F), BF16),
        scratch_shapes=[pltpu.VMEM((tm, D), BF16)],
        compiler_params=_cp("parallel", "arbitrary"),
        name="ffn_up",
    )(x2, gnorm, wg, wu)


def _ffn_down_body(a_ref, w_ref, x_ref, o_ref):
    o_ref[...] = x_ref[...] + jnp.dot(a_ref[...], w_ref[...], preferred_element_type=F32)


def _ffn_down(a, wd, layer, x2):
    T, D = x2.shape
    F = a.shape[1]
    tm = _tile(T, 1024)
    tn = _tile(D, 512)
    return pl.pallas_call(
        _ffn_down_body,
        grid=(T // tm, D // tn),
        in_specs=[
            pl.BlockSpec((tm, F), lambda i, j: (i, 0)),
            pl.BlockSpec((None, F, tn), lambda i, j: (layer, 0, j)),
            pl.BlockSpec((tm, tn), lambda i, j: (i, j)),
        ],
        out_specs=pl.BlockSpec((tm, tn), lambda i, j: (i, j)),
        out_shape=jax.ShapeDtypeStruct((T, D), F32),
        compiler_params=_cp("parallel", "parallel"),
        name="ffn_down",
    )(a, wd, x2)


def _conv_in_body(x_ref, g_ref, wb_ref, wc_ref, wu_ref, gb_ref, gg_ref, h_scr):
    @pl.when(pl.program_id(1) == 0)
    def _():
        h_scr[...] = _rms(x_ref[...], g_ref[...]).astype(BF16)

    h = h_scr[...]
    gb_ref[...] = jnp.dot(h, wb_ref[...], preferred_element_type=F32).astype(BF16)
    c = jnp.dot(h, wc_ref[...], preferred_element_type=F32)
    u = jnp.dot(h, wu_ref[...], preferred_element_type=F32)
    gg_ref[...] = (c * u).astype(BF16)


def _conv_in(x2, gnorm, w_in, layer):
    T, D = x2.shape
    tm = _tile(T, 1024)
    tn = _tile(D, 512)
    nd = D // tn
    return pl.pallas_call(
        _conv_in_body,
        grid=(T // tm, nd),
        in_specs=[
            pl.BlockSpec((tm, D), lambda i, j: (i, 0)),
            pl.BlockSpec((1, D), lambda i, j: (0, 0)),
            pl.BlockSpec((None, D, tn), lambda i, j: (layer, 0, j)),
            pl.BlockSpec((None, D, tn), lambda i, j: (layer, 0, nd + j)),
            pl.BlockSpec((None, D, tn), lambda i, j: (layer, 0, 2 * nd + j)),
        ],
        out_specs=[
            pl.BlockSpec((tm, tn), lambda i, j: (i, j)),
            pl.BlockSpec((tm, tn), lambda i, j: (i, j)),
        ],
        out_shape=[jax.ShapeDtypeStruct((T, D), BF16), jax.ShapeDtypeStruct((T, D), BF16)],
        scratch_shapes=[pltpu.VMEM((tm, D), BF16)],
        compiler_params=_cp("parallel", "arbitrary"),
        name="conv_in",
    )(x2, gnorm, w_in, w_in, w_in)


def _conv_out_body(gb_ref, g_ref, gp_ref, cw_ref, w_ref, x_ref, o_ref, *, tiles_per_seq):
    i = pl.program_id(0)
    g = g_ref[...].astype(F32)
    tm = g.shape[0]
    keep = jnp.where(i % tiles_per_seq == 0, 0.0, 1.0)
    prev = gp_ref[...].astype(F32) * keep
    p1 = prev[BF16_SUBLANE_TILE - 1:BF16_SUBLANE_TILE, :]
    p2 = prev[BF16_SUBLANE_TILE - 2:BF16_SUBLANE_TILE - 1, :]
    row = lax.broadcasted_iota(I32, g.shape, 0)
    g1 = jnp.where(row == 0, p1, pltpu.roll(g, 1, axis=0))
    g2 = jnp.where(row == 0, p2, jnp.where(row == 1, p1, pltpu.roll(g, 2, axis=0)))
    cw = cw_ref[...]
    conv = g2 * cw[0:1, :] + g1 * cw[1:2, :] + g * cw[2:3, :]
    y = (gb_ref[...].astype(F32) * conv).astype(BF16)
    o_ref[...] = x_ref[...] + jnp.dot(y, w_ref[...], preferred_element_type=F32)


def _conv_out(gb, gg, conv_w, w_out, layer, x2, S):
    T, D = x2.shape
    tm = _tile(S, 256)
    pt = BF16_SUBLANE_TILE
    return pl.pallas_call(
        functools.partial(_conv_out_body, tiles_per_seq=S // tm),
        grid=(T // tm,),
        in_specs=[
            pl.BlockSpec((tm, D), lambda i: (i, 0)),
            pl.BlockSpec((tm, D), lambda i: (i, 0)),
            pl.BlockSpec((pt, D), lambda i: (jnp.maximum(i * (tm // pt) - 1, 0), 0)),
            pl.BlockSpec((CONV_WIDTH, D), lambda i: (0, 0)),
            pl.BlockSpec((None, D, D), lambda i: (layer, 0, 0)),
            pl.BlockSpec((tm, D), lambda i: (i, 0)),
        ],
        out_specs=pl.BlockSpec((tm, D), lambda i: (i, 0)),
        out_shape=jax.ShapeDtypeStruct((T, D), F32),
        compiler_params=_cp("parallel"),
        name="conv_out",
    )(gb, gg, gg, conv_w, w_out, x2)


def _router_body(x_ref, g_ref, wrh_ref, wrl_ref, tri_ref, hp_ref, mi_ref, mf_ref, cnt_ref, carry):
    @pl.when(pl.program_id(0) == 0)
    def _():
        carry[...] = jnp.zeros_like(carry)

    h = _rms(x_ref[...], g_ref[...])
    hb, hl = _split_bf16(h)
    bits = pltpu.bitcast(hb.astype(F32), U32)
    half = bits.shape[1] // 2
    hp_ref[...] = (bits[:, :half] >> 16) | bits[:, half:]

    logits = _skinny_nt(wrh_ref, wrl_ref, hb, hl)[:N_EXPERTS]
    eidx = lax.broadcasted_iota(I32, logits.shape, 0).astype(F32)
    ne = float(N_EXPERTS)
    m1 = jnp.max(logits, axis=0, keepdims=True)
    i1 = jnp.min(jnp.where(logits == m1, eidx, ne), axis=0, keepdims=True)
    rest = jnp.where(eidx == i1, NEG_INF, logits)
    m2 = jnp.max(rest, axis=0, keepdims=True)
    i2 = jnp.min(jnp.where(rest == m2, eidx, ne), axis=0, keepdims=True)
    e21 = jnp.exp(m2 - m1)
    g1 = 1.0 / (1.0 + e21)
    g2 = e21 * g1

    sel = jnp.where((eidx == i1) | (eidx == i2), 1.0, 0.0)
    incl = jnp.dot(sel, tri_ref[...], preferred_element_type=F32)
    pos = carry[...] + incl - sel
    carry[...] = carry[...] + jnp.sum(sel, axis=1, keepdims=True)
    p1 = jnp.sum(jnp.where(eidx == i1, pos, 0.0), axis=0, keepdims=True)
    p2 = jnp.sum(jnp.where(eidx == i2, pos, 0.0), axis=0, keepdims=True)
    meta = jnp.where(eidx == 0, i1, jnp.where(eidx == 1, i2, jnp.where(eidx == 2, p1, jnp.where(eidx == 3, p2, 0.0))))
    mi_ref[...] = meta.astype(I32)
    mf_ref[...] = jnp.where(eidx == 0, g1, jnp.where(eidx == 1, g2, 0.0))
    cnt_ref[...] = jnp.broadcast_to(carry[...], cnt_ref.shape)


def _router(x2, gnorm, wr_hi, wr_lo):
    T, D = x2.shape
    tm = _tile(T, 512)
    r = lax.broadcasted_iota(I32, (tm, tm), 0)
    c = lax.broadcasted_iota(I32, (tm, tm), 1)
    tri = (r <= c).astype(F32)
    return pl.pallas_call(
        _router_body,
        grid=(T // tm,),
        in_specs=[
            pl.BlockSpec((tm, D), lambda i: (i, 0)),
            pl.BlockSpec((1, D), lambda i: (0, 0)),
            pl.BlockSpec((BF16_SUBLANE_TILE, D), lambda i: (0, 0)),
            pl.BlockSpec((BF16_SUBLANE_TILE, D), lambda i: (0, 0)),
            pl.BlockSpec((tm, tm), lambda i: (0, 0)),
        ],
        out_specs=[
            pl.BlockSpec((tm, D // 2), lambda i: (i, 0)),
            pl.BlockSpec((N_EXPERTS, tm), lambda i: (0, i)),
            pl.BlockSpec((N_EXPERTS, tm), lambda i: (0, i)),
            pl.BlockSpec((N_EXPERTS, LANE), lambda i: (0, 0)),
        ],
        out_shape=[
            jax.ShapeDtypeStruct((T, D // 2), U32),
            jax.ShapeDtypeStruct((N_EXPERTS, T), I32),
            jax.ShapeDtypeStruct((N_EXPERTS, T), F32),
            jax.ShapeDtypeStruct((N_EXPERTS, LANE), F32),
        ],
        scratch_shapes=[pltpu.VMEM((N_EXPERTS, 1), F32)],
        compiler_params=_cp("arbitrary"),
        name="moe_router",
    )(x2, gnorm, wr_hi, wr_lo, tri)


ROW_DMA_UNROLL = 8


def _dispatch_body(d1_ref, d2_ref, hp_ref, zero_ref, hs_ref, stage, sems):
    del zero_ref
    i = pl.program_id(0)
    n = hp_ref.shape[0]
    slot = i % 2
    stage[slot] = hp_ref[...]

    def issue(blk, carry):
        for u in range(ROW_DMA_UNROLL):
            r = blk * ROW_DMA_UNROLL + u
            src = stage.at[slot, pl.ds(r, 1)]
            pltpu.make_async_copy(src, hs_ref.at[pl.ds(d1_ref[r], 1)], sems.at[slot]).start(priority=0)
            pltpu.make_async_copy(src, hs_ref.at[pl.ds(d2_ref[r], 1)], sems.at[slot]).start(priority=1)
        return carry

    def drain(which):
        def body(blk, carry):
            row = pltpu.make_async_copy(stage.at[which, pl.ds(0, 1)], hs_ref.at[pl.ds(0, 1)], sems.at[which])
            for _ in range(2 * ROW_DMA_UNROLL):
                row.wait()
            return carry
        lax.fori_loop(0, n // ROW_DMA_UNROLL, body, 0)

    lax.fori_loop(0, n // ROW_DMA_UNROLL, issue, 0)

    @pl.when(i > 0)
    def _():
        drain(1 - slot)

    @pl.when(i == pl.num_programs(0) - 1)
    def _():
        drain(slot)


def _dispatch(hp, d1, d2, P):
    T, Dh = hp.shape
    tm = _tile(T, 256)
    zeros = jnp.zeros((P, Dh), U32)
    return pl.pallas_call(
        _dispatch_body,
        grid=(T // tm,),
        in_specs=[
            pl.BlockSpec((tm,), lambda i: (i,), memory_space=pltpu.SMEM),
            pl.BlockSpec((tm,), lambda i: (i,), memory_space=pltpu.SMEM),
            pl.BlockSpec((tm, Dh), lambda i: (i, 0)),
            pl.BlockSpec(memory_space=pl.ANY),
        ],
        out_specs=pl.BlockSpec(memory_space=pl.ANY),
        out_shape=jax.ShapeDtypeStruct((P, Dh), U32),
        scratch_shapes=[pltpu.VMEM((2, tm, Dh), U32), pltpu.SemaphoreType.DMA((2,))],
        input_output_aliases={3: 0},
        compiler_params=_cp("arbitrary"),
        name="moe_dispatch",
    )(d1, d2, hp, zeros)


def _unpack_rows(words):
    lo = pltpu.bitcast(words << 16, F32)
    hi = pltpu.bitcast(words & jnp.uint32(0xFFFF0000), F32)
    return jnp.concatenate([lo, hi], axis=1).astype(BF16)


def _expert_up_body(te_ref, hs_ref, wg_ref, wu_ref, a_ref, *, n_tiles):
    @pl.when(pl.program_id(1) < te_ref[n_tiles])
    def _():
        h = _unpack_rows(hs_ref[...])
        g = jnp.dot(h, wg_ref[...], preferred_element_type=F32)
        u = jnp.dot(h, wu_ref[...], preferred_element_type=F32)
        a_ref[...] = (_silu(g) * u).astype(BF16)

    @pl.when(pl.program_id(1) >= te_ref[n_tiles])
    def _():
        a_ref[...] = jnp.zeros_like(a_ref)


def _expert_up(tile_table, hs, wg, wu, layer, tme):
    P, Dh = hs.shape
    _, _, D, F = wg.shape
    tn = F // 2 if (F // 2) % LANE == 0 else F
    n_tiles = P // tme
    live = lambda i, te: jnp.minimum(i, te[n_tiles] - 1)
    return pl.pallas_call(
        functools.partial(_expert_up_body, n_tiles=n_tiles),
        grid_spec=pltpu.PrefetchScalarGridSpec(
            num_scalar_prefetch=1,
            grid=(F // tn, n_tiles),
            in_specs=[
                pl.BlockSpec((tme, Dh), lambda j, i, te: (live(i, te), 0)),
                pl.BlockSpec((None, None, D, tn), lambda j, i, te: (layer, te[i], 0, j)),
                pl.BlockSpec((None, None, D, tn), lambda j, i, te: (layer, te[i], 0, j)),
            ],
            out_specs=pl.BlockSpec((tme, tn), lambda j, i, te: (i, j)),
        ),
        out_shape=jax.ShapeDtypeStruct((P, F), BF16),
        compiler_params=_cp("arbitrary", "arbitrary"),
        name="moe_expert_up",
    )(tile_table, hs, wg, wu)


def _expert_down_body(te_ref, a_ref, w_ref, y_ref, *, n_tiles):
    @pl.when(pl.program_id(0) < te_ref[n_tiles])
    def _():
        y_ref[...] = jnp.dot(a_ref[...], w_ref[...], preferred_element_type=F32)

    @pl.when(pl.program_id(0) >= te_ref[n_tiles])
    def _():
        y_ref[...] = jnp.zeros_like(y_ref)


def _expert_down(tile_table, act, wd, layer, tme):
    P, F = act.shape
    D = wd.shape[3]
    n_tiles = P // tme
    return pl.pallas_call(
        functools.partial(_expert_down_body, n_tiles=n_tiles),
        grid_spec=pltpu.PrefetchScalarGridSpec(
            num_scalar_prefetch=1,
            grid=(n_tiles,),
            in_specs=[
                pl.BlockSpec((tme, F), lambda i, te: (jnp.minimum(i, te[n_tiles] - 1), 0)),
                pl.BlockSpec((None, None, F, D), lambda i, te: (layer, te[i], 0, 0)),
            ],
            out_specs=pl.BlockSpec((tme, D), lambda i, te: (i, 0)),
        ),
        out_shape=jax.ShapeDtypeStruct((P, D), F32),
        compiler_params=_cp("arbitrary"),
        name="moe_expert_down",
    )(tile_table, act, wd)


def _combine_body(d1_ref, d2_ref, d1n_ref, d2n_ref, mf_ref, x_ref, fn_ref, y_ref, o_ref, ya, yb, sems,
                  *, final_norm):
    i = pl.program_id(0)
    n = x_ref.shape[0]
    slot = i % 2

    def fetch(ia_ref, ib_ref, to):
        def issue(blk, carry):
            for u in range(ROW_DMA_UNROLL):
                r = blk * ROW_DMA_UNROLL + u
                pltpu.make_async_copy(y_ref.at[pl.ds(ia_ref[r], 1)], ya.at[to, pl.ds(r, 1)],
                                      sems.at[to]).start(priority=0)
                pltpu.make_async_copy(y_ref.at[pl.ds(ib_ref[r], 1)], yb.at[to, pl.ds(r, 1)],
                                      sems.at[to]).start(priority=1)
            return carry
        lax.fori_loop(0, n // ROW_DMA_UNROLL, issue, 0)

    @pl.when(i == 0)
    def _():
        fetch(d1_ref, d2_ref, 0)

    @pl.when(i < pl.num_programs(0) - 1)
    def _():
        fetch(d1n_ref, d2n_ref, 1 - slot)

    def drain(blk, carry):
        row = pltpu.make_async_copy(y_ref.at[pl.ds(0, 1)], ya.at[slot, pl.ds(0, 1)], sems.at[slot])
        for _ in range(2 * ROW_DMA_UNROLL):
            row.wait()
        return carry

    lax.fori_loop(0, n // ROW_DMA_UNROLL, drain, 0)

    gates = mf_ref[...]
    eye = lax.broadcasted_iota(I32, (n, n), 0) == lax.broadcasted_iota(I32, (n, n), 1)
    g1 = jnp.sum(jnp.where(eye, gates[0:1, :], 0.0), axis=1, keepdims=True)
    g2 = jnp.sum(jnp.where(eye, gates[1:2, :], 0.0), axis=1, keepdims=True)
    out = x_ref[...] + (g1 * ya[slot] + g2 * yb[slot])
    if final_norm:
        out = _rms(out, fn_ref[...])
    o_ref[...] = out


def _combine(d1, d2, mf, x2, fnorm, y, final_norm):
    T, D = x2.shape
    tm = _tile(T, 256)
    last = T // tm - 1
    cur = lambda i: (i,)
    nxt = lambda i: (jnp.minimum(i + 1, last),)
    return pl.pallas_call(
        functools.partial(_combine_body, final_norm=final_norm),
        grid=(T // tm,),
        in_specs=[
            pl.BlockSpec((tm,), cur, memory_space=pltpu.SMEM),
            pl.BlockSpec((tm,), cur, memory_space=pltpu.SMEM),
            pl.BlockSpec((tm,), nxt, memory_space=pltpu.SMEM),
            pl.BlockSpec((tm,), nxt, memory_space=pltpu.SMEM),
            pl.BlockSpec((N_EXPERTS, tm), lambda i: (0, i)),
            pl.BlockSpec((tm, D), lambda i: (i, 0)),
            pl.BlockSpec((1, D), lambda i: (0, 0)),
            pl.BlockSpec(memory_space=pl.ANY),
        ],
        out_specs=pl.BlockSpec((tm, D), lambda i: (i, 0)),
        out_shape=jax.ShapeDtypeStruct((T, D), F32),
        scratch_shapes=[pltpu.VMEM((2, tm, D), F32), pltpu.VMEM((2, tm, D), F32),
                        pltpu.SemaphoreType.DMA((2,))],
        compiler_params=_cp("arbitrary"),
        name="moe_combine",
    )(d1, d2, d1, d2, mf, x2, fnorm, y)


def _pad_rows16(w_t):
    pad = jnp.zeros((BF16_SUBLANE_TILE - w_t.shape[0], w_t.shape[1]), F32)
    w = jnp.concatenate([w_t, pad], axis=0)
    hi = w.astype(BF16)
    lo = (w - hi.astype(F32)).astype(BF16)
    return hi, lo


def _attention_layer(x2, B, S, gnorm, w_in_all, layer, b_forget, w_out):
    sb0 = 3 * HEADS_W + N_HEADS
    col = lambda part: w_in_all[layer, :, part * HEADS_W:(part + 1) * HEADS_W]
    sbc = lambda part: w_in_all[layer, :, sb0 + part * HEADS_W:sb0 + (part + 1) * HEADS_W]
    w_k = jnp.concatenate([col(1), sbc(1)], axis=1).astype(BF16)
    w_qvt = jnp.concatenate([col(0), sbc(0), col(2), sbc(2)], axis=1).T.astype(BF16)
    wf_hi, wf_lo = _pad_rows16(w_in_all[layer, :, 3 * HEADS_W:sb0].T)
    keys, qvt, f16 = _att_in_proj(x2, gnorm, w_k, w_qvt, wf_hi, wf_lo)
    cb = _forget_cumsum(f16, b_forget.reshape(N_HEADS, 1), B, S)
    oa = _fox_attention(keys, qvt, cb, B, S)
    ob = _sb_attention(keys, qvt, B, S)
    return _att_out_proj(oa, ob, w_out, layer, x2)


def _dense_ffn_layer(x2, gnorm, wg, wu, wd, layer):
    a = _ffn_up(x2, gnorm, wg, wu, layer)
    return _ffn_down(a, wd, layer, x2)


def _conv_layer(x2, S, gnorm, w_in, conv_w, w_out, layer):
    gb, gg = _conv_in(x2, gnorm, w_in, layer)
    return _conv_out(gb, gg, conv_w, w_out, layer, x2, S)


def _moe_layer(x2, gnorm, w_router, wg, wu, wd, layer, fnorm, final_norm):
    T, D = x2.shape
    tme = 512 if T >= 4096 else 128
    n_tiles = (2 * T) // tme + N_EXPERTS
    P = n_tiles * tme

    wr_hi, wr_lo = _pad_rows16(w_router.T)
    hp, mi, mf, cnt = _router(x2, gnorm, wr_hi, wr_lo)

    counts = cnt[:, 0].astype(I32)
    padded = ((counts + tme - 1) // tme) * tme
    ends = jnp.cumsum(padded)
    offs = ends - padded
    d1 = jnp.take(offs, mi[0]) + mi[2]
    d2 = jnp.take(offs, mi[1]) + mi[3]
    tile_start = jnp.arange(n_tiles, dtype=I32) * tme
    tile_expert = jnp.sum((tile_start[:, None] >= ends[None, :]).astype(I32), axis=1)
    tile_expert = jnp.minimum(tile_expert, N_EXPERTS - 1)
    tile_table = jnp.concatenate([tile_expert, (ends[-1:] // tme).astype(I32)])

    hs = _dispatch(hp, d1, d2, P)
    act = _expert_up(tile_table, hs, wg, wu, layer, tme)
    y = _expert_down(tile_table, act, wd, layer, tme)
    return _combine(d1, d2, mf, x2, fnorm, y, final_norm)


def kernel(x, mix_norm, ffn_norm, final_norm, w_in_att, b_forget, w_out_att, w_in_conv, conv_w,
           w_out_conv, w_gate_dense, w_up_dense, w_down_dense, w_router, w_gate_moe, w_up_moe,
           w_down_moe):
    B, S, D = x.shape
    depth = mix_norm.shape[0]
    assert depth % 2 == 0, "the final rmsnorm is fused into the last (routed) layer"
    x2 = x.reshape(B * S, D)
    fnorm = final_norm.reshape(1, D)
    w_out_att, w_in_conv, w_out_conv, w_gate_dense, w_up_dense, w_down_dense, w_gate_moe, w_up_moe, w_down_moe = (
        w.astype(BF16) for w in (w_out_att, w_in_conv, w_out_conv, w_gate_dense, w_up_dense, w_down_dense,
                                 w_gate_moe, w_up_moe, w_down_moe))
    for i in range(depth):
        j = i // 2
        mg = mix_norm[i].reshape(1, D)
        fg = ffn_norm[i].reshape(1, D)
        if i % 2 == 0:
            x2 = _attention_layer(x2, B, S, mg, w_in_att, j, b_forget[j], w_out_att)
            x2 = _dense_ffn_layer(x2, fg, w_gate_dense, w_up_dense, w_down_dense, j)
        else:
            x2 = _conv_layer(x2, S, mg, w_in_conv, conv_w[j], w_out_conv, j)
            x2 = _moe_layer(x2, fg, w_router[j], w_gate_moe, w_up_moe, w_down_moe, j,
                            fnorm, final_norm=(i == depth - 1))
    return x2.reshape(B, S, D)
```

```python
import functools
import math

import jax
import jax.numpy as jnp
from jax import lax
from jax.experimental import pallas as pl
from jax.experimental.pallas import tpu as pltpu

F32 = jnp.float32
BF16 = jnp.bfloat16
I32 = jnp.int32
U32 = jnp.uint32

HEAD_DIM = 128
N_HEADS = 8
HEADS_W = N_HEADS * HEAD_DIM
N_EXPERTS = 8
RMS_EPS = 1e-6
CONV_WIDTH = 3
LOG2E = math.log2(math.e)
QK_SCALE_LOG2 = LOG2E / math.sqrt(HEAD_DIM)

V7X_VMEM_LIMIT_BYTES = 56 * 1024 * 1024
LANE = 128
BF16_SUBLANE_TILE = 16
NEG_INF = float("-inf")

_NT = (((1,), (1,)), ((), ()))


def _cp(*sem):
    return pltpu.CompilerParams(dimension_semantics=sem, vmem_limit_bytes=V7X_VMEM_LIMIT_BYTES)


def _tile(n, pref, unit=LANE):
    if n <= pref:
        return n
    t = (pref // unit) * unit
    while t > unit and n % t:
        t -= unit
    assert n % t == 0, (n, pref)
    return t


def _rms(x, g):
    ms = jnp.mean(x * x, axis=-1, keepdims=True)
    return x * lax.rsqrt(ms + RMS_EPS) * g


def _split_bf16(v):
    hi = v.astype(BF16)
    lo = (v - hi.astype(F32)).astype(BF16)
    return hi, lo


def _log_sigmoid_pair(z):
    sp = jnp.log1p(jnp.exp(-jnp.abs(z)))
    return jnp.minimum(z, 0.0) - sp, -jnp.maximum(z, 0.0) - sp


def _silu(g):
    return g / (1.0 + jnp.exp(-g))


def _skinny_nt(wh_ref, wl_ref, hb, hl):
    wh = wh_ref[...]
    out = lax.dot_general(wh, hb, _NT, preferred_element_type=F32)
    out += lax.dot_general(wh, hl, _NT, preferred_element_type=F32)
    out += lax.dot_general(wl_ref[...], hb, _NT, preferred_element_type=F32)
    return out


def _att_in_body(x_ref, g_ref, w_ref, wv_ref, wfh_ref, wfl_ref, o_ref, vt_ref, f_ref, h_scr, *, q_blocks, n_main):
    j = pl.program_id(1)

    @pl.when(j == 0)
    def _():
        h = _rms(x_ref[...], g_ref[...])
        hb, hl = _split_bf16(h)
        h_scr[...] = hb
        f_ref[...] = _skinny_nt(wfh_ref, wfl_ref, hb, hl)

    @pl.when(j < n_main)
    def _():
        o_ref[...] = jnp.dot(h_scr[...], w_ref[...], preferred_element_type=F32).astype(BF16)

    @pl.when(j >= n_main)
    def _():
        acc = lax.dot_general(wv_ref[...], h_scr[...], _NT, preferred_element_type=F32)
        is_q = (j - n_main) < 2 * q_blocks
        vt_ref[...] = (acc * jnp.where(is_q, QK_SCALE_LOG2, 1.0)).astype(BF16)


def _att_in_proj(x2, gnorm, w_k, w_qvt, wf_hi, wf_lo):
    T, D = x2.shape
    N = w_k.shape[1]
    NV = w_qvt.shape[0]
    tm = _tile(T, 1024)
    tn = _tile(HEADS_W, 1024)
    n_main = N // tn
    return pl.pallas_call(
        functools.partial(_att_in_body, q_blocks=HEADS_W // tn, n_main=n_main),
        grid=(T // tm, n_main + NV // tn),
        in_specs=[
            pl.BlockSpec((tm, D), lambda i, j: (i, 0)),
            pl.BlockSpec((1, D), lambda i, j: (0, 0)),
            pl.BlockSpec((D, tn), lambda i, j: (0, jnp.minimum(j, n_main - 1))),
            pl.BlockSpec((tn, D), lambda i, j: (jnp.maximum(j - n_main, 0), 0)),
            pl.BlockSpec((BF16_SUBLANE_TILE, D), lambda i, j: (0, 0)),
            pl.BlockSpec((BF16_SUBLANE_TILE, D), lambda i, j: (0, 0)),
        ],
        out_specs=[
            pl.BlockSpec((tm, tn), lambda i, j: (i, jnp.minimum(j, n_main - 1))),
            pl.BlockSpec((tn, tm), lambda i, j: (jnp.maximum(j - n_main, 0), i)),
            pl.BlockSpec((BF16_SUBLANE_TILE, tm), lambda i, j: (0, i)),
        ],
        out_shape=[
            jax.ShapeDtypeStruct((T, N), BF16),
            jax.ShapeDtypeStruct((NV, T), BF16),
            jax.ShapeDtypeStruct((BF16_SUBLANE_TILE, T), F32),
        ],
        scratch_shapes=[pltpu.VMEM((tm, D), BF16)],
        compiler_params=_cp("parallel", "arbitrary"),
        name="att_in_proj",
    )(x2, gnorm, w_k, w_qvt, wf_hi, wf_lo)


def _forget_cumsum_body(f_ref, b_ref, cb_ref):
    z = f_ref[...] + b_ref[...]
    lf, _ = _log_sigmoid_pair(z)
    S = lf.shape[1]
    lane = lax.broadcasted_iota(I32, lf.shape, 1)
    c = lf
    sh = 1
    while sh < S:
        c = c + jnp.where(lane >= sh, pltpu.roll(c, sh, axis=1), 0.0)
        sh *= 2
    c2 = c * LOG2E
    hi = c2.astype(BF16).astype(F32)
    r1 = c2 - hi
    mid = r1.astype(BF16).astype(F32)
    lo = (r1 - mid).astype(BF16).astype(F32)
    pad = jnp.zeros((LANE - 3 * N_HEADS, S), F32)
    cb_ref[...] = jnp.concatenate([hi, mid, lo, pad], axis=0).T.astype(BF16)


def _forget_cumsum(f16, b_col, B, S):
    return pl.pallas_call(
        _forget_cumsum_body,
        grid=(B,),
        in_specs=[
            pl.BlockSpec((N_HEADS, S), lambda b: (0, b)),
            pl.BlockSpec((N_HEADS, 1), lambda b: (0, 0)),
        ],
        out_specs=pl.BlockSpec((S, LANE), lambda b: (b, 0)),
        out_shape=jax.ShapeDtypeStruct((B * S, LANE), BF16),
        compiler_params=_cp("parallel"),
        name="forget_cumsum",
    )(f16, b_col)


def _head_cols(g):
    return slice(g * HEAD_DIM, (g + 1) * HEAD_DIM)


def _fox_body(qt_ref, k_ref, vt_ref, cb_ref, o_ref, sc_a, sc_b, m_scr, l_scr, acc_scr, *, tq, heads):
    hg = pl.program_id(1)
    i = pl.program_id(2)
    sub = lax.broadcasted_iota(I32, (LANE, tq), 0)
    qs = []
    for g in range(heads):
        h = hg * heads + g
        pick = (sub == h) | (sub == N_HEADS + h) | (sub == 2 * N_HEADS + h)
        qs.append(jnp.concatenate([qt_ref[_head_cols(g), :], jnp.where(pick, -1.0, 0.0).astype(BF16)], axis=0))
    key = lax.broadcasted_iota(I32, (tq, tq), 0)
    qry = lax.broadcasted_iota(I32, (tq, tq), 1)

    def scores(kt, sc_scr):
        ks = pl.multiple_of(kt * tq, tq)
        cb = cb_ref[pl.ds(ks, tq), :]
        for g in range(heads):
            k_aug = jnp.concatenate([k_ref[pl.ds(ks, tq), _head_cols(g)], cb], axis=1)
            sc_scr[g] = jnp.dot(k_aug, qs[g], preferred_element_type=F32)

    def finish(kt, sc_scr, masked):
        ks = pl.multiple_of(kt * tq, tq)
        probs = []
        for g in range(heads):
            m = m_scr[g]
            s = sc_scr[g]
            if masked:
                s = jnp.where(key <= qry, s, NEG_INF)
            m_new = jnp.maximum(m, jnp.max(s, axis=0, keepdims=True))
            alpha = jnp.exp2(m - m_new)
            p = jnp.exp2(s - m_new)
            m_scr[g] = m_new
            l_scr[g] = alpha * l_scr[g] + jnp.sum(p, axis=0, keepdims=True)
            probs.append((alpha, p.astype(BF16)))
        for g in range(heads):
            alpha, p = probs[g]
            vt = vt_ref[_head_cols(g), pl.ds(ks, tq)]
            acc_scr[g] = alpha * acc_scr[g] + jnp.dot(vt, p, preferred_element_type=F32)

    def pair(k2, carry):
        scores(2 * k2 + 1, sc_b)
        finish(2 * k2, sc_a, False)
        scores(2 * k2 + 2, sc_a)
        finish(2 * k2 + 1, sc_b, False)
        return carry

    m_scr[...] = jnp.full(m_scr.shape, NEG_INF, F32)
    l_scr[...] = jnp.zeros(l_scr.shape, F32)
    acc_scr[...] = jnp.zeros(acc_scr.shape, F32)
    scores(0, sc_a)
    lax.fori_loop(0, i // 2, pair, 0)

    @pl.when(i % 2 == 0)
    def _():
        finish(i, sc_a, True)

    @pl.when(i % 2 == 1)
    def _():
        scores(i, sc_b)
        finish(i - 1, sc_a, False)
        finish(i, sc_b, True)

    for g in range(heads):
        o_ref[:, _head_cols(g)] = (acc_scr[g] / l_scr[g]).T.astype(BF16)


ATT_TILE = 256
ATT_HEADS_PER_STEP = 4


def _fox_attention(keys, qvt, cb, B, S):
    T = B * S
    tq = _tile(S, ATT_TILE)
    nq = S // tq
    hp = ATT_HEADS_PER_STEP
    ng = N_HEADS // hp
    w = hp * HEAD_DIM
    return pl.pallas_call(
        functools.partial(_fox_body, tq=tq, heads=hp),
        grid=(B, ng, nq),
        in_specs=[
            pl.BlockSpec((w, tq), lambda b, h, i: (h, b * nq + i)),
            pl.BlockSpec((S, w), lambda b, h, i: (b, h)),
            pl.BlockSpec((w, S), lambda b, h, i: (2 * ng + h, b)),
            pl.BlockSpec((S, LANE), lambda b, h, i: (b, 0)),
        ],
        out_specs=pl.BlockSpec((tq, w), lambda b, h, i: (b * nq + i, h)),
        out_shape=jax.ShapeDtypeStruct((T, HEADS_W), BF16),
        scratch_shapes=[pltpu.VMEM((hp, tq, tq), F32), pltpu.VMEM((hp, tq, tq), F32),
                        pltpu.VMEM((hp, 1, tq), F32), pltpu.VMEM((hp, 1, tq), F32),
                        pltpu.VMEM((hp, HEAD_DIM, tq), F32)],
        compiler_params=_cp("parallel", "parallel", "arbitrary"),
        name="fox_attention",
    )(qvt, keys, qvt, cb)


F32_EXP2_UNDERFLOW = -150.0


def _sb_body(qt_ref, k_ref, vt_ref, ut_ref, o_ref, r_scr, acc_scr, *, tq, heads):
    i = pl.program_id(2)
    qs = [qt_ref[_head_cols(g), :] for g in range(heads)]
    ut = ut_ref[...]
    key = lax.broadcasted_iota(I32, (tq, tq), 0)
    qry = lax.broadcasted_iota(I32, (tq, tq), 1)
    strict = key < qry

    def tile(kt, masked):
        ks = pl.multiple_of(kt * tq, tq)
        zs = [jnp.dot(k_ref[pl.ds(ks, tq), _head_cols(g)], qs[g], preferred_element_type=F32)
              for g in range(heads)]
        mid = []
        for g in range(heads):
            z = zs[g]
            log_beta = jnp.minimum(z, 0.0) - jnp.log2(1.0 + jnp.exp2(-jnp.abs(z)))
            log_om = log_beta - z
            if masked:
                log_om = jnp.where(strict, log_om, 0.0)
            hi, lo = _split_bf16(log_om)
            e = jnp.dot(ut, jnp.concatenate([hi, lo], axis=0), preferred_element_type=F32)
            mid.append((log_beta, log_om, e))
        for g in range(heads):
            log_beta, log_om, e = mid[g]
            r_sum = r_scr[g]
            a = jnp.exp2(log_beta + e + r_sum)
            if masked:
                a = jnp.where(strict, a, 0.0)
            vt = vt_ref[_head_cols(g), pl.ds(ks, tq)]
            acc_scr[g] += jnp.dot(vt, a.astype(BF16), preferred_element_type=F32)
            r_scr[g] = r_sum + jnp.sum(log_om, axis=0, keepdims=True)

    def live():
        return (jnp.max(r_scr[...]) > F32_EXP2_UNDERFLOW).astype(I32)

    r_scr[...] = jnp.zeros(r_scr.shape, F32)
    acc_scr[...] = jnp.zeros(acc_scr.shape, F32)
    tile(i, True)

    def step(state):
        n, _ = state
        tile(i - 1 - n, False)
        return n + 1, live()

    lax.while_loop(lambda st: (st[0] < i) & (st[1] > 0), step, (jnp.int32(0), live()))
    for g in range(heads):
        o_ref[:, _head_cols(g)] = acc_scr[g].T.astype(BF16)


def _sb_attention(keys, qvt, B, S):
    T = B * S
    tq = _tile(S, ATT_TILE)
    nq = S // tq
    hp = ATT_HEADS_PER_STEP
    ng = N_HEADS // hp
    w = hp * HEAD_DIM
    r = lax.broadcasted_iota(I32, (tq, tq), 0)
    c = lax.broadcasted_iota(I32, (tq, tq), 1)
    ut = (c > r).astype(BF16)
    ut = jnp.concatenate([ut, ut], axis=1)
    return pl.pallas_call(
        functools.partial(_sb_body, tq=tq, heads=hp),
        grid=(B, ng, nq),
        in_specs=[
            pl.BlockSpec((w, tq), lambda b, h, i: (ng + h, b * nq + i)),
            pl.BlockSpec((S, w), lambda b, h, i: (b, ng + h)),
            pl.BlockSpec((w, S), lambda b, h, i: (3 * ng + h, b)),
            pl.BlockSpec((tq, 2 * tq), lambda b, h, i: (0, 0)),
        ],
        out_specs=pl.BlockSpec((tq, w), lambda b, h, i: (b * nq + i, h)),
        out_shape=jax.ShapeDtypeStruct((T, HEADS_W), BF16),
        scratch_shapes=[pltpu.VMEM((hp, 1, tq), F32), pltpu.VMEM((hp, HEAD_DIM, tq), F32)],
        compiler_params=_cp("parallel", "parallel", "arbitrary"),
        name="sb_attention",
    )(qvt, keys, qvt, ut)


def _att_out_body(oa_ref, ob_ref, wa_ref, wb_ref, x_ref, o_ref):
    acc = jnp.dot(oa_ref[...], wa_ref[...], preferred_element_type=F32)
    acc += jnp.dot(ob_ref[...], wb_ref[...], preferred_element_type=F32)
    o_ref[...] = x_ref[...] + acc


def _att_out_proj(oa, ob, w_out, layer, x2):
    T, D = x2.shape
    tm = _tile(T, 512)
    return pl.pallas_call(
        _att_out_body,
        grid=(T // tm,),
        in_specs=[
            pl.BlockSpec((tm, HEADS_W), lambda i: (i, 0)),
            pl.BlockSpec((tm, HEADS_W), lambda i: (i, 0)),
            pl.BlockSpec((None, HEADS_W, D), lambda i: (layer, 0, 0)),
            pl.BlockSpec((None, HEADS_W, D), lambda i: (layer, 1, 0)),
            pl.BlockSpec((tm, D), lambda i: (i, 0)),
        ],
        out_specs=pl.BlockSpec((tm, D), lambda i: (i, 0)),
        out_shape=jax.ShapeDtypeStruct((T, D), F32),
        compiler_params=_cp("parallel"),
        name="att_out_proj",
    )(oa, ob, w_out, w_out, x2)


def _ffn_up_body(x_ref, g_ref, wg_ref, wu_ref, a_ref, h_scr):
    @pl.when(pl.program_id(1) == 0)
    def _():
        h_scr[...] = _rms(x_ref[...], g_ref[...]).astype(BF16)

    h = h_scr[...]
    g = jnp.dot(h, wg_ref[...], preferred_element_type=F32)
    u = jnp.dot(h, wu_ref[...], preferred_element_type=F32)
    a_ref[...] = (_silu(g) * u).astype(BF16)


def _ffn_up(x2, gnorm, wg, wu, layer):
    T, D = x2.shape
    F = wg.shape[2]
    tm = _tile(T, 1024)
    tn = _tile(F, 512)
    return pl.pallas_call(
        _ffn_up_body,
        grid=(T // tm, F // tn),
        in_specs=[
            pl.BlockSpec((tm, D), lambda i, j: (i, 0)),
            pl.BlockSpec((1, D), lambda i, j: (0, 0)),
            pl.BlockSpec((None, D, tn), lambda i, j: (layer, 0, j)),
            pl.BlockSpec((None, D, tn), lambda i, j: (layer, 0, j)),
        ],
        out_specs=pl.BlockSpec((tm, tn), lambda i, j: (i, j)),
        out_shape=jax.ShapeDtypeStruct((T, F), BF16),
        scratch_shapes=[pltpu.VMEM((tm, D), BF16)],
        compiler_params=_cp("parallel", "arbitrary"),
        name="ffn_up",
    )(x2, gnorm, wg, wu)


def _ffn_down_body(a_ref, w_ref, x_ref, o_ref):
    o_ref[...] = x_ref[...] + jnp.dot(a_ref[...], w_ref[...], preferred_element_type=F32)


def _ffn_down(a, wd, layer, x2):
    T, D = x2.shape
    F = a.shape[1]
    tm = _tile(T, 1024)
    tn = _tile(D, 512)
    return pl.pallas_call(
        _ffn_down_body,
        grid=(T // tm, D // tn),
        in_specs=[
            pl.BlockSpec((tm, F), lambda i, j: (i, 0)),
            pl.BlockSpec((None, F, tn), lambda i, j: (layer, 0, j)),
            pl.BlockSpec((tm, tn), lambda i, j: (i, j)),
        ],
        out_specs=pl.BlockSpec((tm, tn), lambda i, j: (i, j)),
        out_shape=jax.ShapeDtypeStruct((T, D), F32),
        compiler_params=_cp("parallel", "parallel"),
        name="ffn_down",
    )(a, wd, x2)


def _conv_in_body(x_ref, g_ref, wb_ref, wc_ref, wu_ref, gb_ref, gg_ref, h_scr):
    @pl.when(pl.program_id(1) == 0)
    def _():
        h_scr[...] = _rms(x_ref[...], g_ref[...]).astype(BF16)

    h = h_scr[...]
    gb_ref[...] = jnp.dot(h, wb_ref[...], preferred_element_type=F32).astype(BF16)
    c = jnp.dot(h, wc_ref[...], preferred_element_type=F32)
    u = jnp.dot(h, wu_ref[...], preferred_element_type=F32)
    gg_ref[...] = (c * u).astype(BF16)


def _conv_in(x2, gnorm, w_in, layer):
    T, D = x2.shape
    tm = _tile(T, 1024)
    tn = _tile(D, 512)
    nd = D // tn
    return pl.pallas_call(
        _conv_in_body,
        grid=(T // tm, nd),
        in_specs=[
            pl.BlockSpec((tm, D), lambda i, j: (i, 0)),
            pl.BlockSpec((1, D), lambda i, j: (0, 0)),
            pl.BlockSpec((None, D, tn), lambda i, j: (layer, 0, j)),
            pl.BlockSpec((None, D, tn), lambda i, j: (layer, 0, nd + j)),
            pl.BlockSpec((None, D, tn), lambda i, j: (layer, 0, 2 * nd + j)),
        ],
        out_specs=[
            pl.BlockSpec((tm, tn), lambda i, j: (i, j)),
            pl.BlockSpec((tm, tn), lambda i, j: (i, j)),
        ],
        out_shape=[jax.ShapeDtypeStruct((T, D), BF16), jax.ShapeDtypeStruct((T, D), BF16)],
        scratch_shapes=[pltpu.VMEM((tm, D), BF16)],
        compiler_params=_cp("parallel", "arbitrary"),
        name="conv_in",
    )(x2, gnorm, w_in, w_in, w_in)


def _conv_out_body(gb_ref, g_ref, gp_ref, cw_ref, w_ref, x_ref, o_ref, *, tiles_per_seq):
    i = pl.program_id(0)
    g = g_ref[...].astype(F32)
    tm = g.shape[0]
    keep = jnp.where(i % tiles_per_seq == 0, 0.0, 1.0)
    prev = gp_ref[...].astype(F32) * keep
    p1 = prev[BF16_SUBLANE_TILE - 1:BF16_SUBLANE_TILE, :]
    p2 = prev[BF16_SUBLANE_TILE - 2:BF16_SUBLANE_TILE - 1, :]
    row = lax.broadcasted_iota(I32, g.shape, 0)
    g1 = jnp.where(row == 0, p1, pltpu.roll(g, 1, axis=0))
    g2 = jnp.where(row == 0, p2, jnp.where(row == 1, p1, pltpu.roll(g, 2, axis=0)))
    cw = cw_ref[...]
    conv = g2 * cw[0:1, :] + g1 * cw[1:2, :] + g * cw[2:3, :]
    y = (gb_ref[...].astype(F32) * conv).astype(BF16)
    o_ref[...] = x_ref[...] + jnp.dot(y, w_ref[...], preferred_element_type=F32)


def _conv_out(gb, gg, conv_w, w_out, layer, x2, S):
    T, D = x2.shape
    tm = _tile(S, 256)
    pt = BF16_SUBLANE_TILE
    return pl.pallas_call(
        functools.partial(_conv_out_body, tiles_per_seq=S // tm),
        grid=(T // tm,),
        in_specs=[
            pl.BlockSpec((tm, D), lambda i: (i, 0)),
            pl.BlockSpec((tm, D), lambda i: (i, 0)),
            pl.BlockSpec((pt, D), lambda i: (jnp.maximum(i * (tm // pt) - 1, 0), 0)),
            pl.BlockSpec((CONV_WIDTH, D), lambda i: (0, 0)),
            pl.BlockSpec((None, D, D), lambda i: (layer, 0, 0)),
            pl.BlockSpec((tm, D), lambda i: (i, 0)),
        ],
        out_specs=pl.BlockSpec((tm, D), lambda i: (i, 0)),
        out_shape=jax.ShapeDtypeStruct((T, D), F32),
        compiler_params=_cp("parallel"),
        name="conv_out",
    )(gb, gg, gg, conv_w, w_out, x2)


def _router_body(x_ref, g_ref, wrh_ref, wrl_ref, tri_ref, hp_ref, mi_ref, mf_ref, cnt_ref, carry):
    @pl.when(pl.program_id(0) == 0)
    def _():
        carry[...] = jnp.zeros_like(carry)

    h = _rms(x_ref[...], g_ref[...])
    hb, hl = _split_bf16(h)
    bits = pltpu.bitcast(hb.astype(F32), U32)
    half = bits.shape[1] // 2
    hp_ref[...] = (bits[:, :half] >> 16) | bits[:, half:]

    logits = _skinny_nt(wrh_ref, wrl_ref, hb, hl)[:N_EXPERTS]
    eidx = lax.broadcasted_iota(I32, logits.shape, 0).astype(F32)
    ne = float(N_EXPERTS)
    m1 = jnp.max(logits, axis=0, keepdims=True)
    i1 = jnp.min(jnp.where(logits == m1, eidx, ne), axis=0, keepdims=True)
    rest = jnp.where(eidx == i1, NEG_INF, logits)
    m2 = jnp.max(rest, axis=0, keepdims=True)
    i2 = jnp.min(jnp.where(rest == m2, eidx, ne), axis=0, keepdims=True)
    e21 = jnp.exp(m2 - m1)
    g1 = 1.0 / (1.0 + e21)
    g2 = e21 * g1

    sel = jnp.where((eidx == i1) | (eidx == i2), 1.0, 0.0)
    incl = jnp.dot(sel, tri_ref[...], preferred_element_type=F32)
    pos = carry[...] + incl - sel
    carry[...] = carry[...] + jnp.sum(sel, axis=1, keepdims=True)
    p1 = jnp.sum(jnp.where(eidx == i1, pos, 0.0), axis=0, keepdims=True)
    p2 = jnp.sum(jnp.where(eidx == i2, pos, 0.0), axis=0, keepdims=True)
    meta = jnp.where(eidx == 0, i1, jnp.where(eidx == 1, i2, jnp.where(eidx == 2, p1, jnp.where(eidx == 3, p2, 0.0))))
    mi_ref[...] = meta.astype(I32)
    mf_ref[...] = jnp.where(eidx == 0, g1, jnp.where(eidx == 1, g2, 0.0))
    cnt_ref[...] = jnp.broadcast_to(carry[...], cnt_ref.shape)


def _router(x2, gnorm, wr_hi, wr_lo):
    T, D = x2.shape
    tm = _tile(T, 512)
    r = lax.broadcasted_iota(I32, (tm, tm), 0)
    c = lax.broadcasted_iota(I32, (tm, tm), 1)
    tri = (r <= c).astype(F32)
    return pl.pallas_call(
        _router_body,
        grid=(T // tm,),
        in_specs=[
            pl.BlockSpec((tm, D), lambda i: (i, 0)),
            pl.BlockSpec((1, D), lambda i: (0, 0)),
            pl.BlockSpec((BF16_SUBLANE_TILE, D), lambda i: (0, 0)),
            pl.BlockSpec((BF16_SUBLANE_TILE, D), lambda i: (0, 0)),
            pl.BlockSpec((tm, tm), lambda i: (0, 0)),
        ],
        out_specs=[
            pl.BlockSpec((tm, D // 2), lambda i: (i, 0)),
            pl.BlockSpec((N_EXPERTS, tm), lambda i: (0, i)),
            pl.BlockSpec((N_EXPERTS, tm), lambda i: (0, i)),
            pl.BlockSpec((N_EXPERTS, LANE), lambda i: (0, 0)),
        ],
        out_shape=[
            jax.ShapeDtypeStruct((T, D // 2), U32),
            jax.ShapeDtypeStruct((N_EXPERTS, T), I32),
            jax.ShapeDtypeStruct((N_EXPERTS, T), F32),
            jax.ShapeDtypeStruct((N_EXPERTS, LANE), F32),
        ],
        scratch_shapes=[pltpu.VMEM((N_EXPERTS, 1), F32)],
        compiler_params=_cp("arbitrary"),
        name="moe_router",
    )(x2, gnorm, wr_hi, wr_lo, tri)


ROW_DMA_UNROLL = 8


def _dispatch_body(d1_ref, d2_ref, hp_ref, zero_ref, hs_ref, stage, sems):
    del zero_ref
    i = pl.program_id(0)
    n = hp_ref.shape[0]
    slot = i % 2
    stage[slot] = hp_ref[...]

    def issue(blk, carry):
        for u in range(ROW_DMA_UNROLL):
            r = blk * ROW_DMA_UNROLL + u
            src = stage.at[slot, pl.ds(r, 1)]
            pltpu.make_async_copy(src, hs_ref.at[pl.ds(d1_ref[r], 1)], sems.at[slot]).start(priority=0)
            pltpu.make_async_copy(src, hs_ref.at[pl.ds(d2_ref[r], 1)], sems.at[slot]).start(priority=1)
        return carry

    def drain(which):
        def body(blk, carry):
            row = pltpu.make_async_copy(stage.at[which, pl.ds(0, 1)], hs_ref.at[pl.ds(0, 1)], sems.at[which])
            for _ in range(2 * ROW_DMA_UNROLL):
                row.wait()
            return carry
        lax.fori_loop(0, n // ROW_DMA_UNROLL, body, 0)

    lax.fori_loop(0, n // ROW_DMA_UNROLL, issue, 0)

    @pl.when(i > 0)
    def _():
        drain(1 - slot)

    @pl.when(i == pl.num_programs(0) - 1)
    def _():
        drain(slot)


def _dispatch(hp, d1, d2, P):
    T, Dh = hp.shape
    tm = _tile(T, 256)
    zeros = jnp.zeros((P, Dh), U32)
    return pl.pallas_call(
        _dispatch_body,
        grid=(T // tm,),
        in_specs=[
            pl.BlockSpec((tm,), lambda i: (i,), memory_space=pltpu.SMEM),
            pl.BlockSpec((tm,), lambda i: (i,), memory_space=pltpu.SMEM),
            pl.BlockSpec((tm, Dh), lambda i: (i, 0)),
            pl.BlockSpec(memory_space=pl.ANY),
        ],
        out_specs=pl.BlockSpec(memory_space=pl.ANY),
        out_shape=jax.ShapeDtypeStruct((P, Dh), U32),
        scratch_shapes=[pltpu.VMEM((2, tm, Dh), U32), pltpu.SemaphoreType.DMA((2,))],
        input_output_aliases={3: 0},
        compiler_params=_cp("arbitrary"),
        name="moe_dispatch",
    )(d1, d2, hp, zeros)


def _unpack_rows(words):
    lo = pltpu.bitcast(words << 16, F32)
    hi = pltpu.bitcast(words & jnp.uint32(0xFFFF0000), F32)
    return jnp.concatenate([lo, hi], axis=1).astype(BF16)


W_STREAM_CHUNK_ROWS = 256
W_STREAM_CHUNKS_PER_STEP = 2


def _expert_schedule(tile_table, n_tiles, n_pass):
    n_steps = n_pass * n_tiles
    s = jnp.arange(n_steps, dtype=I32)
    jj = s // n_tiles
    ii = jnp.minimum(s % n_tiles, tile_table[n_tiles] - 1)
    ee = jnp.take(tile_table, ii)
    key = jj * N_EXPERTS + ee
    first = jnp.concatenate([jnp.ones((1,), I32), (key[1:] != key[:-1]).astype(I32)])
    slot = (jnp.cumsum(first) - 1) % 2
    starts = jnp.where(first == 1, s, n_steps)
    nxt = jnp.concatenate([lax.cummin(starts, reverse=True)[1:], jnp.full((1,), n_steps, I32)])
    nxt_key = jnp.where(nxt < n_steps, jnp.take(key, jnp.minimum(nxt, n_steps - 1)), -1)
    nxt_e = jnp.where(nxt_key >= 0, nxt_key % N_EXPERTS, -1)
    nxt_j = jnp.maximum(nxt_key, 0) // N_EXPERTS
    return jnp.concatenate([first, slot, ee, jj, nxt_e, nxt_j]).astype(I32)


def _weight_stream_step(sched_ref, n_steps, step, w_hbms, layer, wbufs, stg, sems, cnt, col_block):
    ch = stg.shape[1]
    k_rows = wbufs[0].shape[1]
    nrb = k_rows // ch
    n_chunks = len(w_hbms) * nrb
    first = sched_ref[step]
    slot = sched_ref[n_steps + step]
    cur_e = sched_ref[2 * n_steps + step]
    cur_j = sched_ref[3 * n_steps + step]
    nxt_e = sched_ref[4 * n_steps + step]
    nxt_j = sched_ref[5 * n_steps + step]

    def chunk_copy(a, rb, e, j, c):
        src = w_hbms[a].at[layer, e, pl.ds(rb * ch, ch), pl.ds(j * col_block, col_block)]
        return pltpu.make_async_copy(src, stg.at[c % 2], sems.at[c % 2])

    def start(c, e, j):
        for a in range(len(w_hbms)):
            @pl.when(c // nrb == a)
            def _():
                chunk_copy(a, c - a * nrb, e, j, c).start()

    def finish(c, to_slot):
        chunk_copy(0, 0, 0, 0, c).wait()
        for a in range(len(w_hbms)):
            @pl.when(c // nrb == a)
            def _():
                r0 = pl.multiple_of((c - a * nrb) * ch, ch)
                wbufs[a][to_slot, pl.ds(r0, ch), :] = stg[c % 2].astype(BF16)

    @pl.when(step == 0)
    def _():
        cnt[0] = 0
        cnt[1] = 0

    @pl.when(first == 1)
    def _():
        started = cnt[1]

        def catch_up(c, carry):
            @pl.when(c >= started)
            def _():
                start(c, cur_e, cur_j)
            finish(c, slot)
            return carry

        lax.fori_loop(cnt[0], n_chunks, catch_up, 0)
        cnt[0] = 0
        cnt[1] = 0

    @pl.when(nxt_e >= 0)
    def _():
        done, started = cnt[0], cnt[1]

        def fin(c, carry):
            finish(c, 1 - slot)
            return carry

        def beg(c, carry):
            start(c, nxt_e, nxt_j)
            return carry

        lax.fori_loop(done, started, fin, 0)
        upto = jnp.minimum(started + W_STREAM_CHUNKS_PER_STEP, n_chunks)
        lax.fori_loop(started, upto, beg, 0)
        cnt[0] = started
        cnt[1] = upto

    return slot


def _expert_up_body(te_ref, sched_ref, hs_ref, wg_hbm, wu_hbm, a_ref, wg_buf, wu_buf, stg, sems, cnt,
                    *, n_tiles, layer):
    i = pl.program_id(1)
    step = pl.program_id(0) * n_tiles + i
    slot = _weight_stream_step(sched_ref, 2 * n_tiles, step, (wg_hbm, wu_hbm), layer, (wg_buf, wu_buf),
                               stg, sems, cnt, a_ref.shape[1])

    @pl.when(i < te_ref[n_tiles])
    def _():
        h = _unpack_rows(hs_ref[...])
        g = jnp.dot(h, wg_buf[slot], preferred_element_type=F32)
        u = jnp.dot(h, wu_buf[slot], preferred_element_type=F32)
        a_ref[...] = (_silu(g) * u).astype(BF16)

    @pl.when(i >= te_ref[n_tiles])
    def _():
        a_ref[...] = jnp.zeros_like(a_ref)


def _expert_up(tile_table, hs, wg, wu, layer, tme):
    P, Dh = hs.shape
    _, _, D, F = wg.shape
    tn = F // 2 if (F // 2) % LANE == 0 else F
    n_pass = F // tn
    n_tiles = P // tme
    ch = min(W_STREAM_CHUNK_ROWS, D)
    sched = _expert_schedule(tile_table, n_tiles, n_pass)
    live = lambda i, te: jnp.minimum(i, te[n_tiles] - 1)
    return pl.pallas_call(
        functools.partial(_expert_up_body, n_tiles=n_tiles, layer=layer),
        grid_spec=pltpu.PrefetchScalarGridSpec(
            num_scalar_prefetch=2,
            grid=(n_pass, n_tiles),
            in_specs=[
                pl.BlockSpec((tme, Dh), lambda j, i, te, sc: (live(i, te), 0)),
                pl.BlockSpec(memory_space=pl.ANY),
                pl.BlockSpec(memory_space=pl.ANY),
            ],
            out_specs=pl.BlockSpec((tme, tn), lambda j, i, te, sc: (i, j)),
            scratch_shapes=[pltpu.VMEM((2, D, tn), BF16), pltpu.VMEM((2, D, tn), BF16),
                            pltpu.VMEM((2, ch, tn), F32), pltpu.SemaphoreType.DMA((2,)),
                            pltpu.SMEM((2,), I32)],
        ),
        out_shape=jax.ShapeDtypeStruct((P, F), BF16),
        compiler_params=_cp("arbitrary", "arbitrary"),
        name="moe_expert_up",
    )(tile_table, sched, hs, wg, wu)


def _expert_down_body(te_ref, sched_ref, a_ref, w_hbm, y_ref, w_buf, stg, sems, cnt, *, n_tiles, layer):
    i = pl.program_id(0)
    slot = _weight_stream_step(sched_ref, n_tiles, i, (w_hbm,), layer, (w_buf,), stg, sems, cnt,
                               y_ref.shape[1])

    @pl.when(i < te_ref[n_tiles])
    def _():
        y_ref[...] = jnp.dot(a_ref[...], w_buf[slot], preferred_element_type=F32)

    @pl.when(i >= te_ref[n_tiles])
    def _():
        y_ref[...] = jnp.zeros_like(y_ref)


def _expert_down(tile_table, act, wd, layer, tme):
    P, F = act.shape
    D = wd.shape[3]
    n_tiles = P // tme
    ch = min(W_STREAM_CHUNK_ROWS, F)
    sched = _expert_schedule(tile_table, n_tiles, 1)
    return pl.pallas_call(
        functools.partial(_expert_down_body, n_tiles=n_tiles, layer=layer),
        grid_spec=pltpu.PrefetchScalarGridSpec(
            num_scalar_prefetch=2,
            grid=(n_tiles,),
            in_specs=[
                pl.BlockSpec((tme, F), lambda i, te, sc: (jnp.minimum(i, te[n_tiles] - 1), 0)),
                pl.BlockSpec(memory_space=pl.ANY),
            ],
            out_specs=pl.BlockSpec((tme, D), lambda i, te, sc: (i, 0)),
            scratch_shapes=[pltpu.VMEM((2, F, D), BF16), pltpu.VMEM((2, ch, D), F32),
                            pltpu.SemaphoreType.DMA((2,)), pltpu.SMEM((2,), I32)],
        ),
        out_shape=jax.ShapeDtypeStruct((P, D), F32),
        compiler_params=_cp("arbitrary"),
        name="moe_expert_down",
    )(tile_table, sched, act, wd)


def _combine_body(d1_ref, d2_ref, d1n_ref, d2n_ref, mf_ref, x_ref, fn_ref, y_ref, o_ref, ya, yb, sems,
                  *, final_norm):
    i = pl.program_id(0)
    n = x_ref.shape[0]
    slot = i % 2

    def fetch(ia_ref, ib_ref, to):
        def issue(blk, carry):
            for u in range(ROW_DMA_UNROLL):
                r = blk * ROW_DMA_UNROLL + u
                pltpu.make_async_copy(y_ref.at[pl.ds(ia_ref[r], 1)], ya.at[to, pl.ds(r, 1)],
                                      sems.at[to]).start(priority=0)
                pltpu.make_async_copy(y_ref.at[pl.ds(ib_ref[r], 1)], yb.at[to, pl.ds(r, 1)],
                                      sems.at[to]).start(priority=1)
            return carry
        lax.fori_loop(0, n // ROW_DMA_UNROLL, issue, 0)

    @pl.when(i == 0)
    def _():
        fetch(d1_ref, d2_ref, 0)

    @pl.when(i < pl.num_programs(0) - 1)
    def _():
        fetch(d1n_ref, d2n_ref, 1 - slot)

    def drain(blk, carry):
        row = pltpu.make_async_copy(y_ref.at[pl.ds(0, 1)], ya.at[slot, pl.ds(0, 1)], sems.at[slot])
        for _ in range(2 * ROW_DMA_UNROLL):
            row.wait()
        return carry

    lax.fori_loop(0, n // ROW_DMA_UNROLL, drain, 0)

    gates = mf_ref[...]
    eye = lax.broadcasted_iota(I32, (n, n), 0) == lax.broadcasted_iota(I32, (n, n), 1)
    g1 = jnp.sum(jnp.where(eye, gates[0:1, :], 0.0), axis=1, keepdims=True)
    g2 = jnp.sum(jnp.where(eye, gates[1:2, :], 0.0), axis=1, keepdims=True)
    out = x_ref[...] + (g1 * ya[slot] + g2 * yb[slot])
    if final_norm:
        out = _rms(out, fn_ref[...])
    o_ref[...] = out


def _combine(d1, d2, mf, x2, fnorm, y, final_norm):
    T, D = x2.shape
    tm = _tile(T, 256)
    last = T // tm - 1
    cur = lambda i: (i,)
    nxt = lambda i: (jnp.minimum(i + 1, last),)
    return pl.pallas_call(
        functools.partial(_combine_body, final_norm=final_norm),
        grid=(T // tm,),
        in_specs=[
            pl.BlockSpec((tm,), cur, memory_space=pltpu.SMEM),
            pl.BlockSpec((tm,), cur, memory_space=pltpu.SMEM),
            pl.BlockSpec((tm,), nxt, memory_space=pltpu.SMEM),
            pl.BlockSpec((tm,), nxt, memory_space=pltpu.SMEM),
            pl.BlockSpec((N_EXPERTS, tm), lambda i: (0, i)),
            pl.BlockSpec((tm, D), lambda i: (i, 0)),
            pl.BlockSpec((1, D), lambda i: (0, 0)),
            pl.BlockSpec(memory_space=pl.ANY),
        ],
        out_specs=pl.BlockSpec((tm, D), lambda i: (i, 0)),
        out_shape=jax.ShapeDtypeStruct((T, D), F32),
        scratch_shapes=[pltpu.VMEM((2, tm, D), F32), pltpu.VMEM((2, tm, D), F32),
                        pltpu.SemaphoreType.DMA((2,))],
        compiler_params=_cp("arbitrary"),
        name="moe_combine",
    )(d1, d2, d1, d2, mf, x2, fnorm, y)


def _pad_rows16(w_t):
    pad = jnp.zeros((BF16_SUBLANE_TILE - w_t.shape[0], w_t.shape[1]), F32)
    w = jnp.concatenate([w_t, pad], axis=0)
    hi = w.astype(BF16)
    lo = (w - hi.astype(F32)).astype(BF16)
    return hi, lo


def _attention_layer(x2, B, S, gnorm, w_in_all, layer, b_forget, w_out):
    sb0 = 3 * HEADS_W + N_HEADS
    col = lambda part: w_in_all[layer, :, part * HEADS_W:(part + 1) * HEADS_W]
    sbc = lambda part: w_in_all[layer, :, sb0 + part * HEADS_W:sb0 + (part + 1) * HEADS_W]
    w_k = jnp.concatenate([col(1), sbc(1)], axis=1).astype(BF16)
    w_qvt = jnp.concatenate([col(0), sbc(0), col(2), sbc(2)], axis=1).T.astype(BF16)
    wf_hi, wf_lo = _pad_rows16(w_in_all[layer, :, 3 * HEADS_W:sb0].T)
    keys, qvt, f16 = _att_in_proj(x2, gnorm, w_k, w_qvt, wf_hi, wf_lo)
    cb = _forget_cumsum(f16, b_forget.reshape(N_HEADS, 1), B, S)
    oa = _fox_attention(keys, qvt, cb, B, S)
    ob = _sb_attention(keys, qvt, B, S)
    return _att_out_proj(oa, ob, w_out, layer, x2)


def _dense_ffn_layer(x2, gnorm, wg, wu, wd, layer):
    a = _ffn_up(x2, gnorm, wg, wu, layer)
    return _ffn_down(a, wd, layer, x2)


def _conv_layer(x2, S, gnorm, w_in, conv_w, w_out, layer):
    gb, gg = _conv_in(x2, gnorm, w_in, layer)
    return _conv_out(gb, gg, conv_w, w_out, layer, x2, S)


def _moe_layer(x2, gnorm, w_router, wg, wu, wd, layer, fnorm, final_norm):
    T, D = x2.shape
    tme = 512 if T >= 4096 else 128
    n_tiles = (2 * T) // tme + N_EXPERTS
    P = n_tiles * tme

    wr_hi, wr_lo = _pad_rows16(w_router.T)
    hp, mi, mf, cnt = _router(x2, gnorm, wr_hi, wr_lo)

    counts = cnt[:, 0].astype(I32)
    padded = ((counts + tme - 1) // tme) * tme
    ends = jnp.cumsum(padded)
    offs = ends - padded
    d1 = jnp.take(offs, mi[0]) + mi[2]
    d2 = jnp.take(offs, mi[1]) + mi[3]
    tile_start = jnp.arange(n_tiles, dtype=I32) * tme
    tile_expert = jnp.sum((tile_start[:, None] >= ends[None, :]).astype(I32), axis=1)
    tile_expert = jnp.minimum(tile_expert, N_EXPERTS - 1)
    tile_table = jnp.concatenate([tile_expert, (ends[-1:] // tme).astype(I32)])

    hs = _dispatch(hp, d1, d2, P)
    act = _expert_up(tile_table, hs, wg, wu, layer, tme)
    y = _expert_down(tile_table, act, wd, layer, tme)
    return _combine(d1, d2, mf, x2, fnorm, y, final_norm)


def kernel(x, mix_norm, ffn_norm, final_norm, w_in_att, b_forget, w_out_att, w_in_conv, conv_w,
           w_out_conv, w_gate_dense, w_up_dense, w_down_dense, w_router, w_gate_moe, w_up_moe,
           w_down_moe):
    B, S, D = x.shape
    depth = mix_norm.shape[0]
    assert depth % 2 == 0, "the final rmsnorm is fused into the last (routed) layer"
    x2 = x.reshape(B * S, D)
    fnorm = final_norm.reshape(1, D)
    w_out_att, w_in_conv, w_out_conv, w_gate_dense, w_up_dense, w_down_dense = (
        w.astype(BF16) for w in (w_out_att, w_in_conv, w_out_conv, w_gate_dense, w_up_dense, w_down_dense))
    for i in range(depth):
        j = i // 2
        mg = mix_norm[i].reshape(1, D)
        fg = ffn_norm[i].reshape(1, D)
        if i % 2 == 0:
            x2 = _attention_layer(x2, B, S, mg, w_in_att, j, b_forget[j], w_out_att)
            x2 = _dense_ffn_layer(x2, fg, w_gate_dense, w_up_dense, w_down_dense, j)
        else:
            x2 = _conv_layer(x2, S, mg, w_in_conv, conv_w[j], w_out_conv, j)
            x2 = _moe_layer(x2, fg, w_router[j], w_gate_moe, w_up_moe, w_down_moe, j,
                            fnorm, final_norm=(i == depth - 1))
    return x2.reshape(B, S, D)
```

```python
import functools
import math

import jax
import jax.numpy as jnp
from jax import lax
from jax.experimental import pallas as pl
from jax.experimental.pallas import tpu as pltpu

F32 = jnp.float32
BF16 = jnp.bfloat16
I32 = jnp.int32
U32 = jnp.uint32

HEAD_DIM = 128
N_HEADS = 8
HEADS_W = N_HEADS * HEAD_DIM
N_EXPERTS = 8
RMS_EPS = 1e-6
CONV_WIDTH = 3
LOG2E = math.log2(math.e)
QK_SCALE_LOG2 = LOG2E / math.sqrt(HEAD_DIM)

V7X_VMEM_LIMIT_BYTES = 56 * 1024 * 1024
LANE = 128
BF16_SUBLANE_TILE = 16
NEG_INF = float("-inf")

_NT = (((1,), (1,)), ((), ()))


def _cp(*sem):
    return pltpu.CompilerParams(dimension_semantics=sem, vmem_limit_bytes=V7X_VMEM_LIMIT_BYTES)


def _tile(n, pref, unit=LANE):
    if n <= pref:
        return n
    t = (pref // unit) * unit
    while t > unit and n % t:
        t -= unit
    assert n % t == 0, (n, pref)
    return t


def _rms(x, g):
    ms = jnp.mean(x * x, axis=-1, keepdims=True)
    return x * lax.rsqrt(ms + RMS_EPS) * g


def _split_bf16(v):
    hi = v.astype(BF16)
    lo = (v - hi.astype(F32)).astype(BF16)
    return hi, lo


def _log_sigmoid_pair(z):
    sp = jnp.log1p(jnp.exp(-jnp.abs(z)))
    return jnp.minimum(z, 0.0) - sp, -jnp.maximum(z, 0.0) - sp


def _silu(g):
    return g / (1.0 + jnp.exp(-g))


def _skinny_nt(wh_ref, wl_ref, hb, hl):
    wh = wh_ref[...]
    out = lax.dot_general(wh, hb, _NT, preferred_element_type=F32)
    out += lax.dot_general(wh, hl, _NT, preferred_element_type=F32)
    out += lax.dot_general(wl_ref[...], hb, _NT, preferred_element_type=F32)
    return out


def _att_in_body(x_ref, g_ref, w_ref, wv_ref, wfh_ref, wfl_ref, o_ref, vt_ref, f_ref, h_scr, *, q_blocks, n_main):
    j = pl.program_id(1)

    @pl.when(j == 0)
    def _():
        h = _rms(x_ref[...], g_ref[...])
        hb, hl = _split_bf16(h)
        h_scr[...] = hb
        f_ref[...] = _skinny_nt(wfh_ref, wfl_ref, hb, hl)

    @pl.when(j < n_main)
    def _():
        o_ref[...] = jnp.dot(h_scr[...], w_ref[...], preferred_element_type=F32).astype(BF16)

    @pl.when(j >= n_main)
    def _():
        acc = lax.dot_general(wv_ref[...], h_scr[...], _NT, preferred_element_type=F32)
        is_q = (j - n_main) < 2 * q_blocks
        vt_ref[...] = (acc * jnp.where(is_q, QK_SCALE_LOG2, 1.0)).astype(BF16)


def _att_in_proj(x2, gnorm, w_k, w_qvt, wf_hi, wf_lo):
    T, D = x2.shape
    N = w_k.shape[1]
    NV = w_qvt.shape[0]
    tm = _tile(T, 1024)
    tn = _tile(HEADS_W, 1024)
    n_main = N // tn
    return pl.pallas_call(
        functools.partial(_att_in_body, q_blocks=HEADS_W // tn, n_main=n_main),
        grid=(T // tm, n_main + NV // tn),
        in_specs=[
            pl.BlockSpec((tm, D), lambda i, j: (i, 0)),
            pl.BlockSpec((1, D), lambda i, j: (0, 0)),
            pl.BlockSpec((D, tn), lambda i, j: (0, jnp.minimum(j, n_main - 1))),
            pl.BlockSpec((tn, D), lambda i, j: (jnp.maximum(j - n_main, 0), 0)),
            pl.BlockSpec((BF16_SUBLANE_TILE, D), lambda i, j: (0, 0)),
            pl.BlockSpec((BF16_SUBLANE_TILE, D), lambda i, j: (0, 0)),
        ],
        out_specs=[
            pl.BlockSpec((tm, tn), lambda i, j: (i, jnp.minimum(j, n_main - 1))),
            pl.BlockSpec((tn, tm), lambda i, j: (jnp.maximum(j - n_main, 0), i)),
            pl.BlockSpec((BF16_SUBLANE_TILE, tm), lambda i, j: (0, i)),
        ],
        out_shape=[
            jax.ShapeDtypeStruct((T, N), BF16),
            jax.ShapeDtypeStruct((NV, T), BF16),
            jax.ShapeDtypeStruct((BF16_SUBLANE_TILE, T), F32),
        ],
        scratch_shapes=[pltpu.VMEM((tm, D), BF16)],
        compiler_params=_cp("parallel", "arbitrary"),
        name="att_in_proj",
    )(x2, gnorm, w_k, w_qvt, wf_hi, wf_lo)


def _forget_cumsum_body(f_ref, b_ref, cb_ref):
    z = f_ref[...] + b_ref[...]
    lf, _ = _log_sigmoid_pair(z)
    S = lf.shape[1]
    lane = lax.broadcasted_iota(I32, lf.shape, 1)
    c = lf
    sh = 1
    while sh < S:
        c = c + jnp.where(lane >= sh, pltpu.roll(c, sh, axis=1), 0.0)
        sh *= 2
    c2 = c * LOG2E
    hi = c2.astype(BF16).astype(F32)
    r1 = c2 - hi
    mid = r1.astype(BF16).astype(F32)
    lo = (r1 - mid).astype(BF16).astype(F32)
    pad = jnp.zeros((LANE - 3 * N_HEADS, S), F32)
    cb_ref[...] = jnp.concatenate([hi, mid, lo, pad], axis=0).T.astype(BF16)


def _forget_cumsum(f16, b_col, B, S):
    return pl.pallas_call(
        _forget_cumsum_body,
        grid=(B,),
        in_specs=[
            pl.BlockSpec((N_HEADS, S), lambda b: (0, b)),
            pl.BlockSpec((N_HEADS, 1), lambda b: (0, 0)),
        ],
        out_specs=pl.BlockSpec((S, LANE), lambda b: (b, 0)),
        out_shape=jax.ShapeDtypeStruct((B * S, LANE), BF16),
        compiler_params=_cp("parallel"),
        name="forget_cumsum",
    )(f16, b_col)


def _head_cols(g):
    return slice(g * HEAD_DIM, (g + 1) * HEAD_DIM)


def _fox_body(qt_ref, k_ref, vt_ref, cb_ref, o_ref, sc_a, sc_b, m_scr, l_scr, acc_scr, *, tq, heads):
    hg = pl.program_id(1)
    i = pl.program_id(2)
    sub = lax.broadcasted_iota(I32, (LANE, tq), 0)
    qs = []
    for g in range(heads):
        h = hg * heads + g
        pick = (sub == h) | (sub == N_HEADS + h) | (sub == 2 * N_HEADS + h)
        qs.append(jnp.concatenate([qt_ref[_head_cols(g), :], jnp.where(pick, -1.0, 0.0).astype(BF16)], axis=0))
    key = lax.broadcasted_iota(I32, (tq, tq), 0)
    qry = lax.broadcasted_iota(I32, (tq, tq), 1)

    def scores(kt, sc_scr):
        ks = pl.multiple_of(kt * tq, tq)
        cb = cb_ref[pl.ds(ks, tq), :]
        for g in range(heads):
            k_aug = jnp.concatenate([k_ref[pl.ds(ks, tq), _head_cols(g)], cb], axis=1)
            sc_scr[g] = jnp.dot(k_aug, qs[g], preferred_element_type=F32)

    def finish(kt, sc_scr, masked):
        ks = pl.multiple_of(kt * tq, tq)
        probs = []
        for g in range(heads):
            m = m_scr[g]
            s = sc_scr[g]
            if masked:
                s = jnp.where(key <= qry, s, NEG_INF)
            m_new = jnp.maximum(m, jnp.max(s, axis=0, keepdims=True))
            alpha = jnp.exp2(m - m_new)
            p = jnp.exp2(s - m_new)
            m_scr[g] = m_new
            l_scr[g] = alpha * l_scr[g] + jnp.sum(p, axis=0, keepdims=True)
            probs.append((alpha, p.astype(BF16)))
        for g in range(heads):
            alpha, p = probs[g]
            vt = vt_ref[_head_cols(g), pl.ds(ks, tq)]
            acc_scr[g] = alpha * acc_scr[g] + jnp.dot(vt, p, preferred_element_type=F32)

    def pair(k2, carry):
        scores(2 * k2 + 1, sc_b)
        finish(2 * k2, sc_a, False)
        scores(2 * k2 + 2, sc_a)
        finish(2 * k2 + 1, sc_b, False)
        return carry

    m_scr[...] = jnp.full(m_scr.shape, NEG_INF, F32)
    l_scr[...] = jnp.zeros(l_scr.shape, F32)
    acc_scr[...] = jnp.zeros(acc_scr.shape, F32)
    scores(0, sc_a)
    lax.fori_loop(0, i // 2, pair, 0)

    @pl.when(i % 2 == 0)
    def _():
        finish(i, sc_a, True)

    @pl.when(i % 2 == 1)
    def _():
        scores(i, sc_b)
        finish(i - 1, sc_a, False)
        finish(i, sc_b, True)

    for g in range(heads):
        o_ref[:, _head_cols(g)] = (acc_scr[g] / l_scr[g]).T.astype(BF16)


ATT_TILE = 256
ATT_HEADS_PER_STEP = 4


def _fox_attention(keys, qvt, cb, B, S):
    T = B * S
    tq = _tile(S, ATT_TILE)
    nq = S // tq
    hp = ATT_HEADS_PER_STEP
    ng = N_HEADS // hp
    w = hp * HEAD_DIM
    return pl.pallas_call(
        functools.partial(_fox_body, tq=tq, heads=hp),
        grid=(B, ng, nq),
        in_specs=[
            pl.BlockSpec((w, tq), lambda b, h, i: (h, b * nq + i)),
            pl.BlockSpec((S, w), lambda b, h, i: (b, h)),
            pl.BlockSpec((w, S), lambda b, h, i: (2 * ng + h, b)),
            pl.BlockSpec((S, LANE), lambda b, h, i: (b, 0)),
        ],
        out_specs=pl.BlockSpec((tq, w), lambda b, h, i: (b * nq + i, h)),
        out_shape=jax.ShapeDtypeStruct((T, HEADS_W), BF16),
        scratch_shapes=[pltpu.VMEM((hp, tq, tq), F32), pltpu.VMEM((hp, tq, tq), F32),
                        pltpu.VMEM((hp, 1, tq), F32), pltpu.VMEM((hp, 1, tq), F32),
                        pltpu.VMEM((hp, HEAD_DIM, tq), F32)],
        compiler_params=_cp("parallel", "parallel", "arbitrary"),
        name="fox_attention",
    )(qvt, keys, qvt, cb)


F32_EXP2_UNDERFLOW = -150.0


def _sb_body(qt_ref, k_ref, vt_ref, ut_ref, o_ref, r_scr, acc_scr, *, tq, heads):
    i = pl.program_id(2)
    qs = [qt_ref[_head_cols(g), :] for g in range(heads)]
    ut = ut_ref[...]
    key = lax.broadcasted_iota(I32, (tq, tq), 0)
    qry = lax.broadcasted_iota(I32, (tq, tq), 1)
    strict = key < qry

    def tile(kt, masked):
        ks = pl.multiple_of(kt * tq, tq)
        zs = [jnp.dot(k_ref[pl.ds(ks, tq), _head_cols(g)], qs[g], preferred_element_type=F32)
              for g in range(heads)]
        mid = []
        for g in range(heads):
            z = zs[g]
            log_beta = jnp.minimum(z, 0.0) - jnp.log2(1.0 + jnp.exp2(-jnp.abs(z)))
            log_om = log_beta - z
            if masked:
                log_om = jnp.where(strict, log_om, 0.0)
            hi, lo = _split_bf16(log_om)
            e = jnp.dot(ut, jnp.concatenate([hi, lo], axis=0), preferred_element_type=F32)
            mid.append((log_beta, log_om, e))
        for g in range(heads):
            log_beta, log_om, e = mid[g]
            r_sum = r_scr[g]
            a = jnp.exp2(log_beta + e + r_sum)
            if masked:
                a = jnp.where(strict, a, 0.0)
            vt = vt_ref[_head_cols(g), pl.ds(ks, tq)]
            acc_scr[g] += jnp.dot(vt, a.astype(BF16), preferred_element_type=F32)
            r_scr[g] = r_sum + jnp.sum(log_om, axis=0, keepdims=True)

    def live():
        return (jnp.max(r_scr[...]) > F32_EXP2_UNDERFLOW).astype(I32)

    r_scr[...] = jnp.zeros(r_scr.shape, F32)
    acc_scr[...] = jnp.zeros(acc_scr.shape, F32)
    tile(i, True)

    def step(state):
        n, _ = state
        tile(i - 1 - n, False)
        return n + 1, live()

    lax.while_loop(lambda st: (st[0] < i) & (st[1] > 0), step, (jnp.int32(0), live()))
    for g in range(heads):
        o_ref[:, _head_cols(g)] = acc_scr[g].T.astype(BF16)


def _sb_attention(keys, qvt, B, S):
    T = B * S
    tq = _tile(S, ATT_TILE)
    nq = S // tq
    hp = ATT_HEADS_PER_STEP
    ng = N_HEADS // hp
    w = hp * HEAD_DIM
    r = lax.broadcasted_iota(I32, (tq, tq), 0)
    c = lax.broadcasted_iota(I32, (tq, tq), 1)
    ut = (c > r).astype(BF16)
    ut = jnp.concatenate([ut, ut], axis=1)
    return pl.pallas_call(
        functools.partial(_sb_body, tq=tq, heads=hp),
        grid=(B, ng, nq),
        in_specs=[
            pl.BlockSpec((w, tq), lambda b, h, i: (ng + h, b * nq + i)),
            pl.BlockSpec((S, w), lambda b, h, i: (b, ng + h)),
            pl.BlockSpec((w, S), lambda b, h, i: (3 * ng + h, b)),
            pl.BlockSpec((tq, 2 * tq), lambda b, h, i: (0, 0)),
        ],
        out_specs=pl.BlockSpec((tq, w), lambda b, h, i: (b * nq + i, h)),
        out_shape=jax.ShapeDtypeStruct((T, HEADS_W), BF16),
        scratch_shapes=[pltpu.VMEM((hp, 1, tq), F32), pltpu.VMEM((hp, HEAD_DIM, tq), F32)],
        compiler_params=_cp("parallel", "parallel", "arbitrary"),
        name="sb_attention",
    )(qvt, keys, qvt, ut)


def _att_out_body(oa_ref, ob_ref, wa_ref, wb_ref, x_ref, o_ref):
    acc = jnp.dot(oa_ref[...], wa_ref[...], preferred_element_type=F32)
    acc += jnp.dot(ob_ref[...], wb_ref[...], preferred_element_type=F32)
    o_ref[...] = x_ref[...] + acc


def _att_out_proj(oa, ob, w_out, layer, x2):
    T, D = x2.shape
    tm = _tile(T, 512)
    return pl.pallas_call(
        _att_out_body,
        grid=(T // tm,),
        in_specs=[
            pl.BlockSpec((tm, HEADS_W), lambda i: (i, 0)),
            pl.BlockSpec((tm, HEADS_W), lambda i: (i, 0)),
            pl.BlockSpec((None, HEADS_W, D), lambda i: (layer, 0, 0)),
            pl.BlockSpec((None, HEADS_W, D), lambda i: (layer, 1, 0)),
            pl.BlockSpec((tm, D), lambda i: (i, 0)),
        ],
        out_specs=pl.BlockSpec((tm, D), lambda i: (i, 0)),
        out_shape=jax.ShapeDtypeStruct((T, D), F32),
        compiler_params=_cp("parallel"),
        name="att_out_proj",
    )(oa, ob, w_out, w_out, x2)


def _ffn_up_body(x_ref, g_ref, wg_ref, wu_ref, a_ref, h_scr):
    @pl.when(pl.program_id(1) == 0)
    def _():
        h_scr[...] = _rms(x_ref[...], g_ref[...]).astype(BF16)

    h = h_scr[...]
    g = jnp.dot(h, wg_ref[...], preferred_element_type=F32)
    u = jnp.dot(h, wu_ref[...], preferred_element_type=F32)
    a_ref[...] = (_silu(g) * u).astype(BF16)


def _ffn_up(x2, gnorm, wg, wu, layer):
    T, D = x2.shape
    F = wg.shape[2]
    tm = _tile(T, 1024)
    tn = _tile(F, 512)
    return pl.pallas_call(
        _ffn_up_body,
        grid=(T // tm, F // tn),
        in_specs=[
            pl.BlockSpec((tm, D), lambda i, j: (i, 0)),
            pl.BlockSpec((1, D), lambda i, j: (0, 0)),
            pl.BlockSpec((None, D, tn), lambda i, j: (layer, 0, j)),
            pl.BlockSpec((None, D, tn), lambda i, j: (layer, 0, j)),
        ],
        out_specs=pl.BlockSpec((tm, tn), lambda i, j: (i, j)),
        out_shape=jax.ShapeDtypeStruct((T, F), BF16),
        scratch_shapes=[pltpu.VMEM((tm, D), BF16)],
        compiler_params=_cp("parallel", "arbitrary"),
        name="ffn_up",
    )(x2, gnorm, wg, wu)


def _ffn_down_body(a_ref, w_ref, x_ref, o_ref):
    o_ref[...] = x_ref[...] + jnp.dot(a_ref[...], w_ref[...], preferred_element_type=F32)


def _ffn_down(a, wd, layer, x2):
    T, D = x2.shape
    F = a.shape[1]
    tm = _tile(T, 1024)
    tn = _tile(D, 512)
    return pl.pallas_call(
        _ffn_down_body,
        grid=(T // tm, D // tn),
        in_specs=[
            pl.BlockSpec((tm, F), lambda i, j: (i, 0)),
            pl.BlockSpec((None, F, tn), lambda i, j: (layer, 0, j)),
            pl.BlockSpec((tm, tn), lambda i, j: (i, j)),
        ],
        out_specs=pl.BlockSpec((tm, tn), lambda i, j: (i, j)),
        out_shape=jax.ShapeDtypeStruct((T, D), F32),
        compiler_params=_cp("parallel", "parallel"),
        name="ffn_down",
    )(a, wd, x2)


def _conv_in_body(x_ref, g_ref, wb_ref, wc_ref, wu_ref, gb_ref, gg_ref, h_scr):
    @pl.when(pl.program_id(1) == 0)
    def _():
        h_scr[...] = _rms(x_ref[...], g_ref[...]).astype(BF16)

    h = h_scr[...]
    gb_ref[...] = jnp.dot(h, wb_ref[...], preferred_element_type=F32).astype(BF16)
    c = jnp.dot(h, wc_ref[...], preferred_element_type=F32)
    u = jnp.dot(h, wu_ref[...], preferred_element_type=F32)
    gg_ref[...] = (c * u).astype(BF16)


def _conv_in(x2, gnorm, w_in, layer):
    T, D = x2.shape
    tm = _tile(T, 1024)
    tn = _tile(D, 512)
    nd = D // tn
    return pl.pallas_call(
        _conv_in_body,
        grid=(T // tm, nd),
        in_specs=[
            pl.BlockSpec((tm, D), lambda i, j: (i, 0)),
            pl.BlockSpec((1, D), lambda i, j: (0, 0)),
            pl.BlockSpec((None, D, tn), lambda i, j: (layer, 0, j)),
            pl.BlockSpec((None, D, tn), lambda i, j: (layer, 0, nd + j)),
            pl.BlockSpec((None, D, tn), lambda i, j: (layer, 0, 2 * nd + j)),
        ],
        out_specs=[
            pl.BlockSpec((tm, tn), lambda i, j: (i, j)),
            pl.BlockSpec((tm, tn), lambda i, j: (i, j)),
        ],
        out_shape=[jax.ShapeDtypeStruct((T, D), BF16), jax.ShapeDtypeStruct((T, D), BF16)],
        scratch_shapes=[pltpu.VMEM((tm, D), BF16)],
        compiler_params=_cp("parallel", "arbitrary"),
        name="conv_in",
    )(x2, gnorm, w_in, w_in, w_in)


def _conv_out_body(gb_ref, g_ref, gp_ref, cw_ref, w_ref, x_ref, o_ref, *, tiles_per_seq):
    i = pl.program_id(0)
    g = g_ref[...].astype(F32)
    tm = g.shape[0]
    keep = jnp.where(i % tiles_per_seq == 0, 0.0, 1.0)
    prev = gp_ref[...].astype(F32) * keep
    p1 = prev[BF16_SUBLANE_TILE - 1:BF16_SUBLANE_TILE, :]
    p2 = prev[BF16_SUBLANE_TILE - 2:BF16_SUBLANE_TILE - 1, :]
    row = lax.broadcasted_iota(I32, g.shape, 0)
    g1 = jnp.where(row == 0, p1, pltpu.roll(g, 1, axis=0))
    g2 = jnp.where(row == 0, p2, jnp.where(row == 1, p1, pltpu.roll(g, 2, axis=0)))
    cw = cw_ref[...]
    conv = g2 * cw[0:1, :] + g1 * cw[1:2, :] + g * cw[2:3, :]
    y = (gb_ref[...].astype(F32) * conv).astype(BF16)
    o_ref[...] = x_ref[...] + jnp.dot(y, w_ref[...], preferred_element_type=F32)


def _conv_out(gb, gg, conv_w, w_out, layer, x2, S):
    T, D = x2.shape
    tm = _tile(S, 256)
    pt = BF16_SUBLANE_TILE
    return pl.pallas_call(
        functools.partial(_conv_out_body, tiles_per_seq=S // tm),
        grid=(T // tm,),
        in_specs=[
            pl.BlockSpec((tm, D), lambda i: (i, 0)),
            pl.BlockSpec((tm, D), lambda i: (i, 0)),
            pl.BlockSpec((pt, D), lambda i: (jnp.maximum(i * (tm // pt) - 1, 0), 0)),
            pl.BlockSpec((CONV_WIDTH, D), lambda i: (0, 0)),
            pl.BlockSpec((None, D, D), lambda i: (layer, 0, 0)),
            pl.BlockSpec((tm, D), lambda i: (i, 0)),
        ],
        out_specs=pl.BlockSpec((tm, D), lambda i: (i, 0)),
        out_shape=jax.ShapeDtypeStruct((T, D), F32),
        compiler_params=_cp("parallel"),
        name="conv_out",
    )(gb, gg, gg, conv_w, w_out, x2)


def _router_body(x_ref, g_ref, wrh_ref, wrl_ref, tri_ref, hp_ref, mi_ref, mf_ref, cnt_ref, carry):
    @pl.when(pl.program_id(0) == 0)
    def _():
        carry[...] = jnp.zeros_like(carry)

    h = _rms(x_ref[...], g_ref[...])
    hb, hl = _split_bf16(h)
    bits = pltpu.bitcast(hb.astype(F32), U32)
    half = bits.shape[1] // 2
    hp_ref[...] = (bits[:, :half] >> 16) | bits[:, half:]

    logits = _skinny_nt(wrh_ref, wrl_ref, hb, hl)[:N_EXPERTS]
    eidx = lax.broadcasted_iota(I32, logits.shape, 0).astype(F32)
    ne = float(N_EXPERTS)
    m1 = jnp.max(logits, axis=0, keepdims=True)
    i1 = jnp.min(jnp.where(logits == m1, eidx, ne), axis=0, keepdims=True)
    rest = jnp.where(eidx == i1, NEG_INF, logits)
    m2 = jnp.max(rest, axis=0, keepdims=True)
    i2 = jnp.min(jnp.where(rest == m2, eidx, ne), axis=0, keepdims=True)
    e21 = jnp.exp(m2 - m1)
    g1 = 1.0 / (1.0 + e21)
    g2 = e21 * g1

    sel = jnp.where((eidx == i1) | (eidx == i2), 1.0, 0.0)
    incl = jnp.dot(sel, tri_ref[...], preferred_element_type=F32)
    pos = carry[...] + incl - sel
    carry[...] = carry[...] + jnp.sum(sel, axis=1, keepdims=True)
    p1 = jnp.sum(jnp.where(eidx == i1, pos, 0.0), axis=0, keepdims=True)
    p2 = jnp.sum(jnp.where(eidx == i2, pos, 0.0), axis=0, keepdims=True)
    meta = jnp.where(eidx == 0, i1, jnp.where(eidx == 1, i2, jnp.where(eidx == 2, p1, jnp.where(eidx == 3, p2, 0.0))))
    mi_ref[...] = meta.astype(I32)
    mf_ref[...] = jnp.where(eidx == 0, g1, jnp.where(eidx == 1, g2, 0.0))
    cnt_ref[...] = jnp.broadcast_to(carry[...], cnt_ref.shape)


def _router(x2, gnorm, wr_hi, wr_lo):
    T, D = x2.shape
    tm = _tile(T, 512)
    r = lax.broadcasted_iota(I32, (tm, tm), 0)
    c = lax.broadcasted_iota(I32, (tm, tm), 1)
    tri = (r <= c).astype(F32)
    return pl.pallas_call(
        _router_body,
        grid=(T // tm,),
        in_specs=[
            pl.BlockSpec((tm, D), lambda i: (i, 0)),
            pl.BlockSpec((1, D), lambda i: (0, 0)),
            pl.BlockSpec((BF16_SUBLANE_TILE, D), lambda i: (0, 0)),
            pl.BlockSpec((BF16_SUBLANE_TILE, D), lambda i: (0, 0)),
            pl.BlockSpec((tm, tm), lambda i: (0, 0)),
        ],
        out_specs=[
            pl.BlockSpec((tm, D // 2), lambda i: (i, 0)),
            pl.BlockSpec((N_EXPERTS, tm), lambda i: (0, i)),
            pl.BlockSpec((N_EXPERTS, tm), lambda i: (0, i)),
            pl.BlockSpec((N_EXPERTS, LANE), lambda i: (0, 0)),
        ],
        out_shape=[
            jax.ShapeDtypeStruct((T, D // 2), U32),
            jax.ShapeDtypeStruct((N_EXPERTS, T), I32),
            jax.ShapeDtypeStruct((N_EXPERTS, T), F32),
            jax.ShapeDtypeStruct((N_EXPERTS, LANE), F32),
        ],
        scratch_shapes=[pltpu.VMEM((N_EXPERTS, 1), F32)],
        compiler_params=_cp("arbitrary"),
        name="moe_router",
    )(x2, gnorm, wr_hi, wr_lo, tri)


ROW_DMA_UNROLL = 8


def _dispatch_body(d1_ref, d2_ref, pad_ref, hp_ref, hs_ref, stage, ztile, sems, zsem, *, tme):
    i = pl.program_id(0)
    n = hp_ref.shape[0]
    slot = i % 2
    stage[slot] = hp_ref[...]

    def issue(blk, carry):
        for u in range(ROW_DMA_UNROLL):
            r = blk * ROW_DMA_UNROLL + u
            src = stage.at[slot, pl.ds(r, 1)]
            pltpu.make_async_copy(src, hs_ref.at[pl.ds(d1_ref[r], 1)], sems.at[slot]).start(priority=0)
            pltpu.make_async_copy(src, hs_ref.at[pl.ds(d2_ref[r], 1)], sems.at[slot]).start(priority=1)
        return carry

    def drain(which):
        def body(blk, carry):
            row = pltpu.make_async_copy(stage.at[which, pl.ds(0, 1)], hs_ref.at[pl.ds(0, 1)], sems.at[which])
            for _ in range(2 * ROW_DMA_UNROLL):
                row.wait()
            return carry
        lax.fori_loop(0, n // ROW_DMA_UNROLL, body, 0)

    lax.fori_loop(0, n // ROW_DMA_UNROLL, issue, 0)

    @pl.when(i > 0)
    def _():
        drain(1 - slot)

    @pl.when(i == pl.num_programs(0) - 1)
    def _():
        drain(slot)
        ztile[...] = jnp.zeros_like(ztile)
        zrow = pltpu.make_async_copy(ztile.at[pl.ds(0, 1)], hs_ref.at[pl.ds(0, 1)], zsem)
        for e in range(N_EXPERTS):
            first, count = pad_ref[e], pad_ref[N_EXPERTS + e]

            def fill(r, carry):
                pltpu.make_async_copy(ztile.at[pl.ds(0, 1)], hs_ref.at[pl.ds(first + r, 1)], zsem).start()
                return carry

            def fill_done(r, carry):
                zrow.wait()
                return carry

            lax.fori_loop(0, count, fill, 0)
            lax.fori_loop(0, count, fill_done, 0)

        def clear_tile(t, carry):
            whole = pltpu.make_async_copy(ztile, hs_ref.at[pl.ds(pl.multiple_of(t * tme, tme), tme)], zsem)
            whole.start()
            whole.wait()
            return carry

        lax.fori_loop(pad_ref[2 * N_EXPERTS], hs_ref.shape[0] // tme, clear_tile, 0)


def _dispatch(hp, d1, d2, pad_info, P, tme):
    T, Dh = hp.shape
    tm = _tile(T, 256)
    return pl.pallas_call(
        functools.partial(_dispatch_body, tme=tme),
        grid=(T // tm,),
        in_specs=[
            pl.BlockSpec((tm,), lambda i: (i,), memory_space=pltpu.SMEM),
            pl.BlockSpec((tm,), lambda i: (i,), memory_space=pltpu.SMEM),
            pl.BlockSpec(memory_space=pltpu.SMEM),
            pl.BlockSpec((tm, Dh), lambda i: (i, 0)),
        ],
        out_specs=pl.BlockSpec(memory_space=pl.ANY),
        out_shape=jax.ShapeDtypeStruct((P, Dh), U32),
        scratch_shapes=[pltpu.VMEM((2, tm, Dh), U32), pltpu.VMEM((tme, Dh), U32),
                        pltpu.SemaphoreType.DMA((2,)), pltpu.SemaphoreType.DMA(())],
        compiler_params=_cp("arbitrary"),
        name="moe_dispatch",
    )(d1, d2, pad_info, hp)


def _unpack_rows(words):
    lo = pltpu.bitcast(words << 16, F32)
    hi = pltpu.bitcast(words & jnp.uint32(0xFFFF0000), F32)
    return jnp.concatenate([lo, hi], axis=1).astype(BF16)


W_STREAM_CHUNK_ROWS = 256
W_STREAM_CHUNKS_PER_STEP = 2


def _expert_schedule(tile_table, n_tiles, n_pass):
    n_steps = n_pass * n_tiles
    s = jnp.arange(n_steps, dtype=I32)
    jj = s // n_tiles
    ii = jnp.minimum(s % n_tiles, tile_table[n_tiles] - 1)
    ee = jnp.take(tile_table, ii)
    key = jj * N_EXPERTS + ee
    first = jnp.concatenate([jnp.ones((1,), I32), (key[1:] != key[:-1]).astype(I32)])
    slot = (jnp.cumsum(first) - 1) % 2
    starts = jnp.where(first == 1, s, n_steps)
    nxt = jnp.concatenate([lax.cummin(starts, reverse=True)[1:], jnp.full((1,), n_steps, I32)])
    nxt_key = jnp.where(nxt < n_steps, jnp.take(key, jnp.minimum(nxt, n_steps - 1)), -1)
    nxt_e = jnp.where(nxt_key >= 0, nxt_key % N_EXPERTS, -1)
    nxt_j = jnp.maximum(nxt_key, 0) // N_EXPERTS
    return jnp.concatenate([first, slot, ee, jj, nxt_e, nxt_j]).astype(I32)


def _weight_stream_step(sched_ref, n_steps, step, w_hbms, layer, wbufs, stg, sems, cnt, col_block):
    ch = stg.shape[1]
    k_rows = wbufs[0].shape[1]
    nrb = k_rows // ch
    n_chunks = len(w_hbms) * nrb
    first = sched_ref[step]
    slot = sched_ref[n_steps + step]
    cur_e = sched_ref[2 * n_steps + step]
    cur_j = sched_ref[3 * n_steps + step]
    nxt_e = sched_ref[4 * n_steps + step]
    nxt_j = sched_ref[5 * n_steps + step]

    def chunk_copy(a, rb, e, j, c):
        src = w_hbms[a].at[layer, e, pl.ds(rb * ch, ch), pl.ds(j * col_block, col_block)]
        return pltpu.make_async_copy(src, stg.at[c % 2], sems.at[c % 2])

    def start(c, e, j):
        for a in range(len(w_hbms)):
            @pl.when(c // nrb == a)
            def _():
                chunk_copy(a, c - a * nrb, e, j, c).start()

    def finish(c, to_slot):
        chunk_copy(0, 0, 0, 0, c).wait()
        for a in range(len(w_hbms)):
            @pl.when(c // nrb == a)
            def _():
                r0 = pl.multiple_of((c - a * nrb) * ch, ch)
                wbufs[a][to_slot, pl.ds(r0, ch), :] = stg[c % 2].astype(BF16)

    @pl.when(step == 0)
    def _():
        cnt[0] = 0
        cnt[1] = 0

    @pl.when(first == 1)
    def _():
        started = cnt[1]

        def catch_up(c, carry):
            @pl.when(c >= started)
            def _():
                start(c, cur_e, cur_j)
            finish(c, slot)
            return carry

        lax.fori_loop(cnt[0], n_chunks, catch_up, 0)
        cnt[0] = 0
        cnt[1] = 0

    @pl.when(nxt_e >= 0)
    def _():
        done, started = cnt[0], cnt[1]

        def fin(c, carry):
            finish(c, 1 - slot)
            return carry

        def beg(c, carry):
            start(c, nxt_e, nxt_j)
            return carry

        lax.fori_loop(done, started, fin, 0)
        upto = jnp.minimum(started + W_STREAM_CHUNKS_PER_STEP, n_chunks)
        lax.fori_loop(started, upto, beg, 0)
        cnt[0] = started
        cnt[1] = upto

    return slot


def _expert_up_body(te_ref, sched_ref, hs_ref, wg_hbm, wu_hbm, a_ref, wg_buf, wu_buf, stg, sems, cnt,
                    *, n_tiles, layer):
    i = pl.program_id(1)
    step = pl.program_id(0) * n_tiles + i
    slot = _weight_stream_step(sched_ref, 2 * n_tiles, step, (wg_hbm, wu_hbm), layer, (wg_buf, wu_buf),
                               stg, sems, cnt, a_ref.shape[1])

    @pl.when(i < te_ref[n_tiles])
    def _():
        h = _unpack_rows(hs_ref[...])
        g = jnp.dot(h, wg_buf[slot], preferred_element_type=F32)
        u = jnp.dot(h, wu_buf[slot], preferred_element_type=F32)
        a_ref[...] = (_silu(g) * u).astype(BF16)

    @pl.when(i >= te_ref[n_tiles])
    def _():
        a_ref[...] = jnp.zeros_like(a_ref)


def _expert_up(tile_table, hs, wg, wu, layer, tme):
    P, Dh = hs.shape
    _, _, D, F = wg.shape
    tn = F // 2 if (F // 2) % LANE == 0 else F
    n_pass = F // tn
    n_tiles = P // tme
    ch = min(W_STREAM_CHUNK_ROWS, D)
    sched = _expert_schedule(tile_table, n_tiles, n_pass)
    live = lambda i, te: jnp.minimum(i, te[n_tiles] - 1)
    return pl.pallas_call(
        functools.partial(_expert_up_body, n_tiles=n_tiles, layer=layer),
        grid_spec=pltpu.PrefetchScalarGridSpec(
            num_scalar_prefetch=2,
            grid=(n_pass, n_tiles),
            in_specs=[
                pl.BlockSpec((tme, Dh), lambda j, i, te, sc: (live(i, te), 0)),
                pl.BlockSpec(memory_space=pl.ANY),
                pl.BlockSpec(memory_space=pl.ANY),
            ],
            out_specs=pl.BlockSpec((tme, tn), lambda j, i, te, sc: (i, j)),
            scratch_shapes=[pltpu.VMEM((2, D, tn), BF16), pltpu.VMEM((2, D, tn), BF16),
                            pltpu.VMEM((2, ch, tn), F32), pltpu.SemaphoreType.DMA((2,)),
                            pltpu.SMEM((2,), I32)],
        ),
        out_shape=jax.ShapeDtypeStruct((P, F), BF16),
        compiler_params=_cp("arbitrary", "arbitrary"),
        name="moe_expert_up",
    )(tile_table, sched, hs, wg, wu)


def _expert_down_body(te_ref, sched_ref, a_ref, w_hbm, y_ref, w_buf, stg, sems, cnt, *, n_tiles, layer):
    i = pl.program_id(0)
    slot = _weight_stream_step(sched_ref, n_tiles, i, (w_hbm,), layer, (w_buf,), stg, sems, cnt,
                               y_ref.shape[1])

    @pl.when(i < te_ref[n_tiles])
    def _():
        y_ref[...] = jnp.dot(a_ref[...], w_buf[slot], preferred_element_type=F32)

    @pl.when(i >= te_ref[n_tiles])
    def _():
        y_ref[...] = jnp.zeros_like(y_ref)


def _expert_down(tile_table, act, wd, layer, tme):
    P, F = act.shape
    D = wd.shape[3]
    n_tiles = P // tme
    ch = min(W_STREAM_CHUNK_ROWS, F)
    sched = _expert_schedule(tile_table, n_tiles, 1)
    return pl.pallas_call(
        functools.partial(_expert_down_body, n_tiles=n_tiles, layer=layer),
        grid_spec=pltpu.PrefetchScalarGridSpec(
            num_scalar_prefetch=2,
            grid=(n_tiles,),
            in_specs=[
                pl.BlockSpec((tme, F), lambda i, te, sc: (jnp.minimum(i, te[n_tiles] - 1), 0)),
                pl.BlockSpec(memory_space=pl.ANY),
            ],
            out_specs=pl.BlockSpec((tme, D), lambda i, te, sc: (i, 0)),
            scratch_shapes=[pltpu.VMEM((2, F, D), BF16), pltpu.VMEM((2, ch, D), F32),
                            pltpu.SemaphoreType.DMA((2,)), pltpu.SMEM((2,), I32)],
        ),
        out_shape=jax.ShapeDtypeStruct((P, D), F32),
        compiler_params=_cp("arbitrary"),
        name="moe_expert_down",
    )(tile_table, sched, act, wd)


def _combine_body(d1_ref, d2_ref, d1n_ref, d2n_ref, mf_ref, x_ref, fn_ref, y_ref, o_ref, ya, yb, sems,
                  *, final_norm):
    i = pl.program_id(0)
    n = x_ref.shape[0]
    slot = i % 2

    def fetch(ia_ref, ib_ref, to):
        def issue(blk, carry):
            for u in range(ROW_DMA_UNROLL):
                r = blk * ROW_DMA_UNROLL + u
                pltpu.make_async_copy(y_ref.at[pl.ds(ia_ref[r], 1)], ya.at[to, pl.ds(r, 1)],
                                      sems.at[to]).start(priority=0)
                pltpu.make_async_copy(y_ref.at[pl.ds(ib_ref[r], 1)], yb.at[to, pl.ds(r, 1)],
                                      sems.at[to]).start(priority=1)
            return carry
        lax.fori_loop(0, n // ROW_DMA_UNROLL, issue, 0)

    @pl.when(i == 0)
    def _():
        fetch(d1_ref, d2_ref, 0)

    @pl.when(i < pl.num_programs(0) - 1)
    def _():
        fetch(d1n_ref, d2n_ref, 1 - slot)

    def drain(blk, carry):
        row = pltpu.make_async_copy(y_ref.at[pl.ds(0, 1)], ya.at[slot, pl.ds(0, 1)], sems.at[slot])
        for _ in range(2 * ROW_DMA_UNROLL):
            row.wait()
        return carry

    lax.fori_loop(0, n // ROW_DMA_UNROLL, drain, 0)

    gates = mf_ref[...]
    eye = lax.broadcasted_iota(I32, (n, n), 0) == lax.broadcasted_iota(I32, (n, n), 1)
    g1 = jnp.sum(jnp.where(eye, gates[0:1, :], 0.0), axis=1, keepdims=True)
    g2 = jnp.sum(jnp.where(eye, gates[1:2, :], 0.0), axis=1, keepdims=True)
    out = x_ref[...] + (g1 * ya[slot] + g2 * yb[slot])
    if final_norm:
        out = _rms(out, fn_ref[...])
    o_ref[...] = out


def _combine(d1, d2, mf, x2, fnorm, y, final_norm):
    T, D = x2.shape
    tm = _tile(T, 256)
    last = T // tm - 1
    cur = lambda i: (i,)
    nxt = lambda i: (jnp.minimum(i + 1, last),)
    return pl.pallas_call(
        functools.partial(_combine_body, final_norm=final_norm),
        grid=(T // tm,),
        in_specs=[
            pl.BlockSpec((tm,), cur, memory_space=pltpu.SMEM),
            pl.BlockSpec((tm,), cur, memory_space=pltpu.SMEM),
            pl.BlockSpec((tm,), nxt, memory_space=pltpu.SMEM),
            pl.BlockSpec((tm,), nxt, memory_space=pltpu.SMEM),
            pl.BlockSpec((N_EXPERTS, tm), lambda i: (0, i)),
            pl.BlockSpec((tm, D), lambda i: (i, 0)),
            pl.BlockSpec((1, D), lambda i: (0, 0)),
            pl.BlockSpec(memory_space=pl.ANY),
        ],
        out_specs=pl.BlockSpec((tm, D), lambda i: (i, 0)),
        out_shape=jax.ShapeDtypeStruct((T, D), F32),
        scratch_shapes=[pltpu.VMEM((2, tm, D), F32), pltpu.VMEM((2, tm, D), F32),
                        pltpu.SemaphoreType.DMA((2,))],
        compiler_params=_cp("arbitrary"),
        name="moe_combine",
    )(d1, d2, d1, d2, mf, x2, fnorm, y)


def _pad_rows16(w_t):
    pad = jnp.zeros((BF16_SUBLANE_TILE - w_t.shape[0], w_t.shape[1]), F32)
    w = jnp.concatenate([w_t, pad], axis=0)
    hi = w.astype(BF16)
    lo = (w - hi.astype(F32)).astype(BF16)
    return hi, lo


def _attention_layer(x2, B, S, gnorm, w_in_all, layer, b_forget, w_out):
    sb0 = 3 * HEADS_W + N_HEADS
    col = lambda part: w_in_all[layer, :, part * HEADS_W:(part + 1) * HEADS_W]
    sbc = lambda part: w_in_all[layer, :, sb0 + part * HEADS_W:sb0 + (part + 1) * HEADS_W]
    w_k = jnp.concatenate([col(1), sbc(1)], axis=1).astype(BF16)
    w_qvt = jnp.concatenate([col(0), sbc(0), col(2), sbc(2)], axis=1).T.astype(BF16)
    wf_hi, wf_lo = _pad_rows16(w_in_all[layer, :, 3 * HEADS_W:sb0].T)
    keys, qvt, f16 = _att_in_proj(x2, gnorm, w_k, w_qvt, wf_hi, wf_lo)
    cb = _forget_cumsum(f16, b_forget.reshape(N_HEADS, 1), B, S)
    oa = _fox_attention(keys, qvt, cb, B, S)
    ob = _sb_attention(keys, qvt, B, S)
    return _att_out_proj(oa, ob, w_out, layer, x2)


def _dense_ffn_layer(x2, gnorm, wg, wu, wd, layer):
    a = _ffn_up(x2, gnorm, wg, wu, layer)
    return _ffn_down(a, wd, layer, x2)


def _conv_layer(x2, S, gnorm, w_in, conv_w, w_out, layer):
    gb, gg = _conv_in(x2, gnorm, w_in, layer)
    return _conv_out(gb, gg, conv_w, w_out, layer, x2, S)


def _moe_layer(x2, gnorm, w_router, wg, wu, wd, layer, fnorm, final_norm):
    T, D = x2.shape
    tme = 512 if T >= 4096 else 128
    n_tiles = (2 * T) // tme + N_EXPERTS
    P = n_tiles * tme

    wr_hi, wr_lo = _pad_rows16(w_router.T)
    hp, mi, mf, cnt = _router(x2, gnorm, wr_hi, wr_lo)

    counts = cnt[:, 0].astype(I32)
    padded = ((counts + tme - 1) // tme) * tme
    ends = jnp.cumsum(padded)
    offs = ends - padded
    d1 = jnp.take(offs, mi[0]) + mi[2]
    d2 = jnp.take(offs, mi[1]) + mi[3]
    tile_start = jnp.arange(n_tiles, dtype=I32) * tme
    tile_expert = jnp.sum((tile_start[:, None] >= ends[None, :]).astype(I32), axis=1)
    tile_expert = jnp.minimum(tile_expert, N_EXPERTS - 1)
    tile_table = jnp.concatenate([tile_expert, (ends[-1:] // tme).astype(I32)])

    pad_info = jnp.concatenate([offs + counts, padded - counts, ends[-1:] // tme]).astype(I32)
    hs = _dispatch(hp, d1, d2, pad_info, P, tme)
    act = _expert_up(tile_table, hs, wg, wu, layer, tme)
    y = _expert_down(tile_table, act, wd, layer, tme)
    return _combine(d1, d2, mf, x2, fnorm, y, final_norm)


def kernel(x, mix_norm, ffn_norm, final_norm, w_in_att, b_forget, w_out_att, w_in_conv, conv_w,
           w_out_conv, w_gate_dense, w_up_dense, w_down_dense, w_router, w_gate_moe, w_up_moe,
           w_down_moe):
    B, S, D = x.shape
    depth = mix_norm.shape[0]
    assert depth % 2 == 0, "the final rmsnorm is fused into the last (routed) layer"
    x2 = x.reshape(B * S, D)
    fnorm = final_norm.reshape(1, D)
    w_out_att, w_in_conv, w_out_conv, w_gate_dense, w_up_dense, w_down_dense = (
        w.astype(BF16) for w in (w_out_att, w_in_conv, w_out_conv, w_gate_dense, w_up_dense, w_down_dense))
    for i in range(depth):
        j = i // 2
        mg = mix_norm[i].reshape(1, D)
        fg = ffn_norm[i].reshape(1, D)
        if i % 2 == 0:
            x2 = _attention_layer(x2, B, S, mg, w_in_att, j, b_forget[j], w_out_att)
            x2 = _dense_ffn_layer(x2, fg, w_gate_dense, w_up_dense, w_down_dense, j)
        else:
            x2 = _conv_layer(x2, S, mg, w_in_conv, conv_w[j], w_out_conv, j)
            x2 = _moe_layer(x2, fg, w_router[j], w_gate_moe, w_up_moe, w_down_moe, j,
                            fnorm, final_norm=(i == depth - 1))
    return x2.reshape(B, S, D)
```

```python
import functools
import math

import jax
import jax.numpy as jnp
from jax import lax
from jax.experimental import pallas as pl
from jax.experimental.pallas import tpu as pltpu

F32 = jnp.float32
BF16 = jnp.bfloat16
I32 = jnp.int32
U32 = jnp.uint32

HEAD_DIM = 128
N_HEADS = 8
HEADS_W = N_HEADS * HEAD_DIM
N_EXPERTS = 8
RMS_EPS = 1e-6
CONV_WIDTH = 3
LOG2E = math.log2(math.e)
QK_SCALE_LOG2 = LOG2E / math.sqrt(HEAD_DIM)

V7X_VMEM_LIMIT_BYTES = 56 * 1024 * 1024
LANE = 128
BF16_SUBLANE_TILE = 16
NEG_INF = float("-inf")

_NT = (((1,), (1,)), ((), ()))


def _cp(*sem):
    return pltpu.CompilerParams(dimension_semantics=sem, vmem_limit_bytes=V7X_VMEM_LIMIT_BYTES)


def _tile(n, pref, unit=LANE):
    if n <= pref:
        return n
    t = (pref // unit) * unit
    while t > unit and n % t:
        t -= unit
    assert n % t == 0, (n, pref)
    return t


def _rms(x, g):
    ms = jnp.mean(x * x, axis=-1, keepdims=True)
    return x * lax.rsqrt(ms + RMS_EPS) * g


def _split_bf16(v):
    hi = v.astype(BF16)
    lo = (v - hi.astype(F32)).astype(BF16)
    return hi, lo


def _log_sigmoid_pair(z):
    sp = jnp.log1p(jnp.exp(-jnp.abs(z)))
    return jnp.minimum(z, 0.0) - sp, -jnp.maximum(z, 0.0) - sp


def _silu(g):
    return g / (1.0 + jnp.exp(-g))


def _skinny_nt(wh_ref, wl_ref, hb, hl, share_pass):
    wh = wh_ref[...]
    if not share_pass:
        out = lax.dot_general(wh, hb, _NT, preferred_element_type=F32)
        out += lax.dot_general(wh, hl, _NT, preferred_element_type=F32)
        return out + lax.dot_general(wl_ref[...], hb, _NT, preferred_element_type=F32)
    rows = wh.shape[0]
    both = lax.dot_general(jnp.concatenate([wh, wl_ref[...]], axis=0), hb, _NT, preferred_element_type=F32)
    return both[:rows] + both[rows:] + lax.dot_general(wh, hl, _NT, preferred_element_type=F32)


def _att_in_body(x_ref, g_ref, w_ref, wv_ref, wfh_ref, wfl_ref, o_ref, vt_ref, f_ref, h_scr, *, q_blocks, n_main):
    j = pl.program_id(1)

    @pl.when(j == 0)
    def _():
        h = _rms(x_ref[...], g_ref[...])
        hb, hl = _split_bf16(h)
        h_scr[...] = hb
        f_ref[...] = _skinny_nt(wfh_ref, wfl_ref, hb, hl, share_pass=False)

    @pl.when(j < n_main)
    def _():
        o_ref[...] = jnp.dot(h_scr[...], w_ref[...], preferred_element_type=F32).astype(BF16)

    @pl.when(j >= n_main)
    def _():
        acc = lax.dot_general(wv_ref[...], h_scr[...], _NT, preferred_element_type=F32)
        is_q = (j - n_main) < 2 * q_blocks
        vt_ref[...] = (acc * jnp.where(is_q, QK_SCALE_LOG2, 1.0)).astype(BF16)


def _att_in_proj(x2, gnorm, w_k, w_qvt, wf_hi, wf_lo):
    T, D = x2.shape
    N = w_k.shape[1]
    NV = w_qvt.shape[0]
    tm = _tile(T, 1024)
    tn = _tile(HEADS_W, 1024)
    n_main = N // tn
    return pl.pallas_call(
        functools.partial(_att_in_body, q_blocks=HEADS_W // tn, n_main=n_main),
        grid=(T // tm, n_main + NV // tn),
        in_specs=[
            pl.BlockSpec((tm, D), lambda i, j: (i, 0)),
            pl.BlockSpec((1, D), lambda i, j: (0, 0)),
            pl.BlockSpec((D, tn), lambda i, j: (0, jnp.minimum(j, n_main - 1))),
            pl.BlockSpec((tn, D), lambda i, j: (jnp.maximum(j - n_main, 0), 0)),
            pl.BlockSpec((BF16_SUBLANE_TILE, D), lambda i, j: (0, 0)),
            pl.BlockSpec((BF16_SUBLANE_TILE, D), lambda i, j: (0, 0)),
        ],
        out_specs=[
            pl.BlockSpec((tm, tn), lambda i, j: (i, jnp.minimum(j, n_main - 1))),
            pl.BlockSpec((tn, tm), lambda i, j: (jnp.maximum(j - n_main, 0), i)),
            pl.BlockSpec((BF16_SUBLANE_TILE, tm), lambda i, j: (0, i)),
        ],
        out_shape=[
            jax.ShapeDtypeStruct((T, N), BF16),
            jax.ShapeDtypeStruct((NV, T), BF16),
            jax.ShapeDtypeStruct((BF16_SUBLANE_TILE, T), F32),
        ],
        scratch_shapes=[pltpu.VMEM((tm, D), BF16)],
        compiler_params=_cp("parallel", "arbitrary"),
        name="att_in_proj",
    )(x2, gnorm, w_k, w_qvt, wf_hi, wf_lo)


def _forget_cumsum_body(f_ref, b_ref, cb_ref):
    z = f_ref[...] + b_ref[...]
    lf, _ = _log_sigmoid_pair(z)
    S = lf.shape[1]
    lane = lax.broadcasted_iota(I32, lf.shape, 1)
    c = lf
    sh = 1
    while sh < S:
        c = c + jnp.where(lane >= sh, pltpu.roll(c, sh, axis=1), 0.0)
        sh *= 2
    c2 = c * LOG2E
    hi = c2.astype(BF16).astype(F32)
    r1 = c2 - hi
    mid = r1.astype(BF16).astype(F32)
    lo = (r1 - mid).astype(BF16).astype(F32)
    pad = jnp.zeros((LANE - 3 * N_HEADS, S), F32)
    cb_ref[...] = jnp.concatenate([hi, mid, lo, pad], axis=0).T.astype(BF16)


def _forget_cumsum(f16, b_col, B, S):
    return pl.pallas_call(
        _forget_cumsum_body,
        grid=(B,),
        in_specs=[
            pl.BlockSpec((N_HEADS, S), lambda b: (0, b)),
            pl.BlockSpec((N_HEADS, 1), lambda b: (0, 0)),
        ],
        out_specs=pl.BlockSpec((S, LANE), lambda b: (b, 0)),
        out_shape=jax.ShapeDtypeStruct((B * S, LANE), BF16),
        compiler_params=_cp("parallel"),
        name="forget_cumsum",
    )(f16, b_col)


def _head_cols(g):
    return slice(g * HEAD_DIM, (g + 1) * HEAD_DIM)


def _fox_body(qt_ref, k_ref, vt_ref, cb_ref, o_ref, sc_a, sc_b, m_scr, l_scr, acc_scr, *, tq, heads):
    hg = pl.program_id(1)
    i = pl.program_id(2)
    sub = lax.broadcasted_iota(I32, (LANE, tq), 0)
    qs = []
    for g in range(heads):
        h = hg * heads + g
        pick = (sub == h) | (sub == N_HEADS + h) | (sub == 2 * N_HEADS + h)
        qs.append(jnp.concatenate([qt_ref[_head_cols(g), :], jnp.where(pick, -1.0, 0.0).astype(BF16)], axis=0))
    key = lax.broadcasted_iota(I32, (tq, tq), 0)
    qry = lax.broadcasted_iota(I32, (tq, tq), 1)

    def scores(kt, sc_scr):
        ks = pl.multiple_of(kt * tq, tq)
        cb = cb_ref[pl.ds(ks, tq), :]
        for g in range(heads):
            k_aug = jnp.concatenate([k_ref[pl.ds(ks, tq), _head_cols(g)], cb], axis=1)
            sc_scr[g] = jnp.dot(k_aug, qs[g], preferred_element_type=F32)

    def finish(kt, sc_scr, masked):
        ks = pl.multiple_of(kt * tq, tq)
        probs = []
        for g in range(heads):
            m = m_scr[g]
            s = sc_scr[g]
            if masked:
                s = jnp.where(key <= qry, s, NEG_INF)
            m_new = jnp.maximum(m, jnp.max(s, axis=0, keepdims=True))
            alpha = jnp.exp2(m - m_new)
            p = jnp.exp2(s - m_new)
            m_scr[g] = m_new
            l_scr[g] = alpha * l_scr[g] + jnp.sum(p, axis=0, keepdims=True)
            probs.append((alpha, p.astype(BF16)))
        for g in range(heads):
            alpha, p = probs[g]
            vt = vt_ref[_head_cols(g), pl.ds(ks, tq)]
            acc_scr[g] = alpha * acc_scr[g] + jnp.dot(vt, p, preferred_element_type=F32)

    def pair(k2, carry):
        scores(2 * k2 + 1, sc_b)
        finish(2 * k2, sc_a, False)
        scores(2 * k2 + 2, sc_a)
        finish(2 * k2 + 1, sc_b, False)
        return carry

    m_scr[...] = jnp.full(m_scr.shape, NEG_INF, F32)
    l_scr[...] = jnp.zeros(l_scr.shape, F32)
    acc_scr[...] = jnp.zeros(acc_scr.shape, F32)
    scores(0, sc_a)
    lax.fori_loop(0, i // 2, pair, 0)

    @pl.when(i % 2 == 0)
    def _():
        finish(i, sc_a, True)

    @pl.when(i % 2 == 1)
    def _():
        scores(i, sc_b)
        finish(i - 1, sc_a, False)
        finish(i, sc_b, True)

    for g in range(heads):
        o_ref[:, _head_cols(g)] = (acc_scr[g] / l_scr[g]).T.astype(BF16)


ATT_TILE = 256
ATT_HEADS_PER_STEP = 8


def _fox_attention(keys, qvt, cb, B, S):
    T = B * S
    tq = _tile(S, ATT_TILE)
    nq = S // tq
    hp = ATT_HEADS_PER_STEP
    ng = N_HEADS // hp
    w = hp * HEAD_DIM
    return pl.pallas_call(
        functools.partial(_fox_body, tq=tq, heads=hp),
        grid=(B, ng, nq),
        in_specs=[
            pl.BlockSpec((w, tq), lambda b, h, i: (h, b * nq + i)),
            pl.BlockSpec((S, w), lambda b, h, i: (b, h)),
            pl.BlockSpec((w, S), lambda b, h, i: (2 * ng + h, b)),
            pl.BlockSpec((S, LANE), lambda b, h, i: (b, 0)),
        ],
        out_specs=pl.BlockSpec((tq, w), lambda b, h, i: (b * nq + i, h)),
        out_shape=jax.ShapeDtypeStruct((T, HEADS_W), BF16),
        scratch_shapes=[pltpu.VMEM((hp, tq, tq), F32), pltpu.VMEM((hp, tq, tq), F32),
                        pltpu.VMEM((hp, 1, tq), F32), pltpu.VMEM((hp, 1, tq), F32),
                        pltpu.VMEM((hp, HEAD_DIM, tq), F32)],
        compiler_params=_cp("parallel", "parallel", "arbitrary"),
        name="fox_attention",
    )(qvt, keys, qvt, cb)


F32_EXP2_UNDERFLOW = -150.0


def _sb_body(qt_ref, k_ref, vt_ref, ut_ref, o_ref, r_scr, acc_scr, *, tq, heads):
    i = pl.program_id(2)
    qs = [qt_ref[_head_cols(g), :] for g in range(heads)]
    ut = ut_ref[...]
    key = lax.broadcasted_iota(I32, (tq, tq), 0)
    qry = lax.broadcasted_iota(I32, (tq, tq), 1)
    strict = key < qry

    def tile(kt, masked):
        ks = pl.multiple_of(kt * tq, tq)
        zs = [jnp.dot(k_ref[pl.ds(ks, tq), _head_cols(g)], qs[g], preferred_element_type=F32)
              for g in range(heads)]
        mid = []
        for g in range(heads):
            z = zs[g]
            log_beta = jnp.minimum(z, 0.0) - jnp.log2(1.0 + jnp.exp2(-jnp.abs(z)))
            log_om = log_beta - z
            if masked:
                log_om = jnp.where(strict, log_om, 0.0)
            hi, lo = _split_bf16(log_om)
            e = jnp.dot(ut, jnp.concatenate([hi, lo], axis=0), preferred_element_type=F32)
            mid.append((log_beta, log_om, e))
        for g in range(heads):
            log_beta, log_om, e = mid[g]
            r_sum = r_scr[g]
            a = jnp.exp2(log_beta + e + r_sum)
            if masked:
                a = jnp.where(strict, a, 0.0)
            vt = vt_ref[_head_cols(g), pl.ds(ks, tq)]
            acc_scr[g] += jnp.dot(vt, a.astype(BF16), preferred_element_type=F32)
            r_scr[g] = r_sum + jnp.sum(log_om, axis=0, keepdims=True)

    def live():
        return (jnp.max(r_scr[...]) > F32_EXP2_UNDERFLOW).astype(I32)

    r_scr[...] = jnp.zeros(r_scr.shape, F32)
    acc_scr[...] = jnp.zeros(acc_scr.shape, F32)
    tile(i, True)

    def step(state):
        n, _ = state
        tile(i - 1 - n, False)
        return n + 1, live()

    lax.while_loop(lambda st: (st[0] < i) & (st[1] > 0), step, (jnp.int32(0), live()))
    for g in range(heads):
        o_ref[:, _head_cols(g)] = acc_scr[g].T.astype(BF16)


def _sb_attention(keys, qvt, B, S):
    T = B * S
    tq = _tile(S, ATT_TILE)
    nq = S // tq
    hp = ATT_HEADS_PER_STEP
    ng = N_HEADS // hp
    w = hp * HEAD_DIM
    r = lax.broadcasted_iota(I32, (tq, tq), 0)
    c = lax.broadcasted_iota(I32, (tq, tq), 1)
    ut = (c > r).astype(BF16)
    ut = jnp.concatenate([ut, ut], axis=1)
    return pl.pallas_call(
        functools.partial(_sb_body, tq=tq, heads=hp),
        grid=(B, ng, nq),
        in_specs=[
            pl.BlockSpec((w, tq), lambda b, h, i: (ng + h, b * nq + i)),
            pl.BlockSpec((S, w), lambda b, h, i: (b, ng + h)),
            pl.BlockSpec((w, S), lambda b, h, i: (3 * ng + h, b)),
            pl.BlockSpec((tq, 2 * tq), lambda b, h, i: (0, 0)),
        ],
        out_specs=pl.BlockSpec((tq, w), lambda b, h, i: (b * nq + i, h)),
        out_shape=jax.ShapeDtypeStruct((T, HEADS_W), BF16),
        scratch_shapes=[pltpu.VMEM((hp, 1, tq), F32), pltpu.VMEM((hp, HEAD_DIM, tq), F32)],
        compiler_params=_cp("parallel", "parallel", "arbitrary"),
        name="sb_attention",
    )(qvt, keys, qvt, ut)


def _att_out_body(oa_ref, ob_ref, wa_ref, wb_ref, x_ref, o_ref):
    acc = jnp.dot(oa_ref[...], wa_ref[...], preferred_element_type=F32)
    acc += jnp.dot(ob_ref[...], wb_ref[...], preferred_element_type=F32)
    o_ref[...] = x_ref[...] + acc


def _att_out_proj(oa, ob, w_out, layer, x2):
    T, D = x2.shape
    tm = _tile(T, 512)
    return pl.pallas_call(
        _att_out_body,
        grid=(T // tm,),
        in_specs=[
            pl.BlockSpec((tm, HEADS_W), lambda i: (i, 0)),
            pl.BlockSpec((tm, HEADS_W), lambda i: (i, 0)),
            pl.BlockSpec((None, HEADS_W, D), lambda i: (layer, 0, 0)),
            pl.BlockSpec((None, HEADS_W, D), lambda i: (layer, 1, 0)),
            pl.BlockSpec((tm, D), lambda i: (i, 0)),
        ],
        out_specs=pl.BlockSpec((tm, D), lambda i: (i, 0)),
        out_shape=jax.ShapeDtypeStruct((T, D), F32),
        compiler_params=_cp("parallel"),
        name="att_out_proj",
    )(oa, ob, w_out, w_out, x2)


def _ffn_up_body(x_ref, g_ref, wg_ref, wu_ref, a_ref, h_scr):
    @pl.when(pl.program_id(1) == 0)
    def _():
        h_scr[...] = _rms(x_ref[...], g_ref[...]).astype(BF16)

    h = h_scr[...]
    g = jnp.dot(h, wg_ref[...], preferred_element_type=F32)
    u = jnp.dot(h, wu_ref[...], preferred_element_type=F32)
    a_ref[...] = (_silu(g) * u).astype(BF16)


def _ffn_up(x2, gnorm, wg, wu, layer):
    T, D = x2.shape
    F = wg.shape[2]
    tm = _tile(T, 1024)
    tn = _tile(F, 512)
    return pl.pallas_call(
        _ffn_up_body,
        grid=(T // tm, F // tn),
        in_specs=[
            pl.BlockSpec((tm, D), lambda i, j: (i, 0)),
            pl.BlockSpec((1, D), lambda i, j: (0, 0)),
            pl.BlockSpec((None, D, tn), lambda i, j: (layer, 0, j)),
            pl.BlockSpec((None, D, tn), lambda i, j: (layer, 0, j)),
        ],
        out_specs=pl.BlockSpec((tm, tn), lambda i, j: (i, j)),
        out_shape=jax.ShapeDtypeStruct((T, F), BF16),
        scratch_shapes=[pltpu.VMEM((tm, D), BF16)],
        compiler_params=_cp("parallel", "arbitrary"),
        name="ffn_up",
    )(x2, gnorm, wg, wu)


def _ffn_down_body(a_ref, w_ref, x_ref, o_ref):
    o_ref[...] = x_ref[...] + jnp.dot(a_ref[...], w_ref[...], preferred_element_type=F32)


def _ffn_down(a, wd, layer, x2):
    T, D = x2.shape
    F = a.shape[1]
    tm = _tile(T, 1024)
    tn = _tile(D, 512)
    return pl.pallas_call(
        _ffn_down_body,
        grid=(T // tm, D // tn),
        in_specs=[
            pl.BlockSpec((tm, F), lambda i, j: (i, 0)),
            pl.BlockSpec((None, F, tn), lambda i, j: (layer, 0, j)),
            pl.BlockSpec((tm, tn), lambda i, j: (i, j)),
        ],
        out_specs=pl.BlockSpec((tm, tn), lambda i, j: (i, j)),
        out_shape=jax.ShapeDtypeStruct((T, D), F32),
        compiler_params=_cp("parallel", "parallel"),
        name="ffn_down",
    )(a, wd, x2)


def _conv_in_body(x_ref, g_ref, wb_ref, wc_ref, wu_ref, gb_ref, gg_ref, h_scr):
    @pl.when(pl.program_id(1) == 0)
    def _():
        h_scr[...] = _rms(x_ref[...], g_ref[...]).astype(BF16)

    h = h_scr[...]
    gb_ref[...] = jnp.dot(h, wb_ref[...], preferred_element_type=F32).astype(BF16)
    c = jnp.dot(h, wc_ref[...], preferred_element_type=F32)
    u = jnp.dot(h, wu_ref[...], preferred_element_type=F32)
    gg_ref[...] = (c * u).astype(BF16)


def _conv_in(x2, gnorm, w_in, layer):
    T, D = x2.shape
    tm = _tile(T, 1024)
    tn = _tile(D, 512)
    nd = D // tn
    return pl.pallas_call(
        _conv_in_body,
        grid=(T // tm, nd),
        in_specs=[
            pl.BlockSpec((tm, D), lambda i, j: (i, 0)),
            pl.BlockSpec((1, D), lambda i, j: (0, 0)),
            pl.BlockSpec((None, D, tn), lambda i, j: (layer, 0, j)),
            pl.BlockSpec((None, D, tn), lambda i, j: (layer, 0, nd + j)),
            pl.BlockSpec((None, D, tn), lambda i, j: (layer, 0, 2 * nd + j)),
        ],
        out_specs=[
            pl.BlockSpec((tm, tn), lambda i, j: (i, j)),
            pl.BlockSpec((tm, tn), lambda i, j: (i, j)),
        ],
        out_shape=[jax.ShapeDtypeStruct((T, D), BF16), jax.ShapeDtypeStruct((T, D), BF16)],
        scratch_shapes=[pltpu.VMEM((tm, D), BF16)],
        compiler_params=_cp("parallel", "arbitrary"),
        name="conv_in",
    )(x2, gnorm, w_in, w_in, w_in)


def _conv_out_body(gb_ref, g_ref, gp_ref, cw_ref, w_ref, x_ref, o_ref, *, tiles_per_seq):
    i = pl.program_id(0)
    g = g_ref[...].astype(F32)
    tm = g.shape[0]
    keep = jnp.where(i % tiles_per_seq == 0, 0.0, 1.0)
    prev = gp_ref[...].astype(F32) * keep
    p1 = prev[BF16_SUBLANE_TILE - 1:BF16_SUBLANE_TILE, :]
    p2 = prev[BF16_SUBLANE_TILE - 2:BF16_SUBLANE_TILE - 1, :]
    row = lax.broadcasted_iota(I32, g.shape, 0)
    g1 = jnp.where(row == 0, p1, pltpu.roll(g, 1, axis=0))
    g2 = jnp.where(row == 0, p2, jnp.where(row == 1, p1, pltpu.roll(g, 2, axis=0)))
    cw = cw_ref[...]
    conv = g2 * cw[0:1, :] + g1 * cw[1:2, :] + g * cw[2:3, :]
    y = (gb_ref[...].astype(F32) * conv).astype(BF16)
    o_ref[...] = x_ref[...] + jnp.dot(y, w_ref[...], preferred_element_type=F32)


def _conv_out(gb, gg, conv_w, w_out, layer, x2, S):
    T, D = x2.shape
    tm = _tile(S, 256)
    pt = BF16_SUBLANE_TILE
    return pl.pallas_call(
        functools.partial(_conv_out_body, tiles_per_seq=S // tm),
        grid=(T // tm,),
        in_specs=[
            pl.BlockSpec((tm, D), lambda i: (i, 0)),
            pl.BlockSpec((tm, D), lambda i: (i, 0)),
            pl.BlockSpec((pt, D), lambda i: (jnp.maximum(i * (tm // pt) - 1, 0), 0)),
            pl.BlockSpec((CONV_WIDTH, D), lambda i: (0, 0)),
            pl.BlockSpec((None, D, D), lambda i: (layer, 0, 0)),
            pl.BlockSpec((tm, D), lambda i: (i, 0)),
        ],
        out_specs=pl.BlockSpec((tm, D), lambda i: (i, 0)),
        out_shape=jax.ShapeDtypeStruct((T, D), F32),
        compiler_params=_cp("parallel"),
        name="conv_out",
    )(gb, gg, gg, conv_w, w_out, x2)


def _router_body(x_ref, g_ref, wrh_ref, wrl_ref, tri_ref, hp_ref, mi_ref, mf_ref, cnt_ref, carry):
    @pl.when(pl.program_id(0) == 0)
    def _():
        carry[...] = jnp.zeros_like(carry)

    h = _rms(x_ref[...], g_ref[...])
    hb, hl = _split_bf16(h)
    bits = pltpu.bitcast(hb.astype(F32), U32)
    half = bits.shape[1] // 2
    hp_ref[...] = (bits[:, :half] >> 16) | bits[:, half:]

    logits = _skinny_nt(wrh_ref, wrl_ref, hb, hl, share_pass=True)[:N_EXPERTS]
    eidx = lax.broadcasted_iota(I32, logits.shape, 0).astype(F32)
    ne = float(N_EXPERTS)
    m1 = jnp.max(logits, axis=0, keepdims=True)
    i1 = jnp.min(jnp.where(logits == m1, eidx, ne), axis=0, keepdims=True)
    rest = jnp.where(eidx == i1, NEG_INF, logits)
    m2 = jnp.max(rest, axis=0, keepdims=True)
    i2 = jnp.min(jnp.where(rest == m2, eidx, ne), axis=0, keepdims=True)
    e21 = jnp.exp(m2 - m1)
    g1 = 1.0 / (1.0 + e21)
    g2 = e21 * g1

    sel = jnp.where((eidx == i1) | (eidx == i2), 1.0, 0.0)
    incl = jnp.dot(sel, tri_ref[...], preferred_element_type=F32)
    pos = carry[...] + incl - sel
    carry[...] = carry[...] + jnp.sum(sel, axis=1, keepdims=True)
    p1 = jnp.sum(jnp.where(eidx == i1, pos, 0.0), axis=0, keepdims=True)
    p2 = jnp.sum(jnp.where(eidx == i2, pos, 0.0), axis=0, keepdims=True)
    meta = jnp.where(eidx == 0, i1, jnp.where(eidx == 1, i2, jnp.where(eidx == 2, p1, jnp.where(eidx == 3, p2, 0.0))))
    mi_ref[...] = meta.astype(I32)
    mf_ref[...] = jnp.where(eidx == 0, g1, jnp.where(eidx == 1, g2, 0.0))
    cnt_ref[...] = jnp.broadcast_to(carry[...], cnt_ref.shape)


def _router(x2, gnorm, wr_hi, wr_lo):
    T, D = x2.shape
    tm = _tile(T, 512)
    r = lax.broadcasted_iota(I32, (tm, tm), 0)
    c = lax.broadcasted_iota(I32, (tm, tm), 1)
    tri = (r <= c).astype(F32)
    return pl.pallas_call(
        _router_body,
        grid=(T // tm,),
        in_specs=[
            pl.BlockSpec((tm, D), lambda i: (i, 0)),
            pl.BlockSpec((1, D), lambda i: (0, 0)),
            pl.BlockSpec((BF16_SUBLANE_TILE, D), lambda i: (0, 0)),
            pl.BlockSpec((BF16_SUBLANE_TILE, D), lambda i: (0, 0)),
            pl.BlockSpec((tm, tm), lambda i: (0, 0)),
        ],
        out_specs=[
            pl.BlockSpec((tm, D // 2), lambda i: (i, 0)),
            pl.BlockSpec((N_EXPERTS, tm), lambda i: (0, i)),
            pl.BlockSpec((N_EXPERTS, tm), lambda i: (0, i)),
            pl.BlockSpec((N_EXPERTS, LANE), lambda i: (0, 0)),
        ],
        out_shape=[
            jax.ShapeDtypeStruct((T, D // 2), U32),
            jax.ShapeDtypeStruct((N_EXPERTS, T), I32),
            jax.ShapeDtypeStruct((N_EXPERTS, T), F32),
            jax.ShapeDtypeStruct((N_EXPERTS, LANE), F32),
        ],
        scratch_shapes=[pltpu.VMEM((N_EXPERTS, 1), F32)],
        compiler_params=_cp("arbitrary"),
        name="moe_router",
    )(x2, gnorm, wr_hi, wr_lo, tri)


ROW_DMA_UNROLL = 8


def _dispatch_body(d1_ref, d2_ref, pad_ref, hp_ref, hs_ref, stage, ztile, sems, zsem, *, tme):
    i = pl.program_id(0)
    n = hp_ref.shape[0]
    slot = i % 2
    stage[slot] = hp_ref[...]

    def issue(blk, carry):
        for u in range(ROW_DMA_UNROLL):
            r = blk * ROW_DMA_UNROLL + u
            src = stage.at[slot, pl.ds(r, 1)]
            pltpu.make_async_copy(src, hs_ref.at[pl.ds(d1_ref[r], 1)], sems.at[slot]).start(priority=0)
            pltpu.make_async_copy(src, hs_ref.at[pl.ds(d2_ref[r], 1)], sems.at[slot]).start(priority=1)
        return carry

    def drain(which):
        def body(blk, carry):
            row = pltpu.make_async_copy(stage.at[which, pl.ds(0, 1)], hs_ref.at[pl.ds(0, 1)], sems.at[which])
            for _ in range(2 * ROW_DMA_UNROLL):
                row.wait()
            return carry
        lax.fori_loop(0, n // ROW_DMA_UNROLL, body, 0)

    lax.fori_loop(0, n // ROW_DMA_UNROLL, issue, 0)

    @pl.when(i > 0)
    def _():
        drain(1 - slot)

    @pl.when(i == pl.num_programs(0) - 1)
    def _():
        drain(slot)
        ztile[...] = jnp.zeros_like(ztile)
        zrow = pltpu.make_async_copy(ztile.at[pl.ds(0, 1)], hs_ref.at[pl.ds(0, 1)], zsem)
        for e in range(N_EXPERTS):
            first, count = pad_ref[e], pad_ref[N_EXPERTS + e]

            def fill(r, carry):
                pltpu.make_async_copy(ztile.at[pl.ds(0, 1)], hs_ref.at[pl.ds(first + r, 1)], zsem).start()
                return carry

            def fill_done(r, carry):
                zrow.wait()
                return carry

            lax.fori_loop(0, count, fill, 0)
            lax.fori_loop(0, count, fill_done, 0)

        def clear_tile(t, carry):
            whole = pltpu.make_async_copy(ztile, hs_ref.at[pl.ds(pl.multiple_of(t * tme, tme), tme)], zsem)
            whole.start()
            whole.wait()
            return carry

        lax.fori_loop(pad_ref[2 * N_EXPERTS], hs_ref.shape[0] // tme, clear_tile, 0)


def _dispatch(hp, d1, d2, pad_info, P, tme):
    T, Dh = hp.shape
    tm = _tile(T, 256)
    return pl.pallas_call(
        functools.partial(_dispatch_body, tme=tme),
        grid=(T // tm,),
        in_specs=[
            pl.BlockSpec((tm,), lambda i: (i,), memory_space=pltpu.SMEM),
            pl.BlockSpec((tm,), lambda i: (i,), memory_space=pltpu.SMEM),
            pl.BlockSpec(memory_space=pltpu.SMEM),
            pl.BlockSpec((tm, Dh), lambda i: (i, 0)),
        ],
        out_specs=pl.BlockSpec(memory_space=pl.ANY),
        out_shape=jax.ShapeDtypeStruct((P, Dh), U32),
        scratch_shapes=[pltpu.VMEM((2, tm, Dh), U32), pltpu.VMEM((tme, Dh), U32),
                        pltpu.SemaphoreType.DMA((2,)), pltpu.SemaphoreType.DMA(())],
        compiler_params=_cp("arbitrary"),
        name="moe_dispatch",
    )(d1, d2, pad_info, hp)


def _unpack_rows(words):
    lo = pltpu.bitcast(words << 16, F32)
    hi = pltpu.bitcast(words & jnp.uint32(0xFFFF0000), F32)
    return jnp.concatenate([lo, hi], axis=1).astype(BF16)


W_STREAM_CHUNK_ROWS = 256
W_STREAM_CHUNKS_PER_STEP = 2


def _expert_schedule(tile_table, n_tiles, n_pass):
    n_steps = n_pass * n_tiles
    s = jnp.arange(n_steps, dtype=I32)
    jj = s // n_tiles
    ii = jnp.minimum(s % n_tiles, tile_table[n_tiles] - 1)
    ee = jnp.take(tile_table, ii)
    key = jj * N_EXPERTS + ee
    first = jnp.concatenate([jnp.ones((1,), I32), (key[1:] != key[:-1]).astype(I32)])
    slot = (jnp.cumsum(first) - 1) % 2
    starts = jnp.where(first == 1, s, n_steps)
    nxt = jnp.concatenate([lax.cummin(starts, reverse=True)[1:], jnp.full((1,), n_steps, I32)])
    nxt_key = jnp.where(nxt < n_steps, jnp.take(key, jnp.minimum(nxt, n_steps - 1)), -1)
    nxt_e = jnp.where(nxt_key >= 0, nxt_key % N_EXPERTS, -1)
    nxt_j = jnp.maximum(nxt_key, 0) // N_EXPERTS
    return jnp.concatenate([first, slot, ee, jj, nxt_e, nxt_j]).astype(I32)


def _weight_stream_step(sched_ref, n_steps, step, w_hbms, layer, wbufs, stg, sems, cnt, col_block):
    ch = stg.shape[1]
    k_rows = wbufs[0].shape[1]
    nrb = k_rows // ch
    n_chunks = len(w_hbms) * nrb
    first = sched_ref[step]
    slot = sched_ref[n_steps + step]
    cur_e = sched_ref[2 * n_steps + step]
    cur_j = sched_ref[3 * n_steps + step]
    nxt_e = sched_ref[4 * n_steps + step]
    nxt_j = sched_ref[5 * n_steps + step]

    def chunk_copy(a, rb, e, j, c):
        src = w_hbms[a].at[layer, e, pl.ds(rb * ch, ch), pl.ds(j * col_block, col_block)]
        return pltpu.make_async_copy(src, stg.at[c % 2], sems.at[c % 2])

    def start(c, e, j):
        for a in range(len(w_hbms)):
            @pl.when(c // nrb == a)
            def _():
                chunk_copy(a, c - a * nrb, e, j, c).start()

    def finish(c, to_slot):
        chunk_copy(0, 0, 0, 0, c).wait()
        for a in range(len(w_hbms)):
            @pl.when(c // nrb == a)
            def _():
                r0 = pl.multiple_of((c - a * nrb) * ch, ch)
                wbufs[a][to_slot, pl.ds(r0, ch), :] = stg[c % 2].astype(BF16)

    @pl.when(step == 0)
    def _():
        cnt[0] = 0
        cnt[1] = 0

    @pl.when(first == 1)
    def _():
        started = cnt[1]

        def catch_up(c, carry):
            @pl.when(c >= started)
            def _():
                start(c, cur_e, cur_j)
            finish(c, slot)
            return carry

        lax.fori_loop(cnt[0], n_chunks, catch_up, 0)
        cnt[0] = 0
        cnt[1] = 0

    @pl.when(nxt_e >= 0)
    def _():
        done, started = cnt[0], cnt[1]

        def fin(c, carry):
            finish(c, 1 - slot)
            return carry

        def beg(c, carry):
            start(c, nxt_e, nxt_j)
            return carry

        lax.fori_loop(done, started, fin, 0)
        upto = jnp.minimum(started + W_STREAM_CHUNKS_PER_STEP, n_chunks)
        lax.fori_loop(started, upto, beg, 0)
        cnt[0] = started
        cnt[1] = upto

    return slot


def _expert_up_body(te_ref, sched_ref, hs_ref, wg_hbm, wu_hbm, a_ref, wg_buf, wu_buf, stg, sems, cnt,
                    *, n_tiles, layer):
    i = pl.program_id(1)
    step = pl.program_id(0) * n_tiles + i
    slot = _weight_stream_step(sched_ref, 2 * n_tiles, step, (wg_hbm, wu_hbm), layer, (wg_buf, wu_buf),
                               stg, sems, cnt, a_ref.shape[1])

    @pl.when(i < te_ref[n_tiles])
    def _():
        h = _unpack_rows(hs_ref[...])
        g = jnp.dot(h, wg_buf[slot], preferred_element_type=F32)
        u = jnp.dot(h, wu_buf[slot], preferred_element_type=F32)
        a_ref[...] = (_silu(g) * u).astype(BF16)

    @pl.when(i >= te_ref[n_tiles])
    def _():
        a_ref[...] = jnp.zeros_like(a_ref)


def _expert_up(tile_table, hs, wg, wu, layer, tme):
    P, Dh = hs.shape
    _, _, D, F = wg.shape
    tn = F // 2 if (F // 2) % LANE == 0 else F
    n_pass = F // tn
    n_tiles = P // tme
    ch = min(W_STREAM_CHUNK_ROWS, D)
    sched = _expert_schedule(tile_table, n_tiles, n_pass)
    live = lambda i, te: jnp.minimum(i, te[n_tiles] - 1)
    return pl.pallas_call(
        functools.partial(_expert_up_body, n_tiles=n_tiles, layer=layer),
        grid_spec=pltpu.PrefetchScalarGridSpec(
            num_scalar_prefetch=2,
            grid=(n_pass, n_tiles),
            in_specs=[
                pl.BlockSpec((tme, Dh), lambda j, i, te, sc: (live(i, te), 0)),
                pl.BlockSpec(memory_space=pl.ANY),
                pl.BlockSpec(memory_space=pl.ANY),
            ],
            out_specs=pl.BlockSpec((tme, tn), lambda j, i, te, sc: (i, j)),
            scratch_shapes=[pltpu.VMEM((2, D, tn), BF16), pltpu.VMEM((2, D, tn), BF16),
                            pltpu.VMEM((2, ch, tn), F32), pltpu.SemaphoreType.DMA((2,)),
                            pltpu.SMEM((2,), I32)],
        ),
        out_shape=jax.ShapeDtypeStruct((P, F), BF16),
        compiler_params=_cp("arbitrary", "arbitrary"),
        name="moe_expert_up",
    )(tile_table, sched, hs, wg, wu)


def _expert_down_body(te_ref, sched_ref, a_ref, w_hbm, y_ref, w_buf, stg, sems, cnt, *, n_tiles, layer):
    i = pl.program_id(0)
    slot = _weight_stream_step(sched_ref, n_tiles, i, (w_hbm,), layer, (w_buf,), stg, sems, cnt,
                               y_ref.shape[1])

    @pl.when(i < te_ref[n_tiles])
    def _():
        y_ref[...] = jnp.dot(a_ref[...], w_buf[slot], preferred_element_type=F32)

    @pl.when(i >= te_ref[n_tiles])
    def _():
        y_ref[...] = jnp.zeros_like(y_ref)


def _expert_down(tile_table, act, wd, layer, tme):
    P, F = act.shape
    D = wd.shape[3]
    n_tiles = P // tme
    ch = min(W_STREAM_CHUNK_ROWS, F)
    sched = _expert_schedule(tile_table, n_tiles, 1)
    return pl.pallas_call(
        functools.partial(_expert_down_body, n_tiles=n_tiles, layer=layer),
        grid_spec=pltpu.PrefetchScalarGridSpec(
            num_scalar_prefetch=2,
            grid=(n_tiles,),
            in_specs=[
                pl.BlockSpec((tme, F), lambda i, te, sc: (jnp.minimum(i, te[n_tiles] - 1), 0)),
                pl.BlockSpec(memory_space=pl.ANY),
            ],
            out_specs=pl.BlockSpec((tme, D), lambda i, te, sc: (i, 0)),
            scratch_shapes=[pltpu.VMEM((2, F, D), BF16), pltpu.VMEM((2, ch, D), F32),
                            pltpu.SemaphoreType.DMA((2,)), pltpu.SMEM((2,), I32)],
        ),
        out_shape=jax.ShapeDtypeStruct((P, D), F32),
        compiler_params=_cp("arbitrary"),
        name="moe_expert_down",
    )(tile_table, sched, act, wd)


def _combine_body(d1_ref, d2_ref, d1n_ref, d2n_ref, mf_ref, x_ref, fn_ref, y_ref, o_ref, ya, yb, sems,
                  *, final_norm):
    i = pl.program_id(0)
    n = x_ref.shape[0]
    slot = i % 2

    def fetch(ia_ref, ib_ref, to):
        def issue(blk, carry):
            for u in range(ROW_DMA_UNROLL):
                r = blk * ROW_DMA_UNROLL + u
                pltpu.make_async_copy(y_ref.at[pl.ds(ia_ref[r], 1)], ya.at[to, pl.ds(r, 1)],
                                      sems.at[to]).start(priority=0)
                pltpu.make_async_copy(y_ref.at[pl.ds(ib_ref[r], 1)], yb.at[to, pl.ds(r, 1)],
                                      sems.at[to]).start(priority=1)
            return carry
        lax.fori_loop(0, n // ROW_DMA_UNROLL, issue, 0)

    @pl.when(i == 0)
    def _():
        fetch(d1_ref, d2_ref, 0)

    @pl.when(i < pl.num_programs(0) - 1)
    def _():
        fetch(d1n_ref, d2n_ref, 1 - slot)

    def drain(blk, carry):
        row = pltpu.make_async_copy(y_ref.at[pl.ds(0, 1)], ya.at[slot, pl.ds(0, 1)], sems.at[slot])
        for _ in range(2 * ROW_DMA_UNROLL):
            row.wait()
        return carry

    lax.fori_loop(0, n // ROW_DMA_UNROLL, drain, 0)

    gates = mf_ref[...]
    eye = lax.broadcasted_iota(I32, (n, n), 0) == lax.broadcasted_iota(I32, (n, n), 1)
    g1 = jnp.sum(jnp.where(eye, gates[0:1, :], 0.0), axis=1, keepdims=True)
    g2 = jnp.sum(jnp.where(eye, gates[1:2, :], 0.0), axis=1, keepdims=True)
    out = x_ref[...] + (g1 * ya[slot] + g2 * yb[slot])
    if final_norm:
        out = _rms(out, fn_ref[...])
    o_ref[...] = out


def _combine(d1, d2, mf, x2, fnorm, y, final_norm):
    T, D = x2.shape
    tm = _tile(T, 256)
    last = T // tm - 1
    cur = lambda i: (i,)
    nxt = lambda i: (jnp.minimum(i + 1, last),)
    return pl.pallas_call(
        functools.partial(_combine_body, final_norm=final_norm),
        grid=(T // tm,),
        in_specs=[
            pl.BlockSpec((tm,), cur, memory_space=pltpu.SMEM),
            pl.BlockSpec((tm,), cur, memory_space=pltpu.SMEM),
            pl.BlockSpec((tm,), nxt, memory_space=pltpu.SMEM),
            pl.BlockSpec((tm,), nxt, memory_space=pltpu.SMEM),
            pl.BlockSpec((N_EXPERTS, tm), lambda i: (0, i)),
            pl.BlockSpec((tm, D), lambda i: (i, 0)),
            pl.BlockSpec((1, D), lambda i: (0, 0)),
            pl.BlockSpec(memory_space=pl.ANY),
        ],
        out_specs=pl.BlockSpec((tm, D), lambda i: (i, 0)),
        out_shape=jax.ShapeDtypeStruct((T, D), F32),
        scratch_shapes=[pltpu.VMEM((2, tm, D), F32), pltpu.VMEM((2, tm, D), F32),
                        pltpu.SemaphoreType.DMA((2,))],
        compiler_params=_cp("arbitrary"),
        name="moe_combine",
    )(d1, d2, d1, d2, mf, x2, fnorm, y)


def _pad_rows16(w_t):
    pad = jnp.zeros((BF16_SUBLANE_TILE - w_t.shape[0], w_t.shape[1]), F32)
    w = jnp.concatenate([w_t, pad], axis=0)
    hi = w.astype(BF16)
    lo = (w - hi.astype(F32)).astype(BF16)
    return hi, lo


def _attention_layer(x2, B, S, gnorm, w_in_all, layer, b_forget, w_out):
    sb0 = 3 * HEADS_W + N_HEADS
    col = lambda part: w_in_all[layer, :, part * HEADS_W:(part + 1) * HEADS_W]
    sbc = lambda part: w_in_all[layer, :, sb0 + part * HEADS_W:sb0 + (part + 1) * HEADS_W]
    w_k = jnp.concatenate([col(1), sbc(1)], axis=1).astype(BF16)
    w_qvt = jnp.concatenate([col(0), sbc(0), col(2), sbc(2)], axis=1).T.astype(BF16)
    wf_hi, wf_lo = _pad_rows16(w_in_all[layer, :, 3 * HEADS_W:sb0].T)
    keys, qvt, f16 = _att_in_proj(x2, gnorm, w_k, w_qvt, wf_hi, wf_lo)
    cb = _forget_cumsum(f16, b_forget.reshape(N_HEADS, 1), B, S)
    oa = _fox_attention(keys, qvt, cb, B, S)
    ob = _sb_attention(keys, qvt, B, S)
    return _att_out_proj(oa, ob, w_out, layer, x2)


def _dense_ffn_layer(x2, gnorm, wg, wu, wd, layer):
    a = _ffn_up(x2, gnorm, wg, wu, layer)
    return _ffn_down(a, wd, layer, x2)


def _conv_layer(x2, S, gnorm, w_in, conv_w, w_out, layer):
    gb, gg = _conv_in(x2, gnorm, w_in, layer)
    return _conv_out(gb, gg, conv_w, w_out, layer, x2, S)


def _moe_layer(x2, gnorm, w_router, wg, wu, wd, layer, fnorm, final_norm):
    T, D = x2.shape
    tme = 512 if T >= 4096 else 128
    n_tiles = (2 * T) // tme + N_EXPERTS
    P = n_tiles * tme

    wr_hi, wr_lo = _pad_rows16(w_router.T)
    hp, mi, mf, cnt = _router(x2, gnorm, wr_hi, wr_lo)

    counts = cnt[:, 0].astype(I32)
    padded = ((counts + tme - 1) // tme) * tme
    ends = jnp.cumsum(padded)
    offs = ends - padded
    d1 = jnp.take(offs, mi[0]) + mi[2]
    d2 = jnp.take(offs, mi[1]) + mi[3]
    tile_start = jnp.arange(n_tiles, dtype=I32) * tme
    tile_expert = jnp.sum((tile_start[:, None] >= ends[None, :]).astype(I32), axis=1)
    tile_expert = jnp.minimum(tile_expert, N_EXPERTS - 1)
    tile_table = jnp.concatenate([tile_expert, (ends[-1:] // tme).astype(I32)])

    pad_info = jnp.concatenate([offs + counts, padded - counts, ends[-1:] // tme]).astype(I32)
    hs = _dispatch(hp, d1, d2, pad_info, P, tme)
    act = _expert_up(tile_table, hs, wg, wu, layer, tme)
    y = _expert_down(tile_table, act, wd, layer, tme)
    return _combine(d1, d2, mf, x2, fnorm, y, final_norm)


def kernel(x, mix_norm, ffn_norm, final_norm, w_in_att, b_forget, w_out_att, w_in_conv, conv_w,
           w_out_conv, w_gate_dense, w_up_dense, w_down_dense, w_router, w_gate_moe, w_up_moe,
           w_down_moe):
    B, S, D = x.shape
    depth = mix_norm.shape[0]
    assert depth % 2 == 0, "the final rmsnorm is fused into the last (routed) layer"
    x2 = x.reshape(B * S, D)
    fnorm = final_norm.reshape(1, D)
    w_out_att, w_in_conv, w_out_conv, w_gate_dense, w_up_dense, w_down_dense = (
        w.astype(BF16) for w in (w_out_att, w_in_conv, w_out_conv, w_gate_dense, w_up_dense, w_down_dense))
    for i in range(depth):
        j = i // 2
        mg = mix_norm[i].reshape(1, D)
        fg = ffn_norm[i].reshape(1, D)
        if i % 2 == 0:
            x2 = _attention_layer(x2, B, S, mg, w_in_att, j, b_forget[j], w_out_att)
            x2 = _dense_ffn_layer(x2, fg, w_gate_dense, w_up_dense, w_down_dense, j)
        else:
            x2 = _conv_layer(x2, S, mg, w_in_conv, conv_w[j], w_out_conv, j)
            x2 = _moe_layer(x2, fg, w_router[j], w_gate_moe, w_up_moe, w_down_moe, j,
                            fnorm, final_norm=(i == depth - 1))
    return x2.reshape(B, S, D)
```

```python
import functools
import math

import jax
import jax.numpy as jnp
from jax import lax
from jax.experimental import pallas as pl
from jax.experimental.pallas import tpu as pltpu

F32 = jnp.float32
BF16 = jnp.bfloat16
I32 = jnp.int32
U32 = jnp.uint32

HEAD_DIM = 128
N_HEADS = 8
HEADS_W = N_HEADS * HEAD_DIM
N_EXPERTS = 8
RMS_EPS = 1e-6
CONV_WIDTH = 3
LOG2E = math.log2(math.e)
QK_SCALE_LOG2 = LOG2E / math.sqrt(HEAD_DIM)

V7X_VMEM_LIMIT_BYTES = 56 * 1024 * 1024
LANE = 128
BF16_SUBLANE_TILE = 16
NEG_INF = float("-inf")

_NT = (((1,), (1,)), ((), ()))


def _cp(*sem):
    return pltpu.CompilerParams(dimension_semantics=sem, vmem_limit_bytes=V7X_VMEM_LIMIT_BYTES)


def _tile(n, pref, unit=LANE):
    if n <= pref:
        return n
    t = (pref // unit) * unit
    while t > unit and n % t:
        t -= unit
    assert n % t == 0, (n, pref)
    return t


def _rms(x, g):
    ms = jnp.mean(x * x, axis=-1, keepdims=True)
    return x * lax.rsqrt(ms + RMS_EPS) * g


def _split_bf16(v):
    hi = v.astype(BF16)
    lo = (v - hi.astype(F32)).astype(BF16)
    return hi, lo


def _log_sigmoid_pair(z):
    sp = jnp.log1p(jnp.exp(-jnp.abs(z)))
    return jnp.minimum(z, 0.0) - sp, -jnp.maximum(z, 0.0) - sp


def _silu(g):
    return g / (1.0 + jnp.exp(-g))


def _skinny_nt(wh_ref, wl_ref, hb, hl, share_pass):
    wh = wh_ref[...]
    if not share_pass:
        out = lax.dot_general(wh, hb, _NT, preferred_element_type=F32)
        out += lax.dot_general(wh, hl, _NT, preferred_element_type=F32)
        return out + lax.dot_general(wl_ref[...], hb, _NT, preferred_element_type=F32)
    rows = wh.shape[0]
    both = lax.dot_general(jnp.concatenate([wh, wl_ref[...]], axis=0), hb, _NT, preferred_element_type=F32)
    return both[:rows] + both[rows:] + lax.dot_general(wh, hl, _NT, preferred_element_type=F32)


def _att_in_body(x_ref, g_ref, w_ref, wv_ref, wfh_ref, wfl_ref, o_ref, vt_ref, f_ref, h_scr, *, q_blocks, n_main):
    j = pl.program_id(1)

    @pl.when(j == 0)
    def _():
        h = _rms(x_ref[...], g_ref[...])
        hb, hl = _split_bf16(h)
        h_scr[...] = hb
        f_ref[...] = _skinny_nt(wfh_ref, wfl_ref, hb, hl, share_pass=False)

    @pl.when(j < n_main)
    def _():
        o_ref[...] = jnp.dot(h_scr[...], w_ref[...], preferred_element_type=F32).astype(BF16)

    @pl.when(j >= n_main)
    def _():
        acc = lax.dot_general(wv_ref[...], h_scr[...], _NT, preferred_element_type=F32)
        is_q = (j - n_main) < 2 * q_blocks
        vt_ref[...] = (acc * jnp.where(is_q, QK_SCALE_LOG2, 1.0)).astype(BF16)


def _att_in_proj(x2, gnorm, w_k, w_qvt, wf_hi, wf_lo):
    T, D = x2.shape
    N = w_k.shape[1]
    NV = w_qvt.shape[0]
    tm = _tile(T, 1024)
    tn = _tile(HEADS_W, 1024)
    n_main = N // tn
    return pl.pallas_call(
        functools.partial(_att_in_body, q_blocks=HEADS_W // tn, n_main=n_main),
        grid=(T // tm, n_main + NV // tn),
        in_specs=[
            pl.BlockSpec((tm, D), lambda i, j: (i, 0)),
            pl.BlockSpec((1, D), lambda i, j: (0, 0)),
            pl.BlockSpec((D, tn), lambda i, j: (0, jnp.minimum(j, n_main - 1))),
            pl.BlockSpec((tn, D), lambda i, j: (jnp.maximum(j - n_main, 0), 0)),
            pl.BlockSpec((BF16_SUBLANE_TILE, D), lambda i, j: (0, 0)),
            pl.BlockSpec((BF16_SUBLANE_TILE, D), lambda i, j: (0, 0)),
        ],
        out_specs=[
            pl.BlockSpec((tm, tn), lambda i, j: (i, jnp.minimum(j, n_main - 1))),
            pl.BlockSpec((tn, tm), lambda i, j: (jnp.maximum(j - n_main, 0), i)),
            pl.BlockSpec((BF16_SUBLANE_TILE, tm), lambda i, j: (0, i)),
        ],
        out_shape=[
            jax.ShapeDtypeStruct((T, N), BF16),
            jax.ShapeDtypeStruct((NV, T), BF16),
            jax.ShapeDtypeStruct((BF16_SUBLANE_TILE, T), F32),
        ],
        scratch_shapes=[pltpu.VMEM((tm, D), BF16)],
        compiler_params=_cp("parallel", "arbitrary"),
        name="att_in_proj",
    )(x2, gnorm, w_k, w_qvt, wf_hi, wf_lo)


def _forget_cumsum_body(f_ref, b_ref, cb_ref):
    z = f_ref[...] + b_ref[...]
    lf, _ = _log_sigmoid_pair(z)
    S = lf.shape[1]
    lane = lax.broadcasted_iota(I32, lf.shape, 1)
    c = lf
    sh = 1
    while sh < S:
        c = c + jnp.where(lane >= sh, pltpu.roll(c, sh, axis=1), 0.0)
        sh *= 2
    c2 = c * LOG2E
    hi = c2.astype(BF16).astype(F32)
    r1 = c2 - hi
    mid = r1.astype(BF16).astype(F32)
    lo = (r1 - mid).astype(BF16).astype(F32)
    pad = jnp.zeros((LANE - 3 * N_HEADS, S), F32)
    cb_ref[...] = jnp.concatenate([hi, mid, lo, pad], axis=0).T.astype(BF16)


def _forget_cumsum(f16, b_col, B, S):
    return pl.pallas_call(
        _forget_cumsum_body,
        grid=(B,),
        in_specs=[
            pl.BlockSpec((N_HEADS, S), lambda b: (0, b)),
            pl.BlockSpec((N_HEADS, 1), lambda b: (0, 0)),
        ],
        out_specs=pl.BlockSpec((S, LANE), lambda b: (b, 0)),
        out_shape=jax.ShapeDtypeStruct((B * S, LANE), BF16),
        compiler_params=_cp("parallel"),
        name="forget_cumsum",
    )(f16, b_col)


def _head_cols(g):
    return slice(g * HEAD_DIM, (g + 1) * HEAD_DIM)


def _fox_body(qt_ref, k_ref, vt_ref, cb_ref, o_ref, sc_a, sc_b, m_scr, l_scr, acc_scr, *, tq, heads):
    hg = pl.program_id(1)
    i = pl.program_id(2)
    sub = lax.broadcasted_iota(I32, (LANE, tq), 0)
    qs = []
    for g in range(heads):
        h = hg * heads + g
        pick = (sub == h) | (sub == N_HEADS + h) | (sub == 2 * N_HEADS + h)
        qs.append(jnp.concatenate([qt_ref[_head_cols(g), :], jnp.where(pick, -1.0, 0.0).astype(BF16)], axis=0))
    key = lax.broadcasted_iota(I32, (tq, tq), 0)
    qry = lax.broadcasted_iota(I32, (tq, tq), 1)

    def scores(kt, sc_scr):
        ks = pl.multiple_of(kt * tq, tq)
        cb = cb_ref[pl.ds(ks, tq), :]
        for g in range(heads):
            k_aug = jnp.concatenate([k_ref[pl.ds(ks, tq), _head_cols(g)], cb], axis=1)
            sc_scr[g] = jnp.dot(k_aug, qs[g], preferred_element_type=F32)

    def finish(kt, sc_scr, masked):
        ks = pl.multiple_of(kt * tq, tq)
        probs = []
        for g in range(heads):
            m = m_scr[g]
            s = sc_scr[g]
            if masked:
                s = jnp.where(key <= qry, s, NEG_INF)
            m_new = jnp.maximum(m, jnp.max(s, axis=0, keepdims=True))
            alpha = jnp.exp2(m - m_new)
            p = jnp.exp2(s - m_new)
            m_scr[g] = m_new
            l_scr[g] = alpha * l_scr[g] + jnp.sum(p, axis=0, keepdims=True)
            probs.append((alpha, p.astype(BF16)))
        for g in range(heads):
            alpha, p = probs[g]
            vt = vt_ref[_head_cols(g), pl.ds(ks, tq)]
            acc_scr[g] = alpha * acc_scr[g] + jnp.dot(vt, p, preferred_element_type=F32)

    def pair(k2, carry):
        scores(2 * k2 + 1, sc_b)
        finish(2 * k2, sc_a, False)
        scores(2 * k2 + 2, sc_a)
        finish(2 * k2 + 1, sc_b, False)
        return carry

    m_scr[...] = jnp.full(m_scr.shape, NEG_INF, F32)
    l_scr[...] = jnp.zeros(l_scr.shape, F32)
    acc_scr[...] = jnp.zeros(acc_scr.shape, F32)
    scores(0, sc_a)
    lax.fori_loop(0, i // 2, pair, 0)

    @pl.when(i % 2 == 0)
    def _():
        finish(i, sc_a, True)

    @pl.when(i % 2 == 1)
    def _():
        scores(i, sc_b)
        finish(i - 1, sc_a, False)
        finish(i, sc_b, True)

    for g in range(heads):
        o_ref[:, _head_cols(g)] = (acc_scr[g] / l_scr[g]).T.astype(BF16)


ATT_TILE = 256
ATT_HEADS_PER_STEP = 8


def _fox_attention(keys, qvt, cb, B, S):
    T = B * S
    tq = _tile(S, ATT_TILE)
    nq = S // tq
    hp = ATT_HEADS_PER_STEP
    ng = N_HEADS // hp
    w = hp * HEAD_DIM
    return pl.pallas_call(
        functools.partial(_fox_body, tq=tq, heads=hp),
        grid=(B, ng, nq),
        in_specs=[
            pl.BlockSpec((w, tq), lambda b, h, i: (h, b * nq + i)),
            pl.BlockSpec((S, w), lambda b, h, i: (b, h)),
            pl.BlockSpec((w, S), lambda b, h, i: (2 * ng + h, b)),
            pl.BlockSpec((S, LANE), lambda b, h, i: (b, 0)),
        ],
        out_specs=pl.BlockSpec((tq, w), lambda b, h, i: (b * nq + i, h)),
        out_shape=jax.ShapeDtypeStruct((T, HEADS_W), BF16),
        scratch_shapes=[pltpu.VMEM((hp, tq, tq), F32), pltpu.VMEM((hp, tq, tq), F32),
                        pltpu.VMEM((hp, 1, tq), F32), pltpu.VMEM((hp, 1, tq), F32),
                        pltpu.VMEM((hp, HEAD_DIM, tq), F32)],
        compiler_params=_cp("parallel", "parallel", "arbitrary"),
        name="fox_attention",
    )(qvt, keys, qvt, cb)


F32_EXP2_UNDERFLOW = -150.0


def _sb_body(qt_ref, k_ref, vt_ref, ut_ref, o_ref, r_scr, acc_scr, *, tq, heads):
    i = pl.program_id(2)
    qs = [qt_ref[_head_cols(g), :] for g in range(heads)]
    ut = ut_ref[...]
    key = lax.broadcasted_iota(I32, (tq, tq), 0)
    qry = lax.broadcasted_iota(I32, (tq, tq), 1)
    strict = key < qry

    def tile(kt, masked):
        ks = pl.multiple_of(kt * tq, tq)
        zs = [jnp.dot(k_ref[pl.ds(ks, tq), _head_cols(g)], qs[g], preferred_element_type=F32)
              for g in range(heads)]
        mid = []
        for g in range(heads):
            z = zs[g]
            log_beta = jnp.minimum(z, 0.0) - jnp.log2(1.0 + jnp.exp2(-jnp.abs(z)))
            log_om = log_beta - z
            if masked:
                log_om = jnp.where(strict, log_om, 0.0)
            hi, lo = _split_bf16(log_om)
            e = jnp.dot(ut, jnp.concatenate([hi, lo], axis=0), preferred_element_type=F32)
            mid.append((log_beta, log_om, e))
        for g in range(heads):
            log_beta, log_om, e = mid[g]
            r_sum = r_scr[g]
            a = jnp.exp2(log_beta + e + r_sum)
            if masked:
                a = jnp.where(strict, a, 0.0)
            vt = vt_ref[_head_cols(g), pl.ds(ks, tq)]
            acc_scr[g] += jnp.dot(vt, a.astype(BF16), preferred_element_type=F32)
            r_scr[g] = r_sum + jnp.sum(log_om, axis=0, keepdims=True)

    def live():
        return (jnp.max(r_scr[...]) > F32_EXP2_UNDERFLOW).astype(I32)

    r_scr[...] = jnp.zeros(r_scr.shape, F32)
    acc_scr[...] = jnp.zeros(acc_scr.shape, F32)
    tile(i, True)

    def step(state):
        n, _ = state
        tile(i - 1 - n, False)
        return n + 1, live()

    lax.while_loop(lambda st: (st[0] < i) & (st[1] > 0), step, (jnp.int32(0), live()))
    for g in range(heads):
        o_ref[:, _head_cols(g)] = acc_scr[g].T.astype(BF16)


def _sb_attention(keys, qvt, B, S):
    T = B * S
    tq = _tile(S, ATT_TILE)
    nq = S // tq
    hp = ATT_HEADS_PER_STEP
    ng = N_HEADS // hp
    w = hp * HEAD_DIM
    r = lax.broadcasted_iota(I32, (tq, tq), 0)
    c = lax.broadcasted_iota(I32, (tq, tq), 1)
    ut = (c > r).astype(BF16)
    ut = jnp.concatenate([ut, ut], axis=1)
    return pl.pallas_call(
        functools.partial(_sb_body, tq=tq, heads=hp),
        grid=(B, ng, nq),
        in_specs=[
            pl.BlockSpec((w, tq), lambda b, h, i: (ng + h, b * nq + i)),
            pl.BlockSpec((S, w), lambda b, h, i: (b, ng + h)),
            pl.BlockSpec((w, S), lambda b, h, i: (3 * ng + h, b)),
            pl.BlockSpec((tq, 2 * tq), lambda b, h, i: (0, 0)),
        ],
        out_specs=pl.BlockSpec((tq, w), lambda b, h, i: (b * nq + i, h)),
        out_shape=jax.ShapeDtypeStruct((T, HEADS_W), BF16),
        scratch_shapes=[pltpu.VMEM((hp, 1, tq), F32), pltpu.VMEM((hp, HEAD_DIM, tq), F32)],
        compiler_params=_cp("parallel", "parallel", "arbitrary"),
        name="sb_attention",
    )(qvt, keys, qvt, ut)


def _att_out_body(oa_ref, ob_ref, wa_ref, wb_ref, x_ref, o_ref):
    acc = jnp.dot(oa_ref[...], wa_ref[...], preferred_element_type=F32)
    acc += jnp.dot(ob_ref[...], wb_ref[...], preferred_element_type=F32)
    o_ref[...] = x_ref[...] + acc


def _att_out_proj(oa, ob, w_out, layer, x2):
    T, D = x2.shape
    tm = _tile(T, 512)
    return pl.pallas_call(
        _att_out_body,
        grid=(T // tm,),
        in_specs=[
            pl.BlockSpec((tm, HEADS_W), lambda i: (i, 0)),
            pl.BlockSpec((tm, HEADS_W), lambda i: (i, 0)),
            pl.BlockSpec((None, HEADS_W, D), lambda i: (layer, 0, 0)),
            pl.BlockSpec((None, HEADS_W, D), lambda i: (layer, 1, 0)),
            pl.BlockSpec((tm, D), lambda i: (i, 0)),
        ],
        out_specs=pl.BlockSpec((tm, D), lambda i: (i, 0)),
        out_shape=jax.ShapeDtypeStruct((T, D), F32),
        compiler_params=_cp("parallel"),
        name="att_out_proj",
    )(oa, ob, w_out, w_out, x2)


def _ffn_up_body(x_ref, g_ref, wg_ref, wu_ref, a_ref, h_scr):
    @pl.when(pl.program_id(1) == 0)
    def _():
        h_scr[...] = _rms(x_ref[...], g_ref[...]).astype(BF16)

    h = h_scr[...]
    g = jnp.dot(h, wg_ref[...], preferred_element_type=F32)
    u = jnp.dot(h, wu_ref[...], preferred_element_type=F32)
    a_ref[...] = (_silu(g) * u).astype(BF16)


def _ffn_up(x2, gnorm, wg, wu, layer):
    T, D = x2.shape
    F = wg.shape[2]
    tm = _tile(T, 1024)
    tn = _tile(F, 512)
    return pl.pallas_call(
        _ffn_up_body,
        grid=(T // tm, F // tn),
        in_specs=[
            pl.BlockSpec((tm, D), lambda i, j: (i, 0)),
            pl.BlockSpec((1, D), lambda i, j: (0, 0)),
            pl.BlockSpec((None, D, tn), lambda i, j: (layer, 0, j)),
            pl.BlockSpec((None, D, tn), lambda i, j: (layer, 0, j)),
        ],
        out_specs=pl.BlockSpec((tm, tn), lambda i, j: (i, j)),
        out_shape=jax.ShapeDtypeStruct((T, F), BF16),
        scratch_shapes=[pltpu.VMEM((tm, D), BF16)],
        compiler_params=_cp("parallel", "arbitrary"),
        name="ffn_up",
    )(x2, gnorm, wg, wu)


def _ffn_down_body(a_ref, w_ref, x_ref, o_ref):
    o_ref[...] = x_ref[...] + jnp.dot(a_ref[...], w_ref[...], preferred_element_type=F32)


def _ffn_down(a, wd, layer, x2):
    T, D = x2.shape
    F = a.shape[1]
    tm = _tile(T, 512)
    tn = _tile(D, 1024)
    return pl.pallas_call(
        _ffn_down_body,
        grid=(T // tm, D // tn),
        in_specs=[
            pl.BlockSpec((tm, F), lambda i, j: (i, 0)),
            pl.BlockSpec((None, F, tn), lambda i, j: (layer, 0, j)),
            pl.BlockSpec((tm, tn), lambda i, j: (i, j)),
        ],
        out_specs=pl.BlockSpec((tm, tn), lambda i, j: (i, j)),
        out_shape=jax.ShapeDtypeStruct((T, D), F32),
        compiler_params=_cp("parallel", "parallel"),
        name="ffn_down",
    )(a, wd, x2)


def _conv_in_body(x_ref, g_ref, wb_ref, wc_ref, wu_ref, gb_ref, gg_ref, h_scr):
    @pl.when(pl.program_id(1) == 0)
    def _():
        h_scr[...] = _rms(x_ref[...], g_ref[...]).astype(BF16)

    h = h_scr[...]
    gb_ref[...] = jnp.dot(h, wb_ref[...], preferred_element_type=F32).astype(BF16)
    c = jnp.dot(h, wc_ref[...], preferred_element_type=F32)
    u = jnp.dot(h, wu_ref[...], preferred_element_type=F32)
    gg_ref[...] = (c * u).astype(BF16)


def _conv_in(x2, gnorm, w_in, layer):
    T, D = x2.shape
    tm = _tile(T, 1024)
    tn = _tile(D, 512)
    nd = D // tn
    return pl.pallas_call(
        _conv_in_body,
        grid=(T // tm, nd),
        in_specs=[
            pl.BlockSpec((tm, D), lambda i, j: (i, 0)),
            pl.BlockSpec((1, D), lambda i, j: (0, 0)),
            pl.BlockSpec((None, D, tn), lambda i, j: (layer, 0, j)),
            pl.BlockSpec((None, D, tn), lambda i, j: (layer, 0, nd + j)),
            pl.BlockSpec((None, D, tn), lambda i, j: (layer, 0, 2 * nd + j)),
        ],
        out_specs=[
            pl.BlockSpec((tm, tn), lambda i, j: (i, j)),
            pl.BlockSpec((tm, tn), lambda i, j: (i, j)),
        ],
        out_shape=[jax.ShapeDtypeStruct((T, D), BF16), jax.ShapeDtypeStruct((T, D), BF16)],
        scratch_shapes=[pltpu.VMEM((tm, D), BF16)],
        compiler_params=_cp("parallel", "arbitrary"),
        name="conv_in",
    )(x2, gnorm, w_in, w_in, w_in)


CONV_CHANNEL_CHUNK = 512


def _conv_out_body(gb_ref, g_ref, gp_ref, cw_ref, w_ref, x_ref, o_ref, *, tiles_per_seq):
    i = pl.program_id(0)
    tm, D = g_ref.shape
    keep = jnp.where(i % tiles_per_seq == 0, 0.0, 1.0)
    acc = x_ref[...]
    ck = _tile(D, CONV_CHANNEL_CHUNK)
    row = lax.broadcasted_iota(I32, (tm, ck), 0)
    for c0 in range(0, D, ck):
        cols = slice(c0, c0 + ck)
        g = g_ref[:, cols].astype(F32)
        prev = gp_ref[:, cols].astype(F32) * keep
        p1 = prev[BF16_SUBLANE_TILE - 1:BF16_SUBLANE_TILE, :]
        p2 = prev[BF16_SUBLANE_TILE - 2:BF16_SUBLANE_TILE - 1, :]
        g1 = jnp.where(row == 0, p1, pltpu.roll(g, 1, axis=0))
        g2 = jnp.where(row == 0, p2, jnp.where(row == 1, p1, pltpu.roll(g, 2, axis=0)))
        cw = cw_ref[:, cols]
        conv = g2 * cw[0:1, :] + g1 * cw[1:2, :] + g * cw[2:3, :]
        y = (gb_ref[:, cols].astype(F32) * conv).astype(BF16)
        acc = acc + jnp.dot(y, w_ref[cols, :], preferred_element_type=F32)
    o_ref[...] = acc


def _conv_out(gb, gg, conv_w, w_out, layer, x2, S):
    T, D = x2.shape
    tm = _tile(S, 256)
    pt = BF16_SUBLANE_TILE
    return pl.pallas_call(
        functools.partial(_conv_out_body, tiles_per_seq=S // tm),
        grid=(T // tm,),
        in_specs=[
            pl.BlockSpec((tm, D), lambda i: (i, 0)),
            pl.BlockSpec((tm, D), lambda i: (i, 0)),
            pl.BlockSpec((pt, D), lambda i: (jnp.maximum(i * (tm // pt) - 1, 0), 0)),
            pl.BlockSpec((CONV_WIDTH, D), lambda i: (0, 0)),
            pl.BlockSpec((None, D, D), lambda i: (layer, 0, 0)),
            pl.BlockSpec((tm, D), lambda i: (i, 0)),
        ],
        out_specs=pl.BlockSpec((tm, D), lambda i: (i, 0)),
        out_shape=jax.ShapeDtypeStruct((T, D), F32),
        compiler_params=_cp("parallel"),
        name="conv_out",
    )(gb, gg, gg, conv_w, w_out, x2)


def _router_body(x_ref, g_ref, wrh_ref, wrl_ref, tri_ref, hp_ref, mi_ref, mf_ref, cnt_ref, carry):
    @pl.when(pl.program_id(0) == 0)
    def _():
        carry[...] = jnp.zeros_like(carry)

    h = _rms(x_ref[...], g_ref[...])
    hb, hl = _split_bf16(h)
    bits = pltpu.bitcast(hb.astype(F32), U32)
    half = bits.shape[1] // 2
    hp_ref[...] = (bits[:, :half] >> 16) | bits[:, half:]

    logits = _skinny_nt(wrh_ref, wrl_ref, hb, hl, share_pass=True)[:N_EXPERTS]
    eidx = lax.broadcasted_iota(I32, logits.shape, 0).astype(F32)
    ne = float(N_EXPERTS)
    m1 = jnp.max(logits, axis=0, keepdims=True)
    i1 = jnp.min(jnp.where(logits == m1, eidx, ne), axis=0, keepdims=True)
    rest = jnp.where(eidx == i1, NEG_INF, logits)
    m2 = jnp.max(rest, axis=0, keepdims=True)
    i2 = jnp.min(jnp.where(rest == m2, eidx, ne), axis=0, keepdims=True)
    e21 = jnp.exp(m2 - m1)
    g1 = 1.0 / (1.0 + e21)
    g2 = e21 * g1

    sel = jnp.where((eidx == i1) | (eidx == i2), 1.0, 0.0)
    incl = jnp.dot(sel, tri_ref[...], preferred_element_type=F32)
    pos = carry[...] + incl - sel
    carry[...] = carry[...] + jnp.sum(sel, axis=1, keepdims=True)
    p1 = jnp.sum(jnp.where(eidx == i1, pos, 0.0), axis=0, keepdims=True)
    p2 = jnp.sum(jnp.where(eidx == i2, pos, 0.0), axis=0, keepdims=True)
    meta = jnp.where(eidx == 0, i1, jnp.where(eidx == 1, i2, jnp.where(eidx == 2, p1, jnp.where(eidx == 3, p2, 0.0))))
    mi_ref[...] = meta.astype(I32)
    mf_ref[...] = jnp.where(eidx == 0, g1, jnp.where(eidx == 1, g2, 0.0))
    cnt_ref[...] = jnp.broadcast_to(carry[...], cnt_ref.shape)


def _router(x2, gnorm, wr_hi, wr_lo):
    T, D = x2.shape
    tm = _tile(T, 512)
    r = lax.broadcasted_iota(I32, (tm, tm), 0)
    c = lax.broadcasted_iota(I32, (tm, tm), 1)
    tri = (r <= c).astype(F32)
    return pl.pallas_call(
        _router_body,
        grid=(T // tm,),
        in_specs=[
            pl.BlockSpec((tm, D), lambda i: (i, 0)),
            pl.BlockSpec((1, D), lambda i: (0, 0)),
            pl.BlockSpec((BF16_SUBLANE_TILE, D), lambda i: (0, 0)),
            pl.BlockSpec((BF16_SUBLANE_TILE, D), lambda i: (0, 0)),
            pl.BlockSpec((tm, tm), lambda i: (0, 0)),
        ],
        out_specs=[
            pl.BlockSpec((tm, D // 2), lambda i: (i, 0)),
            pl.BlockSpec((N_EXPERTS, tm), lambda i: (0, i)),
            pl.BlockSpec((N_EXPERTS, tm), lambda i: (0, i)),
            pl.BlockSpec((N_EXPERTS, LANE), lambda i: (0, 0)),
        ],
        out_shape=[
            jax.ShapeDtypeStruct((T, D // 2), U32),
            jax.ShapeDtypeStruct((N_EXPERTS, T), I32),
            jax.ShapeDtypeStruct((N_EXPERTS, T), F32),
            jax.ShapeDtypeStruct((N_EXPERTS, LANE), F32),
        ],
        scratch_shapes=[pltpu.VMEM((N_EXPERTS, 1), F32)],
        compiler_params=_cp("arbitrary"),
        name="moe_router",
    )(x2, gnorm, wr_hi, wr_lo, tri)


ROW_DMA_UNROLL = 8


def _dispatch_body(d1_ref, d2_ref, pad_ref, hp_ref, hs_ref, stage, ztile, sems, zsem, *, tme):
    i = pl.program_id(0)
    n = hp_ref.shape[0]
    slot = i % 2
    stage[slot] = hp_ref[...]

    def issue(blk, carry):
        for u in range(ROW_DMA_UNROLL):
            r = blk * ROW_DMA_UNROLL + u
            src = stage.at[slot, pl.ds(r, 1)]
            pltpu.make_async_copy(src, hs_ref.at[pl.ds(d1_ref[r], 1)], sems.at[slot]).start(priority=0)
            pltpu.make_async_copy(src, hs_ref.at[pl.ds(d2_ref[r], 1)], sems.at[slot]).start(priority=1)
        return carry

    def drain(which):
        def body(blk, carry):
            row = pltpu.make_async_copy(stage.at[which, pl.ds(0, 1)], hs_ref.at[pl.ds(0, 1)], sems.at[which])
            for _ in range(2 * ROW_DMA_UNROLL):
                row.wait()
            return carry
        lax.fori_loop(0, n // ROW_DMA_UNROLL, body, 0)

    lax.fori_loop(0, n // ROW_DMA_UNROLL, issue, 0)

    @pl.when(i > 0)
    def _():
        drain(1 - slot)

    @pl.when(i == pl.num_programs(0) - 1)
    def _():
        drain(slot)
        ztile[...] = jnp.zeros_like(ztile)
        zrow = pltpu.make_async_copy(ztile.at[pl.ds(0, 1)], hs_ref.at[pl.ds(0, 1)], zsem)
        for e in range(N_EXPERTS):
            first, count = pad_ref[e], pad_ref[N_EXPERTS + e]

            def fill(r, carry):
                pltpu.make_async_copy(ztile.at[pl.ds(0, 1)], hs_ref.at[pl.ds(first + r, 1)], zsem).start()
                return carry

            def fill_done(r, carry):
                zrow.wait()
                return carry

            lax.fori_loop(0, count, fill, 0)
            lax.fori_loop(0, count, fill_done, 0)

        def clear_tile(t, carry):
            whole = pltpu.make_async_copy(ztile, hs_ref.at[pl.ds(pl.multiple_of(t * tme, tme), tme)], zsem)
            whole.start()
            whole.wait()
            return carry

        lax.fori_loop(pad_ref[2 * N_EXPERTS], hs_ref.shape[0] // tme, clear_tile, 0)


def _dispatch(hp, d1, d2, pad_info, P, tme):
    T, Dh = hp.shape
    tm = _tile(T, 256)
    return pl.pallas_call(
        functools.partial(_dispatch_body, tme=tme),
        grid=(T // tm,),
        in_specs=[
            pl.BlockSpec((tm,), lambda i: (i,), memory_space=pltpu.SMEM),
            pl.BlockSpec((tm,), lambda i: (i,), memory_space=pltpu.SMEM),
            pl.BlockSpec(memory_space=pltpu.SMEM),
            pl.BlockSpec((tm, Dh), lambda i: (i, 0)),
        ],
        out_specs=pl.BlockSpec(memory_space=pl.ANY),
        out_shape=jax.ShapeDtypeStruct((P, Dh), U32),
        scratch_shapes=[pltpu.VMEM((2, tm, Dh), U32), pltpu.VMEM((tme, Dh), U32),
                        pltpu.SemaphoreType.DMA((2,)), pltpu.SemaphoreType.DMA(())],
        compiler_params=_cp("arbitrary"),
        name="moe_dispatch",
    )(d1, d2, pad_info, hp)


def _unpack_rows(words):
    lo = pltpu.bitcast(words << 16, F32)
    hi = pltpu.bitcast(words & jnp.uint32(0xFFFF0000), F32)
    return jnp.concatenate([lo, hi], axis=1).astype(BF16)


W_STREAM_CHUNK_ROWS = 256
W_STREAM_CHUNKS_PER_STEP = 2


def _expert_schedule(tile_table, n_tiles, n_pass):
    n_steps = n_pass * n_tiles
    s = jnp.arange(n_steps, dtype=I32)
    jj = s // n_tiles
    ii = jnp.minimum(s % n_tiles, tile_table[n_tiles] - 1)
    ee = jnp.take(tile_table, ii)
    key = jj * N_EXPERTS + ee
    first = jnp.concatenate([jnp.ones((1,), I32), (key[1:] != key[:-1]).astype(I32)])
    slot = (jnp.cumsum(first) - 1) % 2
    starts = jnp.where(first == 1, s, n_steps)
    nxt = jnp.concatenate([lax.cummin(starts, reverse=True)[1:], jnp.full((1,), n_steps, I32)])
    nxt_key = jnp.where(nxt < n_steps, jnp.take(key, jnp.minimum(nxt, n_steps - 1)), -1)
    nxt_e = jnp.where(nxt_key >= 0, nxt_key % N_EXPERTS, -1)
    nxt_j = jnp.maximum(nxt_key, 0) // N_EXPERTS
    return jnp.concatenate([first, slot, ee, jj, nxt_e, nxt_j]).astype(I32)


def _weight_stream_step(sched_ref, n_steps, step, w_hbms, layer, wbufs, stg, sems, cnt, col_block):
    ch = stg.shape[1]
    k_rows = wbufs[0].shape[1]
    nrb = k_rows // ch
    n_chunks = len(w_hbms) * nrb
    first = sched_ref[step]
    slot = sched_ref[n_steps + step]
    cur_e = sched_ref[2 * n_steps + step]
    cur_j = sched_ref[3 * n_steps + step]
    nxt_e = sched_ref[4 * n_steps + step]
    nxt_j = sched_ref[5 * n_steps + step]

    def chunk_copy(a, rb, e, j, c):
        src = w_hbms[a].at[layer, e, pl.ds(rb * ch, ch), pl.ds(j * col_block, col_block)]
        return pltpu.make_async_copy(src, stg.at[c % 2], sems.at[c % 2])

    def start(c, e, j):
        for a in range(len(w_hbms)):
            @pl.when(c // nrb == a)
            def _():
                chunk_copy(a, c - a * nrb, e, j, c).start()

    def finish(c, to_slot):
        chunk_copy(0, 0, 0, 0, c).wait()
        for a in range(len(w_hbms)):
            @pl.when(c // nrb == a)
            def _():
                r0 = pl.multiple_of((c - a * nrb) * ch, ch)
                wbufs[a][to_slot, pl.ds(r0, ch), :] = stg[c % 2].astype(BF16)

    @pl.when(step == 0)
    def _():
        cnt[0] = 0
        cnt[1] = 0

    @pl.when(first == 1)
    def _():
        started = cnt[1]

        def catch_up(c, carry):
            @pl.when(c >= started)
            def _():
                start(c, cur_e, cur_j)
            finish(c, slot)
            return carry

        lax.fori_loop(cnt[0], n_chunks, catch_up, 0)
        cnt[0] = 0
        cnt[1] = 0

    @pl.when(nxt_e >= 0)
    def _():
        done, started = cnt[0], cnt[1]

        def fin(c, carry):
            finish(c, 1 - slot)
            return carry

        def beg(c, carry):
            start(c, nxt_e, nxt_j)
            return carry

        lax.fori_loop(done, started, fin, 0)
        upto = jnp.minimum(started + W_STREAM_CHUNKS_PER_STEP, n_chunks)
        lax.fori_loop(started, upto, beg, 0)
        cnt[0] = started
        cnt[1] = upto

    return slot


def _expert_up_body(te_ref, sched_ref, hs_ref, wg_hbm, wu_hbm, a_ref, wg_buf, wu_buf, stg, sems, cnt,
                    *, n_tiles, layer):
    i = pl.program_id(1)
    step = pl.program_id(0) * n_tiles + i
    slot = _weight_stream_step(sched_ref, 2 * n_tiles, step, (wg_hbm, wu_hbm), layer, (wg_buf, wu_buf),
                               stg, sems, cnt, a_ref.shape[1])

    @pl.when(i < te_ref[n_tiles])
    def _():
        h = _unpack_rows(hs_ref[...])
        g = jnp.dot(h, wg_buf[slot], preferred_element_type=F32)
        u = jnp.dot(h, wu_buf[slot], preferred_element_type=F32)
        a_ref[...] = (_silu(g) * u).astype(BF16)

    @pl.when(i >= te_ref[n_tiles])
    def _():
        a_ref[...] = jnp.zeros_like(a_ref)


def _expert_up(tile_table, hs, wg, wu, layer, tme):
    P, Dh = hs.shape
    _, _, D, F = wg.shape
    tn = F // 2 if (F // 2) % LANE == 0 else F
    n_pass = F // tn
    n_tiles = P // tme
    ch = min(W_STREAM_CHUNK_ROWS, D)
    sched = _expert_schedule(tile_table, n_tiles, n_pass)
    live = lambda i, te: jnp.minimum(i, te[n_tiles] - 1)
    return pl.pallas_call(
        functools.partial(_expert_up_body, n_tiles=n_tiles, layer=layer),
        grid_spec=pltpu.PrefetchScalarGridSpec(
            num_scalar_prefetch=2,
            grid=(n_pass, n_tiles),
            in_specs=[
                pl.BlockSpec((tme, Dh), lambda j, i, te, sc: (live(i, te), 0)),
                pl.BlockSpec(memory_space=pl.ANY),
                pl.BlockSpec(memory_space=pl.ANY),
            ],
            out_specs=pl.BlockSpec((tme, tn), lambda j, i, te, sc: (i, j)),
            scratch_shapes=[pltpu.VMEM((2, D, tn), BF16), pltpu.VMEM((2, D, tn), BF16),
                            pltpu.VMEM((2, ch, tn), F32), pltpu.SemaphoreType.DMA((2,)),
                            pltpu.SMEM((2,), I32)],
        ),
        out_shape=jax.ShapeDtypeStruct((P, F), BF16),
        compiler_params=_cp("arbitrary", "arbitrary"),
        name="moe_expert_up",
    )(tile_table, sched, hs, wg, wu)


def _expert_down_body(te_ref, sched_ref, a_ref, w_hbm, y_ref, w_buf, stg, sems, cnt, *, n_tiles, layer):
    i = pl.program_id(0)
    slot = _weight_stream_step(sched_ref, n_tiles, i, (w_hbm,), layer, (w_buf,), stg, sems, cnt,
                               y_ref.shape[1])

    @pl.when(i < te_ref[n_tiles])
    def _():
        y_ref[...] = jnp.dot(a_ref[...], w_buf[slot], preferred_element_type=F32)

    @pl.when(i >= te_ref[n_tiles])
    def _():
        y_ref[...] = jnp.zeros_like(y_ref)


def _expert_down(tile_table, act, wd, layer, tme):
    P, F = act.shape
    D = wd.shape[3]
    n_tiles = P // tme
    ch = min(W_STREAM_CHUNK_ROWS, F)
    sched = _expert_schedule(tile_table, n_tiles, 1)
    return pl.pallas_call(
        functools.partial(_expert_down_body, n_tiles=n_tiles, layer=layer),
        grid_spec=pltpu.PrefetchScalarGridSpec(
            num_scalar_prefetch=2,
            grid=(n_tiles,),
            in_specs=[
                pl.BlockSpec((tme, F), lambda i, te, sc: (jnp.minimum(i, te[n_tiles] - 1), 0)),
                pl.BlockSpec(memory_space=pl.ANY),
            ],
            out_specs=pl.BlockSpec((tme, D), lambda i, te, sc: (i, 0)),
            scratch_shapes=[pltpu.VMEM((2, F, D), BF16), pltpu.VMEM((2, ch, D), F32),
                            pltpu.SemaphoreType.DMA((2,)), pltpu.SMEM((2,), I32)],
        ),
        out_shape=jax.ShapeDtypeStruct((P, D), F32),
        compiler_params=_cp("arbitrary"),
        name="moe_expert_down",
    )(tile_table, sched, act, wd)


def _combine_body(d1_ref, d2_ref, d1n_ref, d2n_ref, mf_ref, x_ref, fn_ref, y_ref, o_ref, ya, yb, sems,
                  *, final_norm):
    i = pl.program_id(0)
    n = x_ref.shape[0]
    slot = i % 2

    def fetch(ia_ref, ib_ref, to):
        def issue(blk, carry):
            for u in range(ROW_DMA_UNROLL):
                r = blk * ROW_DMA_UNROLL + u
                pltpu.make_async_copy(y_ref.at[pl.ds(ia_ref[r], 1)], ya.at[to, pl.ds(r, 1)],
                                      sems.at[to]).start(priority=0)
                pltpu.make_async_copy(y_ref.at[pl.ds(ib_ref[r], 1)], yb.at[to, pl.ds(r, 1)],
                                      sems.at[to]).start(priority=1)
            return carry
        lax.fori_loop(0, n // ROW_DMA_UNROLL, issue, 0)

    @pl.when(i == 0)
    def _():
        fetch(d1_ref, d2_ref, 0)

    @pl.when(i < pl.num_programs(0) - 1)
    def _():
        fetch(d1n_ref, d2n_ref, 1 - slot)

    def drain(blk, carry):
        row = pltpu.make_async_copy(y_ref.at[pl.ds(0, 1)], ya.at[slot, pl.ds(0, 1)], sems.at[slot])
        for _ in range(2 * ROW_DMA_UNROLL):
            row.wait()
        return carry

    lax.fori_loop(0, n // ROW_DMA_UNROLL, drain, 0)

    gates = mf_ref[...]
    eye = lax.broadcasted_iota(I32, (n, n), 0) == lax.broadcasted_iota(I32, (n, n), 1)
    g1 = jnp.sum(jnp.where(eye, gates[0:1, :], 0.0), axis=1, keepdims=True)
    g2 = jnp.sum(jnp.where(eye, gates[1:2, :], 0.0), axis=1, keepdims=True)
    out = x_ref[...] + (g1 * ya[slot] + g2 * yb[slot])
    if final_norm:
        out = _rms(out, fn_ref[...])
    o_ref[...] = out


def _combine(d1, d2, mf, x2, fnorm, y, final_norm):
    T, D = x2.shape
    tm = _tile(T, 256)
    last = T // tm - 1
    cur = lambda i: (i,)
    nxt = lambda i: (jnp.minimum(i + 1, last),)
    return pl.pallas_call(
        functools.partial(_combine_body, final_norm=final_norm),
        grid=(T // tm,),
        in_specs=[
            pl.BlockSpec((tm,), cur, memory_space=pltpu.SMEM),
            pl.BlockSpec((tm,), cur, memory_space=pltpu.SMEM),
            pl.BlockSpec((tm,), nxt, memory_space=pltpu.SMEM),
            pl.BlockSpec((tm,), nxt, memory_space=pltpu.SMEM),
            pl.BlockSpec((N_EXPERTS, tm), lambda i: (0, i)),
            pl.BlockSpec((tm, D), lambda i: (i, 0)),
            pl.BlockSpec((1, D), lambda i: (0, 0)),
            pl.BlockSpec(memory_space=pl.ANY),
        ],
        out_specs=pl.BlockSpec((tm, D), lambda i: (i, 0)),
        out_shape=jax.ShapeDtypeStruct((T, D), F32),
        scratch_shapes=[pltpu.VMEM((2, tm, D), F32), pltpu.VMEM((2, tm, D), F32),
                        pltpu.SemaphoreType.DMA((2,))],
        compiler_params=_cp("arbitrary"),
        name="moe_combine",
    )(d1, d2, d1, d2, mf, x2, fnorm, y)


def _pad_rows16(w_t):
    pad = jnp.zeros((BF16_SUBLANE_TILE - w_t.shape[0], w_t.shape[1]), F32)
    w = jnp.concatenate([w_t, pad], axis=0)
    hi = w.astype(BF16)
    lo = (w - hi.astype(F32)).astype(BF16)
    return hi, lo


def _attention_layer(x2, B, S, gnorm, w_in_all, layer, b_forget, w_out):
    sb0 = 3 * HEADS_W + N_HEADS
    col = lambda part: w_in_all[layer, :, part * HEADS_W:(part + 1) * HEADS_W]
    sbc = lambda part: w_in_all[layer, :, sb0 + part * HEADS_W:sb0 + (part + 1) * HEADS_W]
    w_k = jnp.concatenate([col(1), sbc(1)], axis=1).astype(BF16)
    w_qvt = jnp.concatenate([col(0), sbc(0), col(2), sbc(2)], axis=1).T.astype(BF16)
    wf_hi, wf_lo = _pad_rows16(w_in_all[layer, :, 3 * HEADS_W:sb0].T)
    keys, qvt, f16 = _att_in_proj(x2, gnorm, w_k, w_qvt, wf_hi, wf_lo)
    cb = _forget_cumsum(f16, b_forget.reshape(N_HEADS, 1), B, S)
    oa = _fox_attention(keys, qvt, cb, B, S)
    ob = _sb_attention(keys, qvt, B, S)
    return _att_out_proj(oa, ob, w_out, layer, x2)


def _dense_ffn_layer(x2, gnorm, wg, wu, wd, layer):
    a = _ffn_up(x2, gnorm, wg, wu, layer)
    return _ffn_down(a, wd, layer, x2)


def _conv_layer(x2, S, gnorm, w_in, conv_w, w_out, layer):
    gb, gg = _conv_in(x2, gnorm, w_in, layer)
    return _conv_out(gb, gg, conv_w, w_out, layer, x2, S)


def _moe_layer(x2, gnorm, w_router, wg, wu, wd, layer, fnorm, final_norm):
    T, D = x2.shape
    tme = 512 if T >= 4096 else 128
    n_tiles = (2 * T) // tme + N_EXPERTS
    P = n_tiles * tme

    wr_hi, wr_lo = _pad_rows16(w_router.T)
    hp, mi, mf, cnt = _router(x2, gnorm, wr_hi, wr_lo)

    counts = cnt[:, 0].astype(I32)
    padded = ((counts + tme - 1) // tme) * tme
    ends = jnp.cumsum(padded)
    offs = ends - padded
    d1 = jnp.take(offs, mi[0]) + mi[2]
    d2 = jnp.take(offs, mi[1]) + mi[3]
    tile_start = jnp.arange(n_tiles, dtype=I32) * tme
    tile_expert = jnp.sum((tile_start[:, None] >= ends[None, :]).astype(I32), axis=1)
    tile_expert = jnp.minimum(tile_expert, N_EXPERTS - 1)
    tile_table = jnp.concatenate([tile_expert, (ends[-1:] // tme).astype(I32)])

    pad_info = jnp.concatenate([offs + counts, padded - counts, ends[-1:] // tme]).astype(I32)
    hs = _dispatch(hp, d1, d2, pad_info, P, tme)
    act = _expert_up(tile_table, hs, wg, wu, layer, tme)
    y = _expert_down(tile_table, act, wd, layer, tme)
    return _combine(d1, d2, mf, x2, fnorm, y, final_norm)


def kernel(x, mix_norm, ffn_norm, final_norm, w_in_att, b_forget, w_out_att, w_in_conv, conv_w,
           w_out_conv, w_gate_dense, w_up_dense, w_down_dense, w_router, w_gate_moe, w_up_moe,
           w_down_moe):
    B, S, D = x.shape
    depth = mix_norm.shape[0]
    assert depth % 2 == 0, "the final rmsnorm is fused into the last (routed) layer"
    x2 = x.reshape(B * S, D)
    fnorm = final_norm.reshape(1, D)
    w_out_att, w_in_conv, w_out_conv, w_gate_dense, w_up_dense, w_down_dense = (
        w.astype(BF16) for w in (w_out_att, w_in_conv, w_out_conv, w_gate_dense, w_up_dense, w_down_dense))
    for i in range(depth):
        j = i // 2
        mg = mix_norm[i].reshape(1, D)
        fg = ffn_norm[i].reshape(1, D)
        if i % 2 == 0:
            x2 = _attention_layer(x2, B, S, mg, w_in_att, j, b_forget[j], w_out_att)
            x2 = _dense_ffn_layer(x2, fg, w_gate_dense, w_up_dense, w_down_dense, j)
        else:
            x2 = _conv_layer(x2, S, mg, w_in_conv, conv_w[j], w_out_conv, j)
            x2 = _moe_layer(x2, fg, w_router[j], w_gate_moe, w_up_moe, w_down_moe, j,
                            fnorm, final_norm=(i == depth - 1))
    return x2.reshape(B, S, D)
```

```python
import functools
import math

import jax
import jax.numpy as jnp
from jax import lax
from jax.experimental import pallas as pl
from jax.experimental.pallas import tpu as pltpu

F32 = jnp.float32
BF16 = jnp.bfloat16
I32 = jnp.int32
U32 = jnp.uint32

HEAD_DIM = 128
N_HEADS = 8
HEADS_W = N_HEADS * HEAD_DIM
N_EXPERTS = 8
RMS_EPS = 1e-6
CONV_WIDTH = 3
LOG2E = math.log2(math.e)
QK_SCALE_LOG2 = LOG2E / math.sqrt(HEAD_DIM)

V7X_VMEM_LIMIT_BYTES = 56 * 1024 * 1024
LANE = 128
BF16_SUBLANE_TILE = 16
NEG_INF = float("-inf")

_NT = (((1,), (1,)), ((), ()))


def _cp(*sem):
    return pltpu.CompilerParams(dimension_semantics=sem, vmem_limit_bytes=V7X_VMEM_LIMIT_BYTES)


def _tile(n, pref, unit=LANE):
    if n <= pref:
        return n
    t = (pref // unit) * unit
    while t > unit and n % t:
        t -= unit
    assert n % t == 0, (n, pref)
    return t


def _rms(x, g):
    ms = jnp.mean(x * x, axis=-1, keepdims=True)
    return x * lax.rsqrt(ms + RMS_EPS) * g


def _split_bf16(v):
    hi = v.astype(BF16)
    lo = (v - hi.astype(F32)).astype(BF16)
    return hi, lo


def _log_sigmoid_pair(z):
    sp = jnp.log1p(jnp.exp(-jnp.abs(z)))
    return jnp.minimum(z, 0.0) - sp, -jnp.maximum(z, 0.0) - sp


def _silu(g):
    return g / (1.0 + jnp.exp(-g))


def _skinny_nt(wh_ref, wl_ref, hb, hl, share_pass):
    wh = wh_ref[...]
    if not share_pass:
        out = lax.dot_general(wh, hb, _NT, preferred_element_type=F32)
        out += lax.dot_general(wh, hl, _NT, preferred_element_type=F32)
        return out + lax.dot_general(wl_ref[...], hb, _NT, preferred_element_type=F32)
    rows = wh.shape[0]
    both = lax.dot_general(jnp.concatenate([wh, wl_ref[...]], axis=0), hb, _NT, preferred_element_type=F32)
    return both[:rows] + both[rows:] + lax.dot_general(wh, hl, _NT, preferred_element_type=F32)


def _att_in_body(x_ref, g_ref, w_ref, wv_ref, wfh_ref, wfl_ref, o_ref, vt_ref, f_ref, h_scr, *, q_blocks, n_main):
    j = pl.program_id(1)

    @pl.when(j == 0)
    def _():
        h = _rms(x_ref[...], g_ref[...])
        hb, hl = _split_bf16(h)
        h_scr[...] = hb
        f_ref[...] = _skinny_nt(wfh_ref, wfl_ref, hb, hl, share_pass=False)

    @pl.when(j < n_main)
    def _():
        o_ref[...] = jnp.dot(h_scr[...], w_ref[...], preferred_element_type=F32).astype(BF16)

    @pl.when(j >= n_main)
    def _():
        acc = lax.dot_general(wv_ref[...], h_scr[...], _NT, preferred_element_type=F32)
        is_q = (j - n_main) < 2 * q_blocks
        vt_ref[...] = (acc * jnp.where(is_q, QK_SCALE_LOG2, 1.0)).astype(BF16)


def _att_in_proj(x2, gnorm, w_k, w_qvt, wf_hi, wf_lo):
    T, D = x2.shape
    N = w_k.shape[1]
    NV = w_qvt.shape[0]
    tm = _tile(T, 1024)
    tn = _tile(HEADS_W, 1024)
    n_main = N // tn
    return pl.pallas_call(
        functools.partial(_att_in_body, q_blocks=HEADS_W // tn, n_main=n_main),
        grid=(T // tm, n_main + NV // tn),
        in_specs=[
            pl.BlockSpec((tm, D), lambda i, j: (i, 0)),
            pl.BlockSpec((1, D), lambda i, j: (0, 0)),
            pl.BlockSpec((D, tn), lambda i, j: (0, jnp.minimum(j, n_main - 1))),
            pl.BlockSpec((tn, D), lambda i, j: (jnp.maximum(j - n_main, 0), 0)),
            pl.BlockSpec((BF16_SUBLANE_TILE, D), lambda i, j: (0, 0)),
            pl.BlockSpec((BF16_SUBLANE_TILE, D), lambda i, j: (0, 0)),
        ],
        out_specs=[
            pl.BlockSpec((tm, tn), lambda i, j: (i, jnp.minimum(j, n_main - 1))),
            pl.BlockSpec((tn, tm), lambda i, j: (jnp.maximum(j - n_main, 0), i)),
            pl.BlockSpec((BF16_SUBLANE_TILE, tm), lambda i, j: (0, i)),
        ],
        out_shape=[
            jax.ShapeDtypeStruct((T, N), BF16),
            jax.ShapeDtypeStruct((NV, T), BF16),
            jax.ShapeDtypeStruct((BF16_SUBLANE_TILE, T), F32),
        ],
        scratch_shapes=[pltpu.VMEM((tm, D), BF16)],
        compiler_params=_cp("parallel", "arbitrary"),
        name="att_in_proj",
    )(x2, gnorm, w_k, w_qvt, wf_hi, wf_lo)


def _forget_cumsum_body(f_ref, b_ref, cb_ref):
    z = f_ref[...] + b_ref[...]
    lf, _ = _log_sigmoid_pair(z)
    S = lf.shape[1]
    lane = lax.broadcasted_iota(I32, lf.shape, 1)
    c = lf
    sh = 1
    while sh < S:
        c = c + jnp.where(lane >= sh, pltpu.roll(c, sh, axis=1), 0.0)
        sh *= 2
    c2 = c * LOG2E
    hi = c2.astype(BF16).astype(F32)
    r1 = c2 - hi
    mid = r1.astype(BF16).astype(F32)
    lo = (r1 - mid).astype(BF16).astype(F32)
    pad = jnp.zeros((LANE - 3 * N_HEADS, S), F32)
    cb_ref[...] = jnp.concatenate([hi, mid, lo, pad], axis=0).T.astype(BF16)


def _forget_cumsum(f16, b_col, B, S):
    return pl.pallas_call(
        _forget_cumsum_body,
        grid=(B,),
        in_specs=[
            pl.BlockSpec((N_HEADS, S), lambda b: (0, b)),
            pl.BlockSpec((N_HEADS, 1), lambda b: (0, 0)),
        ],
        out_specs=pl.BlockSpec((S, LANE), lambda b: (b, 0)),
        out_shape=jax.ShapeDtypeStruct((B * S, LANE), BF16),
        compiler_params=_cp("parallel"),
        name="forget_cumsum",
    )(f16, b_col)


def _head_cols(g):
    return slice(g * HEAD_DIM, (g + 1) * HEAD_DIM)


def _fox_body(qt_ref, k_ref, vt_ref, cb_ref, o_ref, sc_a, sc_b, m_scr, l_scr, acc_scr, *, tq, heads):
    hg = pl.program_id(1)
    i = pl.program_id(2)
    sub = lax.broadcasted_iota(I32, (LANE, tq), 0)
    qs = []
    for g in range(heads):
        h = hg * heads + g
        pick = (sub == h) | (sub == N_HEADS + h) | (sub == 2 * N_HEADS + h)
        qs.append(jnp.concatenate([qt_ref[_head_cols(g), :], jnp.where(pick, -1.0, 0.0).astype(BF16)], axis=0))
    key = lax.broadcasted_iota(I32, (tq, tq), 0)
    qry = lax.broadcasted_iota(I32, (tq, tq), 1)

    def scores(kt, sc_scr):
        ks = pl.multiple_of(kt * tq, tq)
        cb = cb_ref[pl.ds(ks, tq), :]
        for g in range(heads):
            k_aug = jnp.concatenate([k_ref[pl.ds(ks, tq), _head_cols(g)], cb], axis=1)
            sc_scr[g] = jnp.dot(k_aug, qs[g], preferred_element_type=F32)

    def finish(kt, sc_scr, masked):
        ks = pl.multiple_of(kt * tq, tq)
        probs = []
        for g in range(heads):
            m = m_scr[g]
            s = sc_scr[g]
            if masked:
                s = jnp.where(key <= qry, s, NEG_INF)
            m_new = jnp.maximum(m, jnp.max(s, axis=0, keepdims=True))
            alpha = jnp.exp2(m - m_new)
            p = jnp.exp2(s - m_new)
            m_scr[g] = m_new
            l_scr[g] = alpha * l_scr[g] + jnp.sum(p, axis=0, keepdims=True)
            probs.append((alpha, p.astype(BF16)))
        for g in range(heads):
            alpha, p = probs[g]
            vt = vt_ref[_head_cols(g), pl.ds(ks, tq)]
            acc_scr[g] = alpha * acc_scr[g] + jnp.dot(vt, p, preferred_element_type=F32)

    def pair(k2, carry):
        scores(2 * k2 + 1, sc_b)
        finish(2 * k2, sc_a, False)
        scores(2 * k2 + 2, sc_a)
        finish(2 * k2 + 1, sc_b, False)
        return carry

    m_scr[...] = jnp.full(m_scr.shape, NEG_INF, F32)
    l_scr[...] = jnp.zeros(l_scr.shape, F32)
    acc_scr[...] = jnp.zeros(acc_scr.shape, F32)
    scores(0, sc_a)
    lax.fori_loop(0, i // 2, pair, 0)

    @pl.when(i % 2 == 0)
    def _():
        finish(i, sc_a, True)

    @pl.when(i % 2 == 1)
    def _():
        scores(i, sc_b)
        finish(i - 1, sc_a, False)
        finish(i, sc_b, True)

    for g in range(heads):
        o_ref[:, _head_cols(g)] = (acc_scr[g] / l_scr[g]).T.astype(BF16)


ATT_TILE = 256
ATT_HEADS_PER_STEP = 8


def _fox_attention(keys, qvt, cb, B, S):
    T = B * S
    tq = _tile(S, ATT_TILE)
    nq = S // tq
    hp = ATT_HEADS_PER_STEP
    ng = N_HEADS // hp
    w = hp * HEAD_DIM
    return pl.pallas_call(
        functools.partial(_fox_body, tq=tq, heads=hp),
        grid=(B, ng, nq),
        in_specs=[
            pl.BlockSpec((w, tq), lambda b, h, i: (h, b * nq + i)),
            pl.BlockSpec((S, w), lambda b, h, i: (b, h)),
            pl.BlockSpec((w, S), lambda b, h, i: (2 * ng + h, b)),
            pl.BlockSpec((S, LANE), lambda b, h, i: (b, 0)),
        ],
        out_specs=pl.BlockSpec((tq, w), lambda b, h, i: (b * nq + i, h)),
        out_shape=jax.ShapeDtypeStruct((T, HEADS_W), BF16),
        scratch_shapes=[pltpu.VMEM((hp, tq, tq), F32), pltpu.VMEM((hp, tq, tq), F32),
                        pltpu.VMEM((hp, 1, tq), F32), pltpu.VMEM((hp, 1, tq), F32),
                        pltpu.VMEM((hp, HEAD_DIM, tq), F32)],
        compiler_params=_cp("parallel", "parallel", "arbitrary"),
        name="fox_attention",
    )(qvt, keys, qvt, cb)


F32_EXP2_UNDERFLOW = -150.0


def _sb_body(qt_ref, k_ref, vt_ref, ut_ref, o_ref, r_scr, acc_scr, *, tq, heads):
    i = pl.program_id(2)
    qs = [qt_ref[_head_cols(g), :] for g in range(heads)]
    ut = ut_ref[...]
    key = lax.broadcasted_iota(I32, (tq, tq), 0)
    qry = lax.broadcasted_iota(I32, (tq, tq), 1)
    strict = key < qry

    def tile(kt, masked):
        ks = pl.multiple_of(kt * tq, tq)
        zs = [jnp.dot(k_ref[pl.ds(ks, tq), _head_cols(g)], qs[g], preferred_element_type=F32)
              for g in range(heads)]
        mid = []
        for g in range(heads):
            z = zs[g]
            log_beta = jnp.minimum(z, 0.0) - jnp.log2(1.0 + jnp.exp2(-jnp.abs(z)))
            log_om = log_beta - z
            if masked:
                log_om = jnp.where(strict, log_om, 0.0)
            hi, lo = _split_bf16(log_om)
            e = jnp.dot(ut, jnp.concatenate([hi, lo], axis=0), preferred_element_type=F32)
            mid.append((log_beta, log_om, e))
        for g in range(heads):
            log_beta, log_om, e = mid[g]
            r_sum = r_scr[g]
            a = jnp.exp2(log_beta + e + r_sum)
            if masked:
                a = jnp.where(strict, a, 0.0)
            vt = vt_ref[_head_cols(g), pl.ds(ks, tq)]
            acc_scr[g] += jnp.dot(vt, a.astype(BF16), preferred_element_type=F32)
            r_scr[g] = r_sum + jnp.sum(log_om, axis=0, keepdims=True)

    def live():
        return (jnp.max(r_scr[...]) > F32_EXP2_UNDERFLOW).astype(I32)

    r_scr[...] = jnp.zeros(r_scr.shape, F32)
    acc_scr[...] = jnp.zeros(acc_scr.shape, F32)
    tile(i, True)

    def step(state):
        n, _ = state
        tile(i - 1 - n, False)
        return n + 1, live()

    lax.while_loop(lambda st: (st[0] < i) & (st[1] > 0), step, (jnp.int32(0), live()))
    for g in range(heads):
        o_ref[:, _head_cols(g)] = acc_scr[g].T.astype(BF16)


def _sb_attention(keys, qvt, B, S):
    T = B * S
    tq = _tile(S, ATT_TILE)
    nq = S // tq
    hp = ATT_HEADS_PER_STEP
    ng = N_HEADS // hp
    w = hp * HEAD_DIM
    r = lax.broadcasted_iota(I32, (tq, tq), 0)
    c = lax.broadcasted_iota(I32, (tq, tq), 1)
    ut = (c > r).astype(BF16)
    ut = jnp.concatenate([ut, ut], axis=1)
    return pl.pallas_call(
        functools.partial(_sb_body, tq=tq, heads=hp),
        grid=(B, ng, nq),
        in_specs=[
            pl.BlockSpec((w, tq), lambda b, h, i: (ng + h, b * nq + i)),
            pl.BlockSpec((S, w), lambda b, h, i: (b, ng + h)),
            pl.BlockSpec((w, S), lambda b, h, i: (3 * ng + h, b)),
            pl.BlockSpec((tq, 2 * tq), lambda b, h, i: (0, 0)),
        ],
        out_specs=pl.BlockSpec((tq, w), lambda b, h, i: (b * nq + i, h)),
        out_shape=jax.ShapeDtypeStruct((T, HEADS_W), BF16),
        scratch_shapes=[pltpu.VMEM((hp, 1, tq), F32), pltpu.VMEM((hp, HEAD_DIM, tq), F32)],
        compiler_params=_cp("parallel", "parallel", "arbitrary"),
        name="sb_attention",
    )(qvt, keys, qvt, ut)


def _att_out_body(oa_ref, ob_ref, wa_ref, wb_ref, x_ref, o_ref):
    acc = jnp.dot(oa_ref[...], wa_ref[...], preferred_element_type=F32)
    acc += jnp.dot(ob_ref[...], wb_ref[...], preferred_element_type=F32)
    o_ref[...] = x_ref[...] + acc


def _att_out_proj(oa, ob, w_out, layer, x2):
    T, D = x2.shape
    tm = _tile(T, 512)
    return pl.pallas_call(
        _att_out_body,
        grid=(T // tm,),
        in_specs=[
            pl.BlockSpec((tm, HEADS_W), lambda i: (i, 0)),
            pl.BlockSpec((tm, HEADS_W), lambda i: (i, 0)),
            pl.BlockSpec((None, HEADS_W, D), lambda i: (layer, 0, 0)),
            pl.BlockSpec((None, HEADS_W, D), lambda i: (layer, 1, 0)),
            pl.BlockSpec((tm, D), lambda i: (i, 0)),
        ],
        out_specs=pl.BlockSpec((tm, D), lambda i: (i, 0)),
        out_shape=jax.ShapeDtypeStruct((T, D), F32),
        compiler_params=_cp("parallel"),
        name="att_out_proj",
    )(oa, ob, w_out, w_out, x2)


def _ffn_up_body(x_ref, g_ref, wg_ref, wu_ref, a_ref, h_scr):
    @pl.when(pl.program_id(1) == 0)
    def _():
        h_scr[...] = _rms(x_ref[...], g_ref[...]).astype(BF16)

    h = h_scr[...]
    g = jnp.dot(h, wg_ref[...], preferred_element_type=F32)
    u = jnp.dot(h, wu_ref[...], preferred_element_type=F32)
    a_ref[...] = (_silu(g) * u).astype(BF16)


def _ffn_up(x2, gnorm, wg, wu, layer):
    T, D = x2.shape
    F = wg.shape[2]
    tm = _tile(T, 1024)
    tn = _tile(F, 512)
    return pl.pallas_call(
        _ffn_up_body,
        grid=(T // tm, F // tn),
        in_specs=[
            pl.BlockSpec((tm, D), lambda i, j: (i, 0)),
            pl.BlockSpec((1, D), lambda i, j: (0, 0)),
            pl.BlockSpec((None, D, tn), lambda i, j: (layer, 0, j)),
            pl.BlockSpec((None, D, tn), lambda i, j: (layer, 0, j)),
        ],
        out_specs=pl.BlockSpec((tm, tn), lambda i, j: (i, j)),
        out_shape=jax.ShapeDtypeStruct((T, F), BF16),
        scratch_shapes=[pltpu.VMEM((tm, D), BF16)],
        compiler_params=_cp("parallel", "arbitrary"),
        name="ffn_up",
    )(x2, gnorm, wg, wu)


def _ffn_down_body(a_ref, w_ref, x_ref, o_ref):
    o_ref[...] = x_ref[...] + jnp.dot(a_ref[...], w_ref[...], preferred_element_type=F32)


def _ffn_down(a, wd, layer, x2):
    T, D = x2.shape
    F = a.shape[1]
    tm = _tile(T, 1024)
    tn = _tile(D, 512)
    return pl.pallas_call(
        _ffn_down_body,
        grid=(T // tm, D // tn),
        in_specs=[
            pl.BlockSpec((tm, F), lambda i, j: (i, 0)),
            pl.BlockSpec((None, F, tn), lambda i, j: (layer, 0, j)),
            pl.BlockSpec((tm, tn), lambda i, j: (i, j)),
        ],
        out_specs=pl.BlockSpec((tm, tn), lambda i, j: (i, j)),
        out_shape=jax.ShapeDtypeStruct((T, D), F32),
        compiler_params=_cp("parallel", "parallel"),
        name="ffn_down",
    )(a, wd, x2)


def _conv_in_body(x_ref, g_ref, wb_ref, wc_ref, wu_ref, gb_ref, gg_ref, h_scr):
    @pl.when(pl.program_id(1) == 0)
    def _():
        h_scr[...] = _rms(x_ref[...], g_ref[...]).astype(BF16)

    h = h_scr[...]
    gb_ref[...] = jnp.dot(h, wb_ref[...], preferred_element_type=F32).astype(BF16)
    c = jnp.dot(h, wc_ref[...], preferred_element_type=F32)
    u = jnp.dot(h, wu_ref[...], preferred_element_type=F32)
    gg_ref[...] = (c * u).astype(BF16)


def _conv_in(x2, gnorm, w_in, layer):
    T, D = x2.shape
    tm = _tile(T, 1024)
    tn = _tile(D, 512)
    nd = D // tn
    return pl.pallas_call(
        _conv_in_body,
        grid=(T // tm, nd),
        in_specs=[
            pl.BlockSpec((tm, D), lambda i, j: (i, 0)),
            pl.BlockSpec((1, D), lambda i, j: (0, 0)),
            pl.BlockSpec((None, D, tn), lambda i, j: (layer, 0, j)),
            pl.BlockSpec((None, D, tn), lambda i, j: (layer, 0, nd + j)),
            pl.BlockSpec((None, D, tn), lambda i, j: (layer, 0, 2 * nd + j)),
        ],
        out_specs=[
            pl.BlockSpec((tm, tn), lambda i, j: (i, j)),
            pl.BlockSpec((tm, tn), lambda i, j: (i, j)),
        ],
        out_shape=[jax.ShapeDtypeStruct((T, D), BF16), jax.ShapeDtypeStruct((T, D), BF16)],
        scratch_shapes=[pltpu.VMEM((tm, D), BF16)],
        compiler_params=_cp("parallel", "arbitrary"),
        name="conv_in",
    )(x2, gnorm, w_in, w_in, w_in)


CONV_CHANNEL_CHUNK = 512


def _conv_out_body(gb_ref, g_ref, gp_ref, cw_ref, w_ref, x_ref, o_ref, *, tiles_per_seq):
    i = pl.program_id(0)
    tm, D = g_ref.shape
    keep = jnp.where(i % tiles_per_seq == 0, 0.0, 1.0)
    acc = x_ref[...]
    ck = _tile(D, CONV_CHANNEL_CHUNK)
    row = lax.broadcasted_iota(I32, (tm, ck), 0)
    for c0 in range(0, D, ck):
        cols = slice(c0, c0 + ck)
        g = g_ref[:, cols].astype(F32)
        prev = gp_ref[:, cols].astype(F32) * keep
        p1 = prev[BF16_SUBLANE_TILE - 1:BF16_SUBLANE_TILE, :]
        p2 = prev[BF16_SUBLANE_TILE - 2:BF16_SUBLANE_TILE - 1, :]
        g1 = jnp.where(row == 0, p1, pltpu.roll(g, 1, axis=0))
        g2 = jnp.where(row == 0, p2, jnp.where(row == 1, p1, pltpu.roll(g, 2, axis=0)))
        cw = cw_ref[:, cols]
        conv = g2 * cw[0:1, :] + g1 * cw[1:2, :] + g * cw[2:3, :]
        y = (gb_ref[:, cols].astype(F32) * conv).astype(BF16)
        acc = acc + jnp.dot(y, w_ref[cols, :], preferred_element_type=F32)
    o_ref[...] = acc


def _conv_out(gb, gg, conv_w, w_out, layer, x2, S):
    T, D = x2.shape
    tm = _tile(S, 256)
    pt = BF16_SUBLANE_TILE
    return pl.pallas_call(
        functools.partial(_conv_out_body, tiles_per_seq=S // tm),
        grid=(T // tm,),
        in_specs=[
            pl.BlockSpec((tm, D), lambda i: (i, 0)),
            pl.BlockSpec((tm, D), lambda i: (i, 0)),
            pl.BlockSpec((pt, D), lambda i: (jnp.maximum(i * (tm // pt) - 1, 0), 0)),
            pl.BlockSpec((CONV_WIDTH, D), lambda i: (0, 0)),
            pl.BlockSpec((None, D, D), lambda i: (layer, 0, 0)),
            pl.BlockSpec((tm, D), lambda i: (i, 0)),
        ],
        out_specs=pl.BlockSpec((tm, D), lambda i: (i, 0)),
        out_shape=jax.ShapeDtypeStruct((T, D), F32),
        compiler_params=_cp("parallel"),
        name="conv_out",
    )(gb, gg, gg, conv_w, w_out, x2)


def _router_body(x_ref, g_ref, wrh_ref, wrl_ref, tri_ref, hp_ref, mi_ref, mf_ref, cnt_ref, carry):
    @pl.when(pl.program_id(0) == 0)
    def _():
        carry[...] = jnp.zeros_like(carry)

    h = _rms(x_ref[...], g_ref[...])
    hb, hl = _split_bf16(h)
    bits = pltpu.bitcast(hb.astype(F32), U32)
    half = bits.shape[1] // 2
    hp_ref[...] = (bits[:, :half] >> 16) | bits[:, half:]

    logits = _skinny_nt(wrh_ref, wrl_ref, hb, hl, share_pass=True)[:N_EXPERTS]
    eidx = lax.broadcasted_iota(I32, logits.shape, 0).astype(F32)
    ne = float(N_EXPERTS)
    m1 = jnp.max(logits, axis=0, keepdims=True)
    i1 = jnp.min(jnp.where(logits == m1, eidx, ne), axis=0, keepdims=True)
    rest = jnp.where(eidx == i1, NEG_INF, logits)
    m2 = jnp.max(rest, axis=0, keepdims=True)
    i2 = jnp.min(jnp.where(rest == m2, eidx, ne), axis=0, keepdims=True)
    e21 = jnp.exp(m2 - m1)
    g1 = 1.0 / (1.0 + e21)
    g2 = e21 * g1

    sel = jnp.where((eidx == i1) | (eidx == i2), 1.0, 0.0)
    incl = jnp.dot(sel, tri_ref[...], preferred_element_type=F32)
    pos = carry[...] + incl - sel
    carry[...] = carry[...] + jnp.sum(sel, axis=1, keepdims=True)
    p1 = jnp.sum(jnp.where(eidx == i1, pos, 0.0), axis=0, keepdims=True)
    p2 = jnp.sum(jnp.where(eidx == i2, pos, 0.0), axis=0, keepdims=True)
    meta = jnp.where(eidx == 0, i1, jnp.where(eidx == 1, i2, jnp.where(eidx == 2, p1, jnp.where(eidx == 3, p2, 0.0))))
    mi_ref[...] = meta.astype(I32)
    mf_ref[...] = jnp.where(eidx == 0, g1, jnp.where(eidx == 1, g2, 0.0))
    cnt_ref[...] = jnp.broadcast_to(carry[...], cnt_ref.shape)


def _router(x2, gnorm, wr_hi, wr_lo):
    T, D = x2.shape
    tm = _tile(T, 512)
    r = lax.broadcasted_iota(I32, (tm, tm), 0)
    c = lax.broadcasted_iota(I32, (tm, tm), 1)
    tri = (r <= c).astype(F32)
    return pl.pallas_call(
        _router_body,
        grid=(T // tm,),
        in_specs=[
            pl.BlockSpec((tm, D), lambda i: (i, 0)),
            pl.BlockSpec((1, D), lambda i: (0, 0)),
            pl.BlockSpec((BF16_SUBLANE_TILE, D), lambda i: (0, 0)),
            pl.BlockSpec((BF16_SUBLANE_TILE, D), lambda i: (0, 0)),
            pl.BlockSpec((tm, tm), lambda i: (0, 0)),
        ],
        out_specs=[
            pl.BlockSpec((tm, D // 2), lambda i: (i, 0)),
            pl.BlockSpec((N_EXPERTS, tm), lambda i: (0, i)),
            pl.BlockSpec((N_EXPERTS, tm), lambda i: (0, i)),
            pl.BlockSpec((N_EXPERTS, LANE), lambda i: (0, 0)),
        ],
        out_shape=[
            jax.ShapeDtypeStruct((T, D // 2), U32),
            jax.ShapeDtypeStruct((N_EXPERTS, T), I32),
            jax.ShapeDtypeStruct((N_EXPERTS, T), F32),
            jax.ShapeDtypeStruct((N_EXPERTS, LANE), F32),
        ],
        scratch_shapes=[pltpu.VMEM((N_EXPERTS, 1), F32)],
        compiler_params=_cp("arbitrary"),
        name="moe_router",
    )(x2, gnorm, wr_hi, wr_lo, tri)


ROW_DMA_UNROLL = 8


def _dispatch_body(d1_ref, d2_ref, pad_ref, hp_ref, hs_ref, stage, ztile, sems, zsem, *, tme):
    i = pl.program_id(0)
    n = hp_ref.shape[0]
    slot = i % 2
    stage[slot] = hp_ref[...]

    def issue(blk, carry):
        for u in range(ROW_DMA_UNROLL):
            r = blk * ROW_DMA_UNROLL + u
            src = stage.at[slot, pl.ds(r, 1)]
            pltpu.make_async_copy(src, hs_ref.at[pl.ds(d1_ref[r], 1)], sems.at[slot]).start(priority=0)
            pltpu.make_async_copy(src, hs_ref.at[pl.ds(d2_ref[r], 1)], sems.at[slot]).start(priority=1)
        return carry

    def drain(which):
        def body(blk, carry):
            row = pltpu.make_async_copy(stage.at[which, pl.ds(0, 1)], hs_ref.at[pl.ds(0, 1)], sems.at[which])
            for _ in range(2 * ROW_DMA_UNROLL):
                row.wait()
            return carry
        lax.fori_loop(0, n // ROW_DMA_UNROLL, body, 0)

    lax.fori_loop(0, n // ROW_DMA_UNROLL, issue, 0)

    @pl.when(i > 0)
    def _():
        drain(1 - slot)

    def zero_fill(begin):
        for e in range(N_EXPERTS):
            first, count = pad_ref[e], pad_ref[N_EXPERTS + e]

            def row(r, carry):
                cp = pltpu.make_async_copy(ztile.at[pl.ds(0, 1)], hs_ref.at[pl.ds(first + r, 1)], zsem)
                cp.start() if begin else cp.wait()
                return carry

            lax.fori_loop(0, count, row, 0)

        def tile(t, carry):
            cp = pltpu.make_async_copy(ztile, hs_ref.at[pl.ds(pl.multiple_of(t * tme, tme), tme)], zsem)
            cp.start() if begin else cp.wait()
            return carry

        lax.fori_loop(pad_ref[2 * N_EXPERTS], hs_ref.shape[0] // tme, tile, 0)

    @pl.when(i == 0)
    def _():
        ztile[...] = jnp.zeros_like(ztile)
        zero_fill(True)

    @pl.when(i == pl.num_programs(0) - 1)
    def _():
        drain(slot)
        zero_fill(False)


def _dispatch(hp, d1, d2, pad_info, P, tme):
    T, Dh = hp.shape
    tm = _tile(T, 256)
    return pl.pallas_call(
        functools.partial(_dispatch_body, tme=tme),
        grid=(T // tm,),
        in_specs=[
            pl.BlockSpec((tm,), lambda i: (i,), memory_space=pltpu.SMEM),
            pl.BlockSpec((tm,), lambda i: (i,), memory_space=pltpu.SMEM),
            pl.BlockSpec(memory_space=pltpu.SMEM),
            pl.BlockSpec((tm, Dh), lambda i: (i, 0)),
        ],
        out_specs=pl.BlockSpec(memory_space=pl.ANY),
        out_shape=jax.ShapeDtypeStruct((P, Dh), U32),
        scratch_shapes=[pltpu.VMEM((2, tm, Dh), U32), pltpu.VMEM((tme, Dh), U32),
                        pltpu.SemaphoreType.DMA((2,)), pltpu.SemaphoreType.DMA(())],
        compiler_params=_cp("arbitrary"),
        name="moe_dispatch",
    )(d1, d2, pad_info, hp)


def _unpack_rows(words):
    lo = pltpu.bitcast(words << 16, F32)
    hi = pltpu.bitcast(words & jnp.uint32(0xFFFF0000), F32)
    return jnp.concatenate([lo, hi], axis=1).astype(BF16)


W_STREAM_CHUNK_ROWS = 256
W_STREAM_CHUNKS_PER_STEP = 2


def _expert_schedule(tile_table, n_tiles, n_pass):
    n_steps = n_pass * n_tiles
    s = jnp.arange(n_steps, dtype=I32)
    jj = s // n_tiles
    ii = jnp.minimum(s % n_tiles, tile_table[n_tiles] - 1)
    ee = jnp.take(tile_table, ii)
    key = jj * N_EXPERTS + ee
    first = jnp.concatenate([jnp.ones((1,), I32), (key[1:] != key[:-1]).astype(I32)])
    slot = (jnp.cumsum(first) - 1) % 2
    starts = jnp.where(first == 1, s, n_steps)
    nxt = jnp.concatenate([lax.cummin(starts, reverse=True)[1:], jnp.full((1,), n_steps, I32)])
    nxt_key = jnp.where(nxt < n_steps, jnp.take(key, jnp.minimum(nxt, n_steps - 1)), -1)
    nxt_e = jnp.where(nxt_key >= 0, nxt_key % N_EXPERTS, -1)
    nxt_j = jnp.maximum(nxt_key, 0) // N_EXPERTS
    return jnp.concatenate([first, slot, ee, jj, nxt_e, nxt_j]).astype(I32)


def _weight_stream_step(sched_ref, n_steps, step, w_hbms, layer, wbufs, stg, sems, cnt, col_block):
    ch = stg.shape[1]
    k_rows = wbufs[0].shape[1]
    nrb = k_rows // ch
    n_chunks = len(w_hbms) * nrb
    first = sched_ref[step]
    slot = sched_ref[n_steps + step]
    cur_e = sched_ref[2 * n_steps + step]
    cur_j = sched_ref[3 * n_steps + step]
    nxt_e = sched_ref[4 * n_steps + step]
    nxt_j = sched_ref[5 * n_steps + step]

    def chunk_copy(a, rb, e, j, c):
        src = w_hbms[a].at[layer, e, pl.ds(rb * ch, ch), pl.ds(j * col_block, col_block)]
        return pltpu.make_async_copy(src, stg.at[c % 2], sems.at[c % 2])

    def start(c, e, j):
        for a in range(len(w_hbms)):
            @pl.when(c // nrb == a)
            def _():
                chunk_copy(a, c - a * nrb, e, j, c).start()

    def finish(c, to_slot):
        chunk_copy(0, 0, 0, 0, c).wait()
        for a in range(len(w_hbms)):
            @pl.when(c // nrb == a)
            def _():
                r0 = pl.multiple_of((c - a * nrb) * ch, ch)
                wbufs[a][to_slot, pl.ds(r0, ch), :] = stg[c % 2].astype(BF16)

    @pl.when(step == 0)
    def _():
        cnt[0] = 0
        cnt[1] = 0

    @pl.when(first == 1)
    def _():
        started = cnt[1]

        def catch_up(c, carry):
            @pl.when(c >= started)
            def _():
                start(c, cur_e, cur_j)
            finish(c, slot)
            return carry

        lax.fori_loop(cnt[0], n_chunks, catch_up, 0)
        cnt[0] = 0
        cnt[1] = 0

    @pl.when(nxt_e >= 0)
    def _():
        done, started = cnt[0], cnt[1]

        def fin(c, carry):
            finish(c, 1 - slot)
            return carry

        def beg(c, carry):
            start(c, nxt_e, nxt_j)
            return carry

        lax.fori_loop(done, started, fin, 0)
        upto = jnp.minimum(started + W_STREAM_CHUNKS_PER_STEP, n_chunks)
        lax.fori_loop(started, upto, beg, 0)
        cnt[0] = started
        cnt[1] = upto

    return slot


def _expert_up_body(te_ref, sched_ref, hs_ref, wg_hbm, wu_hbm, a_ref, wg_buf, wu_buf, stg, sems, cnt,
                    *, n_tiles, layer):
    i = pl.program_id(1)
    step = pl.program_id(0) * n_tiles + i
    slot = _weight_stream_step(sched_ref, 2 * n_tiles, step, (wg_hbm, wu_hbm), layer, (wg_buf, wu_buf),
                               stg, sems, cnt, a_ref.shape[1])

    @pl.when(i < te_ref[n_tiles])
    def _():
        h = _unpack_rows(hs_ref[...])
        g = jnp.dot(h, wg_buf[slot], preferred_element_type=F32)
        u = jnp.dot(h, wu_buf[slot], preferred_element_type=F32)
        a_ref[...] = (_silu(g) * u).astype(BF16)

    @pl.when(i >= te_ref[n_tiles])
    def _():
        a_ref[...] = jnp.zeros_like(a_ref)


def _expert_up(tile_table, hs, wg, wu, layer, tme):
    P, Dh = hs.shape
    _, _, D, F = wg.shape
    tn = F // 2 if (F // 2) % LANE == 0 else F
    n_pass = F // tn
    n_tiles = P // tme
    ch = min(W_STREAM_CHUNK_ROWS, D)
    sched = _expert_schedule(tile_table, n_tiles, n_pass)
    live = lambda i, te: jnp.minimum(i, te[n_tiles] - 1)
    return pl.pallas_call(
        functools.partial(_expert_up_body, n_tiles=n_tiles, layer=layer),
        grid_spec=pltpu.PrefetchScalarGridSpec(
            num_scalar_prefetch=2,
            grid=(n_pass, n_tiles),
            in_specs=[
                pl.BlockSpec((tme, Dh), lambda j, i, te, sc: (live(i, te), 0)),
                pl.BlockSpec(memory_space=pl.ANY),
                pl.BlockSpec(memory_space=pl.ANY),
            ],
            out_specs=pl.BlockSpec((tme, tn), lambda j, i, te, sc: (i, j)),
            scratch_shapes=[pltpu.VMEM((2, D, tn), BF16), pltpu.VMEM((2, D, tn), BF16),
                            pltpu.VMEM((2, ch, tn), F32), pltpu.SemaphoreType.DMA((2,)),
                            pltpu.SMEM((2,), I32)],
        ),
        out_shape=jax.ShapeDtypeStruct((P, F), BF16),
        compiler_params=_cp("arbitrary", "arbitrary"),
        name="moe_expert_up",
    )(tile_table, sched, hs, wg, wu)


def _expert_down_body(te_ref, sched_ref, a_ref, w_hbm, y_ref, w_buf, stg, sems, cnt, *, n_tiles, layer):
    i = pl.program_id(0)
    slot = _weight_stream_step(sched_ref, n_tiles, i, (w_hbm,), layer, (w_buf,), stg, sems, cnt,
                               y_ref.shape[1])

    @pl.when(i < te_ref[n_tiles])
    def _():
        y_ref[...] = jnp.dot(a_ref[...], w_buf[slot], preferred_element_type=F32)

    @pl.when(i >= te_ref[n_tiles])
    def _():
        y_ref[...] = jnp.zeros_like(y_ref)


def _expert_down(tile_table, act, wd, layer, tme):
    P, F = act.shape
    D = wd.shape[3]
    n_tiles = P // tme
    ch = min(W_STREAM_CHUNK_ROWS, F)
    sched = _expert_schedule(tile_table, n_tiles, 1)
    return pl.pallas_call(
        functools.partial(_expert_down_body, n_tiles=n_tiles, layer=layer),
        grid_spec=pltpu.PrefetchScalarGridSpec(
            num_scalar_prefetch=2,
            grid=(n_tiles,),
            in_specs=[
                pl.BlockSpec((tme, F), lambda i, te, sc: (jnp.minimum(i, te[n_tiles] - 1), 0)),
                pl.BlockSpec(memory_space=pl.ANY),
            ],
            out_specs=pl.BlockSpec((tme, D), lambda i, te, sc: (i, 0)),
            scratch_shapes=[pltpu.VMEM((2, F, D), BF16), pltpu.VMEM((2, ch, D), F32),
                            pltpu.SemaphoreType.DMA((2,)), pltpu.SMEM((2,), I32)],
        ),
        out_shape=jax.ShapeDtypeStruct((P, D), F32),
        compiler_params=_cp("arbitrary"),
        name="moe_expert_down",
    )(tile_table, sched, act, wd)


def _combine_body(d1_ref, d2_ref, d1n_ref, d2n_ref, mf_ref, x_ref, fn_ref, y_ref, o_ref, ya, yb, sems,
                  *, final_norm):
    i = pl.program_id(0)
    n = x_ref.shape[0]
    slot = i % 2

    def fetch(ia_ref, ib_ref, to):
        def issue(blk, carry):
            for u in range(ROW_DMA_UNROLL):
                r = blk * ROW_DMA_UNROLL + u
                pltpu.make_async_copy(y_ref.at[pl.ds(ia_ref[r], 1)], ya.at[to, pl.ds(r, 1)],
                                      sems.at[to]).start(priority=0)
                pltpu.make_async_copy(y_ref.at[pl.ds(ib_ref[r], 1)], yb.at[to, pl.ds(r, 1)],
                                      sems.at[to]).start(priority=1)
            return carry
        lax.fori_loop(0, n // ROW_DMA_UNROLL, issue, 0)

    @pl.when(i == 0)
    def _():
        fetch(d1_ref, d2_ref, 0)

    @pl.when(i < pl.num_programs(0) - 1)
    def _():
        fetch(d1n_ref, d2n_ref, 1 - slot)

    def drain(blk, carry):
        row = pltpu.make_async_copy(y_ref.at[pl.ds(0, 1)], ya.at[slot, pl.ds(0, 1)], sems.at[slot])
        for _ in range(2 * ROW_DMA_UNROLL):
            row.wait()
        return carry

    lax.fori_loop(0, n // ROW_DMA_UNROLL, drain, 0)

    gates = mf_ref[...]
    eye = lax.broadcasted_iota(I32, (n, n), 0) == lax.broadcasted_iota(I32, (n, n), 1)
    g1 = jnp.sum(jnp.where(eye, gates[0:1, :], 0.0), axis=1, keepdims=True)
    g2 = jnp.sum(jnp.where(eye, gates[1:2, :], 0.0), axis=1, keepdims=True)
    out = x_ref[...] + (g1 * ya[slot] + g2 * yb[slot])
    if final_norm:
        out = _rms(out, fn_ref[...])
    o_ref[...] = out


def _combine(d1, d2, mf, x2, fnorm, y, final_norm):
    T, D = x2.shape
    tm = _tile(T, 256)
    last = T // tm - 1
    cur = lambda i: (i,)
    nxt = lambda i: (jnp.minimum(i + 1, last),)
    return pl.pallas_call(
        functools.partial(_combine_body, final_norm=final_norm),
        grid=(T // tm,),
        in_specs=[
            pl.BlockSpec((tm,), cur, memory_space=pltpu.SMEM),
            pl.BlockSpec((tm,), cur, memory_space=pltpu.SMEM),
            pl.BlockSpec((tm,), nxt, memory_space=pltpu.SMEM),
            pl.BlockSpec((tm,), nxt, memory_space=pltpu.SMEM),
            pl.BlockSpec((N_EXPERTS, tm), lambda i: (0, i)),
            pl.BlockSpec((tm, D), lambda i: (i, 0)),
            pl.BlockSpec((1, D), lambda i: (0, 0)),
            pl.BlockSpec(memory_space=pl.ANY),
        ],
        out_specs=pl.BlockSpec((tm, D), lambda i: (i, 0)),
        out_shape=jax.ShapeDtypeStruct((T, D), F32),
        scratch_shapes=[pltpu.VMEM((2, tm, D), F32), pltpu.VMEM((2, tm, D), F32),
                        pltpu.SemaphoreType.DMA((2,))],
        compiler_params=_cp("arbitrary"),
        name="moe_combine",
    )(d1, d2, d1, d2, mf, x2, fnorm, y)


def _pad_rows16(w_t):
    pad = jnp.zeros((BF16_SUBLANE_TILE - w_t.shape[0], w_t.shape[1]), F32)
    w = jnp.concatenate([w_t, pad], axis=0)
    hi = w.astype(BF16)
    lo = (w - hi.astype(F32)).astype(BF16)
    return hi, lo


def _attention_layer(x2, B, S, gnorm, w_in_all, layer, b_forget, w_out):
    sb0 = 3 * HEADS_W + N_HEADS
    col = lambda part: w_in_all[layer, :, part * HEADS_W:(part + 1) * HEADS_W]
    sbc = lambda part: w_in_all[layer, :, sb0 + part * HEADS_W:sb0 + (part + 1) * HEADS_W]
    w_k = jnp.concatenate([col(1), sbc(1)], axis=1).astype(BF16)
    w_qvt = jnp.concatenate([col(0), sbc(0), col(2), sbc(2)], axis=1).T.astype(BF16)
    wf_hi, wf_lo = _pad_rows16(w_in_all[layer, :, 3 * HEADS_W:sb0].T)
    keys, qvt, f16 = _att_in_proj(x2, gnorm, w_k, w_qvt, wf_hi, wf_lo)
    cb = _forget_cumsum(f16, b_forget.reshape(N_HEADS, 1), B, S)
    oa = _fox_attention(keys, qvt, cb, B, S)
    ob = _sb_attention(keys, qvt, B, S)
    return _att_out_proj(oa, ob, w_out, layer, x2)


def _dense_ffn_layer(x2, gnorm, wg, wu, wd, layer):
    a = _ffn_up(x2, gnorm, wg, wu, layer)
    return _ffn_down(a, wd, layer, x2)


def _conv_layer(x2, S, gnorm, w_in, conv_w, w_out, layer):
    gb, gg = _conv_in(x2, gnorm, w_in, layer)
    return _conv_out(gb, gg, conv_w, w_out, layer, x2, S)


def _moe_layer(x2, gnorm, w_router, wg, wu, wd, layer, fnorm, final_norm):
    T, D = x2.shape
    tme = 512 if T >= 4096 else 128
    n_tiles = (2 * T) // tme + N_EXPERTS
    P = n_tiles * tme

    wr_hi, wr_lo = _pad_rows16(w_router.T)
    hp, mi, mf, cnt = _router(x2, gnorm, wr_hi, wr_lo)

    counts = cnt[:, 0].astype(I32)
    padded = ((counts + tme - 1) // tme) * tme
    ends = jnp.cumsum(padded)
    offs = ends - padded
    d1 = jnp.take(offs, mi[0]) + mi[2]
    d2 = jnp.take(offs, mi[1]) + mi[3]
    tile_start = jnp.arange(n_tiles, dtype=I32) * tme
    tile_expert = jnp.sum((tile_start[:, None] >= ends[None, :]).astype(I32), axis=1)
    tile_expert = jnp.minimum(tile_expert, N_EXPERTS - 1)
    tile_table = jnp.concatenate([tile_expert, (ends[-1:] // tme).astype(I32)])

    pad_info = jnp.concatenate([offs + counts, padded - counts, ends[-1:] // tme]).astype(I32)
    hs = _dispatch(hp, d1, d2, pad_info, P, tme)
    act = _expert_up(tile_table, hs, wg, wu, layer, tme)
    y = _expert_down(tile_table, act, wd, layer, tme)
    return _combine(d1, d2, mf, x2, fnorm, y, final_norm)


def kernel(x, mix_norm, ffn_norm, final_norm, w_in_att, b_forget, w_out_att, w_in_conv, conv_w,
           w_out_conv, w_gate_dense, w_up_dense, w_down_dense, w_router, w_gate_moe, w_up_moe,
           w_down_moe):
    B, S, D = x.shape
    depth = mix_norm.shape[0]
    assert depth % 2 == 0, "the final rmsnorm is fused into the last (routed) layer"
    x2 = x.reshape(B * S, D)
    fnorm = final_norm.reshape(1, D)
    w_out_att, w_in_conv, w_out_conv, w_gate_dense, w_up_dense, w_down_dense = (
        w.astype(BF16) for w in (w_out_att, w_in_conv, w_out_conv, w_gate_dense, w_up_dense, w_down_dense))
    for i in range(depth):
        j = i // 2
        mg = mix_norm[i].reshape(1, D)
        fg = ffn_norm[i].reshape(1, D)
        if i % 2 == 0:
            x2 = _attention_layer(x2, B, S, mg, w_in_att, j, b_forget[j], w_out_att)
            x2 = _dense_ffn_layer(x2, fg, w_gate_dense, w_up_dense, w_down_dense, j)
        else:
            x2 = _conv_layer(x2, S, mg, w_in_conv, conv_w[j], w_out_conv, j)
            x2 = _moe_layer(x2, fg, w_router[j], w_gate_moe, w_up_moe, w_down_moe, j,
                            fnorm, final_norm=(i == depth - 1))
    return x2.reshape(B, S, D)
```

```python
import functools
import math

import jax
import jax.numpy as jnp
from jax import lax
from jax.experimental import pallas as pl
from jax.experimental.pallas import tpu as pltpu

F32 = jnp.float32
BF16 = jnp.bfloat16
I32 = jnp.int32
U32 = jnp.uint32

HEAD_DIM = 128
N_HEADS = 8
HEADS_W = N_HEADS * HEAD_DIM
N_EXPERTS = 8
RMS_EPS = 1e-6
CONV_WIDTH = 3
LOG2E = math.log2(math.e)
QK_SCALE_LOG2 = LOG2E / math.sqrt(HEAD_DIM)

V7X_VMEM_LIMIT_BYTES = 56 * 1024 * 1024
LANE = 128
BF16_SUBLANE_TILE = 16
NEG_INF = float("-inf")

_NT = (((1,), (1,)), ((), ()))


def _cp(*sem):
    return pltpu.CompilerParams(dimension_semantics=sem, vmem_limit_bytes=V7X_VMEM_LIMIT_BYTES)


def _tile(n, pref, unit=LANE):
    if n <= pref:
        return n
    t = (pref // unit) * unit
    while t > unit and n % t:
        t -= unit
    assert n % t == 0, (n, pref)
    return t


def _rms(x, g):
    ms = jnp.mean(x * x, axis=-1, keepdims=True)
    return x * lax.rsqrt(ms + RMS_EPS) * g


def _split_bf16(v):
    hi = v.astype(BF16)
    lo = (v - hi.astype(F32)).astype(BF16)
    return hi, lo


def _log_sigmoid_pair(z):
    sp = jnp.log1p(jnp.exp(-jnp.abs(z)))
    return jnp.minimum(z, 0.0) - sp, -jnp.maximum(z, 0.0) - sp


def _silu(g):
    return g / (1.0 + jnp.exp(-g))


def _skinny_nt(wh_ref, wl_ref, hb, hl, share_pass):
    wh = wh_ref[...]
    if not share_pass:
        out = lax.dot_general(wh, hb, _NT, preferred_element_type=F32)
        out += lax.dot_general(wh, hl, _NT, preferred_element_type=F32)
        return out + lax.dot_general(wl_ref[...], hb, _NT, preferred_element_type=F32)
    rows = wh.shape[0]
    both = lax.dot_general(jnp.concatenate([wh, wl_ref[...]], axis=0), hb, _NT, preferred_element_type=F32)
    return both[:rows] + both[rows:] + lax.dot_general(wh, hl, _NT, preferred_element_type=F32)


def _att_in_body(x_ref, g_ref, w_ref, wv_ref, wfh_ref, wfl_ref, o_ref, vt_ref, f_ref, h_scr, *, q_blocks, n_main):
    j = pl.program_id(1)

    @pl.when(j == 0)
    def _():
        h = _rms(x_ref[...], g_ref[...])
        hb, hl = _split_bf16(h)
        h_scr[...] = hb
        f_ref[...] = _skinny_nt(wfh_ref, wfl_ref, hb, hl, share_pass=False)

    @pl.when(j < n_main)
    def _():
        o_ref[...] = jnp.dot(h_scr[...], w_ref[...], preferred_element_type=F32).astype(BF16)

    @pl.when(j >= n_main)
    def _():
        acc = lax.dot_general(wv_ref[...], h_scr[...], _NT, preferred_element_type=F32)
        is_q = (j - n_main) < 2 * q_blocks
        vt_ref[...] = (acc * jnp.where(is_q, QK_SCALE_LOG2, 1.0)).astype(BF16)


def _att_in_proj(x2, gnorm, w_k, w_qvt, wf_hi, wf_lo):
    T, D = x2.shape
    N = w_k.shape[1]
    NV = w_qvt.shape[0]
    tm = _tile(T, 1024)
    tn = _tile(HEADS_W, 1024)
    n_main = N // tn
    return pl.pallas_call(
        functools.partial(_att_in_body, q_blocks=HEADS_W // tn, n_main=n_main),
        grid=(T // tm, n_main + NV // tn),
        in_specs=[
            pl.BlockSpec((tm, D), lambda i, j: (i, 0)),
            pl.BlockSpec((1, D), lambda i, j: (0, 0)),
            pl.BlockSpec((D, tn), lambda i, j: (0, jnp.minimum(j, n_main - 1))),
            pl.BlockSpec((tn, D), lambda i, j: (jnp.maximum(j - n_main, 0), 0)),
            pl.BlockSpec((BF16_SUBLANE_TILE, D), lambda i, j: (0, 0)),
            pl.BlockSpec((BF16_SUBLANE_TILE, D), lambda i, j: (0, 0)),
        ],
        out_specs=[
            pl.BlockSpec((tm, tn), lambda i, j: (i, jnp.minimum(j, n_main - 1))),
            pl.BlockSpec((tn, tm), lambda i, j: (jnp.maximum(j - n_main, 0), i)),
            pl.BlockSpec((BF16_SUBLANE_TILE, tm), lambda i, j: (0, i)),
        ],
        out_shape=[
            jax.ShapeDtypeStruct((T, N), BF16),
            jax.ShapeDtypeStruct((NV, T), BF16),
            jax.ShapeDtypeStruct((BF16_SUBLANE_TILE, T), F32),
        ],
        scratch_shapes=[pltpu.VMEM((tm, D), BF16)],
        compiler_params=_cp("parallel", "arbitrary"),
        name="att_in_proj",
    )(x2, gnorm, w_k, w_qvt, wf_hi, wf_lo)


def _forget_cumsum_body(f_ref, b_ref, cb_ref):
    z = f_ref[...] + b_ref[...]
    lf, _ = _log_sigmoid_pair(z)
    S = lf.shape[1]
    lane = lax.broadcasted_iota(I32, lf.shape, 1)
    c = lf
    sh = 1
    while sh < S:
        c = c + jnp.where(lane >= sh, pltpu.roll(c, sh, axis=1), 0.0)
        sh *= 2
    c2 = c * LOG2E
    hi = c2.astype(BF16).astype(F32)
    r1 = c2 - hi
    mid = r1.astype(BF16).astype(F32)
    lo = (r1 - mid).astype(BF16).astype(F32)
    pad = jnp.zeros((LANE - 3 * N_HEADS, S), F32)
    cb_ref[...] = jnp.concatenate([hi, mid, lo, pad], axis=0).T.astype(BF16)


def _forget_cumsum(f16, b_col, B, S):
    return pl.pallas_call(
        _forget_cumsum_body,
        grid=(B,),
        in_specs=[
            pl.BlockSpec((N_HEADS, S), lambda b: (0, b)),
            pl.BlockSpec((N_HEADS, 1), lambda b: (0, 0)),
        ],
        out_specs=pl.BlockSpec((S, LANE), lambda b: (b, 0)),
        out_shape=jax.ShapeDtypeStruct((B * S, LANE), BF16),
        compiler_params=_cp("parallel"),
        name="forget_cumsum",
    )(f16, b_col)


def _head_cols(g):
    return slice(g * HEAD_DIM, (g + 1) * HEAD_DIM)


def _fox_body(qt_ref, k_ref, vt_ref, cb_ref, o_ref, sc_a, sc_b, m_scr, l_scr, acc_scr, *, tq, heads):
    hg = pl.program_id(1)
    i = pl.program_id(2)
    sub = lax.broadcasted_iota(I32, (LANE, tq), 0)
    qs = []
    for g in range(heads):
        h = hg * heads + g
        pick = (sub == h) | (sub == N_HEADS + h) | (sub == 2 * N_HEADS + h)
        qs.append(jnp.concatenate([qt_ref[_head_cols(g), :], jnp.where(pick, -1.0, 0.0).astype(BF16)], axis=0))
    key = lax.broadcasted_iota(I32, (tq, tq), 0)
    qry = lax.broadcasted_iota(I32, (tq, tq), 1)

    def scores(kt, sc_scr):
        ks = pl.multiple_of(kt * tq, tq)
        cb = cb_ref[pl.ds(ks, tq), :]
        for g in range(heads):
            k_aug = jnp.concatenate([k_ref[pl.ds(ks, tq), _head_cols(g)], cb], axis=1)
            sc_scr[g] = jnp.dot(k_aug, qs[g], preferred_element_type=F32)

    def finish(kt, sc_scr, masked):
        ks = pl.multiple_of(kt * tq, tq)
        probs = []
        for g in range(heads):
            m = m_scr[g]
            s = sc_scr[g]
            if masked:
                s = jnp.where(key <= qry, s, NEG_INF)
            m_new = jnp.maximum(m, jnp.max(s, axis=0, keepdims=True))
            alpha = jnp.exp2(m - m_new)
            p = jnp.exp2(s - m_new)
            m_scr[g] = m_new
            l_scr[g] = alpha * l_scr[g] + jnp.sum(p, axis=0, keepdims=True)
            probs.append((alpha, p.astype(BF16)))
        for g in range(heads):
            alpha, p = probs[g]
            vt = vt_ref[_head_cols(g), pl.ds(ks, tq)]
            acc_scr[g] = alpha * acc_scr[g] + jnp.dot(vt, p, preferred_element_type=F32)

    def pair(k2, carry):
        scores(2 * k2 + 1, sc_b)
        finish(2 * k2, sc_a, False)
        scores(2 * k2 + 2, sc_a)
        finish(2 * k2 + 1, sc_b, False)
        return carry

    m_scr[...] = jnp.full(m_scr.shape, NEG_INF, F32)
    l_scr[...] = jnp.zeros(l_scr.shape, F32)
    acc_scr[...] = jnp.zeros(acc_scr.shape, F32)
    scores(0, sc_a)
    lax.fori_loop(0, i // 2, pair, 0)

    @pl.when(i % 2 == 0)
    def _():
        finish(i, sc_a, True)

    @pl.when(i % 2 == 1)
    def _():
        scores(i, sc_b)
        finish(i - 1, sc_a, False)
        finish(i, sc_b, True)

    for g in range(heads):
        o_ref[:, _head_cols(g)] = (acc_scr[g] / l_scr[g]).T.astype(BF16)


ATT_TILE = 256
ATT_HEADS_PER_STEP = 8


def _fox_attention(keys, qvt, cb, B, S):
    T = B * S
    tq = _tile(S, ATT_TILE)
    nq = S // tq
    hp = ATT_HEADS_PER_STEP
    ng = N_HEADS // hp
    w = hp * HEAD_DIM
    return pl.pallas_call(
        functools.partial(_fox_body, tq=tq, heads=hp),
        grid=(B, ng, nq),
        in_specs=[
            pl.BlockSpec((w, tq), lambda b, h, i: (h, b * nq + i)),
            pl.BlockSpec((S, w), lambda b, h, i: (b, h)),
            pl.BlockSpec((w, S), lambda b, h, i: (2 * ng + h, b)),
            pl.BlockSpec((S, LANE), lambda b, h, i: (b, 0)),
        ],
        out_specs=pl.BlockSpec((tq, w), lambda b, h, i: (b * nq + i, h)),
        out_shape=jax.ShapeDtypeStruct((T, HEADS_W), BF16),
        scratch_shapes=[pltpu.VMEM((hp, tq, tq), F32), pltpu.VMEM((hp, tq, tq), F32),
                        pltpu.VMEM((hp, 1, tq), F32), pltpu.VMEM((hp, 1, tq), F32),
                        pltpu.VMEM((hp, HEAD_DIM, tq), F32)],
        compiler_params=_cp("parallel", "parallel", "arbitrary"),
        name="fox_attention",
    )(qvt, keys, qvt, cb)


F32_EXP2_UNDERFLOW = -150.0


def _sb_body(qt_ref, k_ref, vt_ref, ut_ref, o_ref, r_scr, acc_scr, *, tq, heads):
    i = pl.program_id(2)
    qs = [qt_ref[_head_cols(g), :] for g in range(heads)]
    ut = ut_ref[...]
    key = lax.broadcasted_iota(I32, (tq, tq), 0)
    qry = lax.broadcasted_iota(I32, (tq, tq), 1)
    strict = key < qry

    def tile(kt, masked):
        ks = pl.multiple_of(kt * tq, tq)
        zs = [jnp.dot(k_ref[pl.ds(ks, tq), _head_cols(g)], qs[g], preferred_element_type=F32)
              for g in range(heads)]
        mid = []
        for g in range(heads):
            z = zs[g]
            log_beta = jnp.minimum(z, 0.0) - jnp.log2(1.0 + jnp.exp2(-jnp.abs(z)))
            log_om = log_beta - z
            if masked:
                log_om = jnp.where(strict, log_om, 0.0)
            hi, lo = _split_bf16(log_om)
            e = jnp.dot(ut, jnp.concatenate([hi, lo], axis=0), preferred_element_type=F32)
            mid.append((log_beta, log_om, e))
        for g in range(heads):
            log_beta, log_om, e = mid[g]
            r_sum = r_scr[g]
            a = jnp.exp2(log_beta + e + r_sum)
            if masked:
                a = jnp.where(strict, a, 0.0)
            vt = vt_ref[_head_cols(g), pl.ds(ks, tq)]
            acc_scr[g] += jnp.dot(vt, a.astype(BF16), preferred_element_type=F32)
            r_scr[g] = r_sum + jnp.sum(log_om, axis=0, keepdims=True)

    def live():
        return (jnp.max(r_scr[...]) > F32_EXP2_UNDERFLOW).astype(I32)

    r_scr[...] = jnp.zeros(r_scr.shape, F32)
    acc_scr[...] = jnp.zeros(acc_scr.shape, F32)
    tile(i, True)

    def step(state):
        n, _ = state
        tile(i - 1 - n, False)
        return n + 1, live()

    lax.while_loop(lambda st: (st[0] < i) & (st[1] > 0), step, (jnp.int32(0), live()))
    for g in range(heads):
        o_ref[:, _head_cols(g)] = acc_scr[g].T.astype(BF16)


def _sb_attention(keys, qvt, B, S):
    T = B * S
    tq = _tile(S, ATT_TILE)
    nq = S // tq
    hp = ATT_HEADS_PER_STEP
    ng = N_HEADS // hp
    w = hp * HEAD_DIM
    r = lax.broadcasted_iota(I32, (tq, tq), 0)
    c = lax.broadcasted_iota(I32, (tq, tq), 1)
    ut = (c > r).astype(BF16)
    ut = jnp.concatenate([ut, ut], axis=1)
    return pl.pallas_call(
        functools.partial(_sb_body, tq=tq, heads=hp),
        grid=(B, ng, nq),
        in_specs=[
            pl.BlockSpec((w, tq), lambda b, h, i: (ng + h, b * nq + i)),
            pl.BlockSpec((S, w), lambda b, h, i: (b, ng + h)),
            pl.BlockSpec((w, S), lambda b, h, i: (3 * ng + h, b)),
            pl.BlockSpec((tq, 2 * tq), lambda b, h, i: (0, 0)),
        ],
        out_specs=pl.BlockSpec((tq, w), lambda b, h, i: (b * nq + i, h)),
        out_shape=jax.ShapeDtypeStruct((T, HEADS_W), BF16),
        scratch_shapes=[pltpu.VMEM((hp, 1, tq), F32), pltpu.VMEM((hp, HEAD_DIM, tq), F32)],
        compiler_params=_cp("parallel", "parallel", "arbitrary"),
        name="sb_attention",
    )(qvt, keys, qvt, ut)


def _att_out_body(oa_ref, ob_ref, wa_ref, wb_ref, x_ref, o_ref):
    acc = jnp.dot(oa_ref[...], wa_ref[...], preferred_element_type=F32)
    acc += jnp.dot(ob_ref[...], wb_ref[...], preferred_element_type=F32)
    o_ref[...] = x_ref[...] + acc


def _att_out_proj(oa, ob, w_out, layer, x2):
    T, D = x2.shape
    tm = _tile(T, 512)
    return pl.pallas_call(
        _att_out_body,
        grid=(T // tm,),
        in_specs=[
            pl.BlockSpec((tm, HEADS_W), lambda i: (i, 0)),
            pl.BlockSpec((tm, HEADS_W), lambda i: (i, 0)),
            pl.BlockSpec((None, HEADS_W, D), lambda i: (layer, 0, 0)),
            pl.BlockSpec((None, HEADS_W, D), lambda i: (layer, 1, 0)),
            pl.BlockSpec((tm, D), lambda i: (i, 0)),
        ],
        out_specs=pl.BlockSpec((tm, D), lambda i: (i, 0)),
        out_shape=jax.ShapeDtypeStruct((T, D), F32),
        compiler_params=_cp("parallel"),
        name="att_out_proj",
    )(oa, ob, w_out, w_out, x2)


def _ffn_up_body(x_ref, g_ref, wg_ref, wu_ref, a_ref, h_scr):
    @pl.when(pl.program_id(1) == 0)
    def _():
        h_scr[...] = _rms(x_ref[...], g_ref[...]).astype(BF16)

    h = h_scr[...]
    g = jnp.dot(h, wg_ref[...], preferred_element_type=F32)
    u = jnp.dot(h, wu_ref[...], preferred_element_type=F32)
    a_ref[...] = (_silu(g) * u).astype(BF16)


def _ffn_up(x2, gnorm, wg, wu, layer):
    T, D = x2.shape
    F = wg.shape[2]
    tm = _tile(T, 1024)
    tn = _tile(F, 512)
    return pl.pallas_call(
        _ffn_up_body,
        grid=(T // tm, F // tn),
        in_specs=[
            pl.BlockSpec((tm, D), lambda i, j: (i, 0)),
            pl.BlockSpec((1, D), lambda i, j: (0, 0)),
            pl.BlockSpec((None, D, tn), lambda i, j: (layer, 0, j)),
            pl.BlockSpec((None, D, tn), lambda i, j: (layer, 0, j)),
        ],
        out_specs=pl.BlockSpec((tm, tn), lambda i, j: (i, j)),
        out_shape=jax.ShapeDtypeStruct((T, F), BF16),
        scratch_shapes=[pltpu.VMEM((tm, D), BF16)],
        compiler_params=_cp("parallel", "arbitrary"),
        name="ffn_up",
    )(x2, gnorm, wg, wu)


def _ffn_down_body(a_ref, w_ref, x_ref, o_ref):
    o_ref[...] = x_ref[...] + jnp.dot(a_ref[...], w_ref[...], preferred_element_type=F32)


def _ffn_down(a, wd, layer, x2):
    T, D = x2.shape
    F = a.shape[1]
    tm = _tile(T, 1024)
    tn = _tile(D, 512)
    return pl.pallas_call(
        _ffn_down_body,
        grid=(T // tm, D // tn),
        in_specs=[
            pl.BlockSpec((tm, F), lambda i, j: (i, 0)),
            pl.BlockSpec((None, F, tn), lambda i, j: (layer, 0, j)),
            pl.BlockSpec((tm, tn), lambda i, j: (i, j)),
        ],
        out_specs=pl.BlockSpec((tm, tn), lambda i, j: (i, j)),
        out_shape=jax.ShapeDtypeStruct((T, D), F32),
        compiler_params=_cp("parallel", "parallel"),
        name="ffn_down",
    )(a, wd, x2)


def _conv_in_body(x_ref, g_ref, wb_ref, wc_ref, wu_ref, gb_ref, gg_ref, h_scr):
    @pl.when(pl.program_id(1) == 0)
    def _():
        h_scr[...] = _rms(x_ref[...], g_ref[...]).astype(BF16)

    h = h_scr[...]
    gb_ref[...] = jnp.dot(h, wb_ref[...], preferred_element_type=F32).astype(BF16)
    c = jnp.dot(h, wc_ref[...], preferred_element_type=F32)
    u = jnp.dot(h, wu_ref[...], preferred_element_type=F32)
    gg_ref[...] = (c * u).astype(BF16)


def _conv_in(x2, gnorm, w_in, layer):
    T, D = x2.shape
    tm = _tile(T, 1024)
    tn = _tile(D, 512)
    nd = D // tn
    return pl.pallas_call(
        _conv_in_body,
        grid=(T // tm, nd),
        in_specs=[
            pl.BlockSpec((tm, D), lambda i, j: (i, 0)),
            pl.BlockSpec((1, D), lambda i, j: (0, 0)),
            pl.BlockSpec((None, D, tn), lambda i, j: (layer, 0, j)),
            pl.BlockSpec((None, D, tn), lambda i, j: (layer, 0, nd + j)),
            pl.BlockSpec((None, D, tn), lambda i, j: (layer, 0, 2 * nd + j)),
        ],
        out_specs=[
            pl.BlockSpec((tm, tn), lambda i, j: (i, j)),
            pl.BlockSpec((tm, tn), lambda i, j: (i, j)),
        ],
        out_shape=[jax.ShapeDtypeStruct((T, D), BF16), jax.ShapeDtypeStruct((T, D), BF16)],
        scratch_shapes=[pltpu.VMEM((tm, D), BF16)],
        compiler_params=_cp("parallel", "arbitrary"),
        name="conv_in",
    )(x2, gnorm, w_in, w_in, w_in)


CONV_CHANNEL_CHUNK = 512


def _conv_out_body(gb_ref, g_ref, gp_ref, cw_ref, w_ref, x_ref, o_ref, *, tiles_per_seq):
    i = pl.program_id(0)
    tm, D = g_ref.shape
    keep = jnp.where(i % tiles_per_seq == 0, 0.0, 1.0)
    acc = x_ref[...]
    ck = _tile(D, CONV_CHANNEL_CHUNK)
    row = lax.broadcasted_iota(I32, (tm, ck), 0)
    for c0 in range(0, D, ck):
        cols = slice(c0, c0 + ck)
        g = g_ref[:, cols].astype(F32)
        prev = gp_ref[:, cols].astype(F32) * keep
        p1 = prev[BF16_SUBLANE_TILE - 1:BF16_SUBLANE_TILE, :]
        p2 = prev[BF16_SUBLANE_TILE - 2:BF16_SUBLANE_TILE - 1, :]
        g1 = jnp.where(row == 0, p1, pltpu.roll(g, 1, axis=0))
        g2 = jnp.where(row == 0, p2, jnp.where(row == 1, p1, pltpu.roll(g, 2, axis=0)))
        cw = cw_ref[:, cols]
        conv = g2 * cw[0:1, :] + g1 * cw[1:2, :] + g * cw[2:3, :]
        y = (gb_ref[:, cols].astype(F32) * conv).astype(BF16)
        acc = acc + jnp.dot(y, w_ref[cols, :], preferred_element_type=F32)
    o_ref[...] = acc


def _conv_out(gb, gg, conv_w, w_out, layer, x2, S):
    T, D = x2.shape
    tm = _tile(S, 256)
    pt = BF16_SUBLANE_TILE
    return pl.pallas_call(
        functools.partial(_conv_out_body, tiles_per_seq=S // tm),
        grid=(T // tm,),
        in_specs=[
            pl.BlockSpec((tm, D), lambda i: (i, 0)),
            pl.BlockSpec((tm, D), lambda i: (i, 0)),
            pl.BlockSpec((pt, D), lambda i: (jnp.maximum(i * (tm // pt) - 1, 0), 0)),
            pl.BlockSpec((CONV_WIDTH, D), lambda i: (0, 0)),
            pl.BlockSpec((None, D, D), lambda i: (layer, 0, 0)),
            pl.BlockSpec((tm, D), lambda i: (i, 0)),
        ],
        out_specs=pl.BlockSpec((tm, D), lambda i: (i, 0)),
        out_shape=jax.ShapeDtypeStruct((T, D), F32),
        compiler_params=_cp("parallel"),
        name="conv_out",
    )(gb, gg, gg, conv_w, w_out, x2)


def _router_body(x_ref, g_ref, wrh_ref, wrl_ref, tri_ref, hp_ref, mi_ref, mf_ref, cnt_ref, carry):
    @pl.when(pl.program_id(0) == 0)
    def _():
        carry[...] = jnp.zeros_like(carry)

    h = _rms(x_ref[...], g_ref[...])
    hb, hl = _split_bf16(h)
    bits = pltpu.bitcast(hb.astype(F32), U32)
    half = bits.shape[1] // 2
    hp_ref[...] = (bits[:, :half] >> 16) | bits[:, half:]

    logits = _skinny_nt(wrh_ref, wrl_ref, hb, hl, share_pass=True)[:N_EXPERTS]
    eidx = lax.broadcasted_iota(I32, logits.shape, 0).astype(F32)
    ne = float(N_EXPERTS)
    m1 = jnp.max(logits, axis=0, keepdims=True)
    i1 = jnp.min(jnp.where(logits == m1, eidx, ne), axis=0, keepdims=True)
    rest = jnp.where(eidx == i1, NEG_INF, logits)
    m2 = jnp.max(rest, axis=0, keepdims=True)
    i2 = jnp.min(jnp.where(rest == m2, eidx, ne), axis=0, keepdims=True)
    e21 = jnp.exp(m2 - m1)
    g1 = 1.0 / (1.0 + e21)
    g2 = e21 * g1

    sel = jnp.where((eidx == i1) | (eidx == i2), 1.0, 0.0)
    incl = jnp.dot(sel, tri_ref[...], preferred_element_type=F32)
    pos = carry[...] + incl - sel
    carry[...] = carry[...] + jnp.sum(sel, axis=1, keepdims=True)
    p1 = jnp.sum(jnp.where(eidx == i1, pos, 0.0), axis=0, keepdims=True)
    p2 = jnp.sum(jnp.where(eidx == i2, pos, 0.0), axis=0, keepdims=True)
    meta = jnp.where(eidx == 0, i1, jnp.where(eidx == 1, i2, jnp.where(eidx == 2, p1, jnp.where(eidx == 3, p2, 0.0))))
    mi_ref[...] = meta.astype(I32)
    mf_ref[...] = jnp.where(eidx == 0, g1, jnp.where(eidx == 1, g2, 0.0))
    cnt_ref[...] = jnp.broadcast_to(carry[...], cnt_ref.shape)


def _router(x2, gnorm, wr_hi, wr_lo):
    T, D = x2.shape
    tm = _tile(T, 512)
    r = lax.broadcasted_iota(I32, (tm, tm), 0)
    c = lax.broadcasted_iota(I32, (tm, tm), 1)
    tri = (r <= c).astype(F32)
    return pl.pallas_call(
        _router_body,
        grid=(T // tm,),
        in_specs=[
            pl.BlockSpec((tm, D), lambda i: (i, 0)),
            pl.BlockSpec((1, D), lambda i: (0, 0)),
            pl.BlockSpec((BF16_SUBLANE_TILE, D), lambda i: (0, 0)),
            pl.BlockSpec((BF16_SUBLANE_TILE, D), lambda i: (0, 0)),
            pl.BlockSpec((tm, tm), lambda i: (0, 0)),
        ],
        out_specs=[
            pl.BlockSpec((tm, D // 2), lambda i: (i, 0)),
            pl.BlockSpec((N_EXPERTS, tm), lambda i: (0, i)),
            pl.BlockSpec((N_EXPERTS, tm), lambda i: (0, i)),
            pl.BlockSpec((N_EXPERTS, LANE), lambda i: (0, 0)),
        ],
        out_shape=[
            jax.ShapeDtypeStruct((T, D // 2), U32),
            jax.ShapeDtypeStruct((N_EXPERTS, T), I32),
            jax.ShapeDtypeStruct((N_EXPERTS, T), F32),
            jax.ShapeDtypeStruct((N_EXPERTS, LANE), F32),
        ],
        scratch_shapes=[pltpu.VMEM((N_EXPERTS, 1), F32)],
        compiler_params=_cp("arbitrary"),
        name="moe_router",
    )(x2, gnorm, wr_hi, wr_lo, tri)


ROW_DMA_UNROLL = 8


def _dispatch_body(d1_ref, d2_ref, pad_ref, hp_ref, hs_ref, stage, ztile, sems, zsem, *, tme):
    i = pl.program_id(0)
    n = hp_ref.shape[0]
    slot = i % 2
    stage[slot] = hp_ref[...]

    def issue(blk, carry):
        for u in range(ROW_DMA_UNROLL):
            r = blk * ROW_DMA_UNROLL + u
            src = stage.at[slot, pl.ds(r, 1)]
            pltpu.make_async_copy(src, hs_ref.at[pl.ds(d1_ref[r], 1)], sems.at[slot]).start(priority=0)
            pltpu.make_async_copy(src, hs_ref.at[pl.ds(d2_ref[r], 1)], sems.at[slot]).start(priority=1)
        return carry

    def drain(which):
        def body(blk, carry):
            row = pltpu.make_async_copy(stage.at[which, pl.ds(0, 1)], hs_ref.at[pl.ds(0, 1)], sems.at[which])
            for _ in range(2 * ROW_DMA_UNROLL):
                row.wait()
            return carry
        lax.fori_loop(0, n // ROW_DMA_UNROLL, body, 0)

    lax.fori_loop(0, n // ROW_DMA_UNROLL, issue, 0)

    @pl.when(i > 0)
    def _():
        drain(1 - slot)

    def zero_fill(begin):
        for e in range(N_EXPERTS):
            first, count = pad_ref[e], pad_ref[N_EXPERTS + e]

            def row(r, carry):
                cp = pltpu.make_async_copy(ztile.at[pl.ds(0, 1)], hs_ref.at[pl.ds(first + r, 1)], zsem)
                cp.start() if begin else cp.wait()
                return carry

            lax.fori_loop(0, count, row, 0)

        def tile(t, carry):
            cp = pltpu.make_async_copy(ztile, hs_ref.at[pl.ds(pl.multiple_of(t * tme, tme), tme)], zsem)
            cp.start() if begin else cp.wait()
            return carry

        lax.fori_loop(pad_ref[2 * N_EXPERTS], hs_ref.shape[0] // tme, tile, 0)

    @pl.when(i == 0)
    def _():
        ztile[...] = jnp.zeros_like(ztile)
        zero_fill(True)

    @pl.when(i == pl.num_programs(0) - 1)
    def _():
        drain(slot)
        zero_fill(False)


def _dispatch(hp, d1, d2, pad_info, P, tme):
    T, Dh = hp.shape
    tm = _tile(T, 512)
    return pl.pallas_call(
        functools.partial(_dispatch_body, tme=tme),
        grid=(T // tm,),
        in_specs=[
            pl.BlockSpec((tm,), lambda i: (i,), memory_space=pltpu.SMEM),
            pl.BlockSpec((tm,), lambda i: (i,), memory_space=pltpu.SMEM),
            pl.BlockSpec(memory_space=pltpu.SMEM),
            pl.BlockSpec((tm, Dh), lambda i: (i, 0)),
        ],
        out_specs=pl.BlockSpec(memory_space=pl.ANY),
        out_shape=jax.ShapeDtypeStruct((P, Dh), U32),
        scratch_shapes=[pltpu.VMEM((2, tm, Dh), U32), pltpu.VMEM((tme, Dh), U32),
                        pltpu.SemaphoreType.DMA((2,)), pltpu.SemaphoreType.DMA(())],
        compiler_params=_cp("arbitrary"),
        name="moe_dispatch",
    )(d1, d2, pad_info, hp)


def _unpack_rows(words):
    lo = pltpu.bitcast(words << 16, F32)
    hi = pltpu.bitcast(words & jnp.uint32(0xFFFF0000), F32)
    return jnp.concatenate([lo, hi], axis=1).astype(BF16)


W_STREAM_CHUNK_ROWS = 256
W_STREAM_CHUNKS_PER_STEP = 2


def _expert_schedule(tile_table, n_tiles, n_pass):
    n_steps = n_pass * n_tiles
    s = jnp.arange(n_steps, dtype=I32)
    jj = s // n_tiles
    ii = jnp.minimum(s % n_tiles, tile_table[n_tiles] - 1)
    ee = jnp.take(tile_table, ii)
    key = jj * N_EXPERTS + ee
    first = jnp.concatenate([jnp.ones((1,), I32), (key[1:] != key[:-1]).astype(I32)])
    slot = (jnp.cumsum(first) - 1) % 2
    starts = jnp.where(first == 1, s, n_steps)
    nxt = jnp.concatenate([lax.cummin(starts, reverse=True)[1:], jnp.full((1,), n_steps, I32)])
    nxt_key = jnp.where(nxt < n_steps, jnp.take(key, jnp.minimum(nxt, n_steps - 1)), -1)
    nxt_e = jnp.where(nxt_key >= 0, nxt_key % N_EXPERTS, -1)
    nxt_j = jnp.maximum(nxt_key, 0) // N_EXPERTS
    return jnp.concatenate([first, slot, ee, jj, nxt_e, nxt_j]).astype(I32)


def _weight_stream_step(sched_ref, n_steps, step, w_hbms, layer, wbufs, stg, sems, cnt, col_block):
    ch = stg.shape[1]
    k_rows = wbufs[0].shape[1]
    nrb = k_rows // ch
    n_chunks = len(w_hbms) * nrb
    first = sched_ref[step]
    slot = sched_ref[n_steps + step]
    cur_e = sched_ref[2 * n_steps + step]
    cur_j = sched_ref[3 * n_steps + step]
    nxt_e = sched_ref[4 * n_steps + step]
    nxt_j = sched_ref[5 * n_steps + step]

    def chunk_copy(a, rb, e, j, c):
        src = w_hbms[a].at[layer, e, pl.ds(rb * ch, ch), pl.ds(j * col_block, col_block)]
        return pltpu.make_async_copy(src, stg.at[c % 2], sems.at[c % 2])

    def start(c, e, j):
        for a in range(len(w_hbms)):
            @pl.when(c // nrb == a)
            def _():
                chunk_copy(a, c - a * nrb, e, j, c).start()

    def finish(c, to_slot):
        chunk_copy(0, 0, 0, 0, c).wait()
        for a in range(len(w_hbms)):
            @pl.when(c // nrb == a)
            def _():
                r0 = pl.multiple_of((c - a * nrb) * ch, ch)
                wbufs[a][to_slot, pl.ds(r0, ch), :] = stg[c % 2].astype(BF16)

    @pl.when(step == 0)
    def _():
        cnt[0] = 0
        cnt[1] = 0

    @pl.when(first == 1)
    def _():
        started = cnt[1]

        def catch_up(c, carry):
            @pl.when(c >= started)
            def _():
                start(c, cur_e, cur_j)
            finish(c, slot)
            return carry

        lax.fori_loop(cnt[0], n_chunks, catch_up, 0)
        cnt[0] = 0
        cnt[1] = 0

    @pl.when(nxt_e >= 0)
    def _():
        done, started = cnt[0], cnt[1]

        def fin(c, carry):
            finish(c, 1 - slot)
            return carry

        def beg(c, carry):
            start(c, nxt_e, nxt_j)
            return carry

        lax.fori_loop(done, started, fin, 0)
        upto = jnp.minimum(started + W_STREAM_CHUNKS_PER_STEP, n_chunks)
        lax.fori_loop(started, upto, beg, 0)
        cnt[0] = started
        cnt[1] = upto

    return slot


def _expert_up_body(te_ref, sched_ref, hs_ref, wg_hbm, wu_hbm, a_ref, wg_buf, wu_buf, stg, sems, cnt,
                    *, n_tiles, layer):
    i = pl.program_id(1)
    step = pl.program_id(0) * n_tiles + i
    slot = _weight_stream_step(sched_ref, 2 * n_tiles, step, (wg_hbm, wu_hbm), layer, (wg_buf, wu_buf),
                               stg, sems, cnt, a_ref.shape[1])

    @pl.when(i < te_ref[n_tiles])
    def _():
        h = _unpack_rows(hs_ref[...])
        g = jnp.dot(h, wg_buf[slot], preferred_element_type=F32)
        u = jnp.dot(h, wu_buf[slot], preferred_element_type=F32)
        a_ref[...] = (_silu(g) * u).astype(BF16)

    @pl.when(i >= te_ref[n_tiles])
    def _():
        a_ref[...] = jnp.zeros_like(a_ref)


def _expert_up(tile_table, hs, wg, wu, layer, tme):
    P, Dh = hs.shape
    _, _, D, F = wg.shape
    tn = F // 2 if (F // 2) % LANE == 0 else F
    n_pass = F // tn
    n_tiles = P // tme
    ch = min(W_STREAM_CHUNK_ROWS, D)
    sched = _expert_schedule(tile_table, n_tiles, n_pass)
    live = lambda i, te: jnp.minimum(i, te[n_tiles] - 1)
    return pl.pallas_call(
        functools.partial(_expert_up_body, n_tiles=n_tiles, layer=layer),
        grid_spec=pltpu.PrefetchScalarGridSpec(
            num_scalar_prefetch=2,
            grid=(n_pass, n_tiles),
            in_specs=[
                pl.BlockSpec((tme, Dh), lambda j, i, te, sc: (live(i, te), 0)),
                pl.BlockSpec(memory_space=pl.ANY),
                pl.BlockSpec(memory_space=pl.ANY),
            ],
            out_specs=pl.BlockSpec((tme, tn), lambda j, i, te, sc: (i, j)),
            scratch_shapes=[pltpu.VMEM((2, D, tn), BF16), pltpu.VMEM((2, D, tn), BF16),
                            pltpu.VMEM((2, ch, tn), F32), pltpu.SemaphoreType.DMA((2,)),
                            pltpu.SMEM((2,), I32)],
        ),
        out_shape=jax.ShapeDtypeStruct((P, F), BF16),
        compiler_params=_cp("arbitrary", "arbitrary"),
        name="moe_expert_up",
    )(tile_table, sched, hs, wg, wu)


def _expert_down_body(te_ref, sched_ref, a_ref, w_hbm, y_ref, w_buf, stg, sems, cnt, *, n_tiles, layer):
    i = pl.program_id(0)
    slot = _weight_stream_step(sched_ref, n_tiles, i, (w_hbm,), layer, (w_buf,), stg, sems, cnt,
                               y_ref.shape[1])

    @pl.when(i < te_ref[n_tiles])
    def _():
        y_ref[...] = jnp.dot(a_ref[...], w_buf[slot], preferred_element_type=F32)

    @pl.when(i >= te_ref[n_tiles])
    def _():
        y_ref[...] = jnp.zeros_like(y_ref)


def _expert_down(tile_table, act, wd, layer, tme):
    P, F = act.shape
    D = wd.shape[3]
    n_tiles = P // tme
    ch = min(W_STREAM_CHUNK_ROWS, F)
    sched = _expert_schedule(tile_table, n_tiles, 1)
    return pl.pallas_call(
        functools.partial(_expert_down_body, n_tiles=n_tiles, layer=layer),
        grid_spec=pltpu.PrefetchScalarGridSpec(
            num_scalar_prefetch=2,
            grid=(n_tiles,),
            in_specs=[
                pl.BlockSpec((tme, F), lambda i, te, sc: (jnp.minimum(i, te[n_tiles] - 1), 0)),
                pl.BlockSpec(memory_space=pl.ANY),
            ],
            out_specs=pl.BlockSpec((tme, D), lambda i, te, sc: (i, 0)),
            scratch_shapes=[pltpu.VMEM((2, F, D), BF16), pltpu.VMEM((2, ch, D), F32),
                            pltpu.SemaphoreType.DMA((2,)), pltpu.SMEM((2,), I32)],
        ),
        out_shape=jax.ShapeDtypeStruct((P, D), F32),
        compiler_params=_cp("arbitrary"),
        name="moe_expert_down",
    )(tile_table, sched, act, wd)


def _combine_body(d1_ref, d2_ref, d1n_ref, d2n_ref, mf_ref, x_ref, fn_ref, y_ref, o_ref, ya, yb, sems,
                  *, final_norm):
    i = pl.program_id(0)
    n = x_ref.shape[0]
    slot = i % 2

    def fetch(ia_ref, ib_ref, to):
        def issue(blk, carry):
            for u in range(ROW_DMA_UNROLL):
                r = blk * ROW_DMA_UNROLL + u
                pltpu.make_async_copy(y_ref.at[pl.ds(ia_ref[r], 1)], ya.at[to, pl.ds(r, 1)],
                                      sems.at[to]).start(priority=0)
                pltpu.make_async_copy(y_ref.at[pl.ds(ib_ref[r], 1)], yb.at[to, pl.ds(r, 1)],
                                      sems.at[to]).start(priority=1)
            return carry
        lax.fori_loop(0, n // ROW_DMA_UNROLL, issue, 0)

    @pl.when(i == 0)
    def _():
        fetch(d1_ref, d2_ref, 0)

    @pl.when(i < pl.num_programs(0) - 1)
    def _():
        fetch(d1n_ref, d2n_ref, 1 - slot)

    def drain(blk, carry):
        row = pltpu.make_async_copy(y_ref.at[pl.ds(0, 1)], ya.at[slot, pl.ds(0, 1)], sems.at[slot])
        for _ in range(2 * ROW_DMA_UNROLL):
            row.wait()
        return carry

    lax.fori_loop(0, n // ROW_DMA_UNROLL, drain, 0)

    gates = mf_ref[...]
    eye = lax.broadcasted_iota(I32, (n, n), 0) == lax.broadcasted_iota(I32, (n, n), 1)
    g1 = jnp.sum(jnp.where(eye, gates[0:1, :], 0.0), axis=1, keepdims=True)
    g2 = jnp.sum(jnp.where(eye, gates[1:2, :], 0.0), axis=1, keepdims=True)
    out = x_ref[...] + (g1 * ya[slot] + g2 * yb[slot])
    if final_norm:
        out = _rms(out, fn_ref[...])
    o_ref[...] = out


def _combine(d1, d2, mf, x2, fnorm, y, final_norm):
    T, D = x2.shape
    tm = _tile(T, 512)
    last = T // tm - 1
    cur = lambda i: (i,)
    nxt = lambda i: (jnp.minimum(i + 1, last),)
    return pl.pallas_call(
        functools.partial(_combine_body, final_norm=final_norm),
        grid=(T // tm,),
        in_specs=[
            pl.BlockSpec((tm,), cur, memory_space=pltpu.SMEM),
            pl.BlockSpec((tm,), cur, memory_space=pltpu.SMEM),
            pl.BlockSpec((tm,), nxt, memory_space=pltpu.SMEM),
            pl.BlockSpec((tm,), nxt, memory_space=pltpu.SMEM),
            pl.BlockSpec((N_EXPERTS, tm), lambda i: (0, i)),
            pl.BlockSpec((tm, D), lambda i: (i, 0)),
            pl.BlockSpec((1, D), lambda i: (0, 0)),
            pl.BlockSpec(memory_space=pl.ANY),
        ],
        out_specs=pl.BlockSpec((tm, D), lambda i: (i, 0)),
        out_shape=jax.ShapeDtypeStruct((T, D), F32),
        scratch_shapes=[pltpu.VMEM((2, tm, D), F32), pltpu.VMEM((2, tm, D), F32),
                        pltpu.SemaphoreType.DMA((2,))],
        compiler_params=_cp("arbitrary"),
        name="moe_combine",
    )(d1, d2, d1, d2, mf, x2, fnorm, y)


def _pad_rows16(w_t):
    pad = jnp.zeros((BF16_SUBLANE_TILE - w_t.shape[0], w_t.shape[1]), F32)
    w = jnp.concatenate([w_t, pad], axis=0)
    hi = w.astype(BF16)
    lo = (w - hi.astype(F32)).astype(BF16)
    return hi, lo


def _attention_layer(x2, B, S, gnorm, w_in_all, layer, b_forget, w_out):
    sb0 = 3 * HEADS_W + N_HEADS
    col = lambda part: w_in_all[layer, :, part * HEADS_W:(part + 1) * HEADS_W]
    sbc = lambda part: w_in_all[layer, :, sb0 + part * HEADS_W:sb0 + (part + 1) * HEADS_W]
    w_k = jnp.concatenate([col(1), sbc(1)], axis=1).astype(BF16)
    w_qvt = jnp.concatenate([col(0), sbc(0), col(2), sbc(2)], axis=1).T.astype(BF16)
    wf_hi, wf_lo = _pad_rows16(w_in_all[layer, :, 3 * HEADS_W:sb0].T)
    keys, qvt, f16 = _att_in_proj(x2, gnorm, w_k, w_qvt, wf_hi, wf_lo)
    cb = _forget_cumsum(f16, b_forget.reshape(N_HEADS, 1), B, S)
    oa = _fox_attention(keys, qvt, cb, B, S)
    ob = _sb_attention(keys, qvt, B, S)
    return _att_out_proj(oa, ob, w_out, layer, x2)


def _dense_ffn_layer(x2, gnorm, wg, wu, wd, layer):
    a = _ffn_up(x2, gnorm, wg, wu, layer)
    return _ffn_down(a, wd, layer, x2)


def _conv_layer(x2, S, gnorm, w_in, conv_w, w_out, layer):
    gb, gg = _conv_in(x2, gnorm, w_in, layer)
    return _conv_out(gb, gg, conv_w, w_out, layer, x2, S)


def _moe_layer(x2, gnorm, w_router, wg, wu, wd, layer, fnorm, final_norm):
    T, D = x2.shape
    tme = 512 if T >= 4096 else 128
    n_tiles = (2 * T) // tme + N_EXPERTS
    P = n_tiles * tme

    wr_hi, wr_lo = _pad_rows16(w_router.T)
    hp, mi, mf, cnt = _router(x2, gnorm, wr_hi, wr_lo)

    counts = cnt[:, 0].astype(I32)
    padded = ((counts + tme - 1) // tme) * tme
    ends = jnp.cumsum(padded)
    offs = ends - padded
    d1 = jnp.take(offs, mi[0]) + mi[2]
    d2 = jnp.take(offs, mi[1]) + mi[3]
    tile_start = jnp.arange(n_tiles, dtype=I32) * tme
    tile_expert = jnp.sum((tile_start[:, None] >= ends[None, :]).astype(I32), axis=1)
    tile_expert = jnp.minimum(tile_expert, N_EXPERTS - 1)
    tile_table = jnp.concatenate([tile_expert, (ends[-1:] // tme).astype(I32)])

    pad_info = jnp.concatenate([offs + counts, padded - counts, ends[-1:] // tme]).astype(I32)
    hs = _dispatch(hp, d1, d2, pad_info, P, tme)
    act = _expert_up(tile_table, hs, wg, wu, layer, tme)
    y = _expert_down(tile_table, act, wd, layer, tme)
    return _combine(d1, d2, mf, x2, fnorm, y, final_norm)


def kernel(x, mix_norm, ffn_norm, final_norm, w_in_att, b_forget, w_out_att, w_in_conv, conv_w,
           w_out_conv, w_gate_dense, w_up_dense, w_down_dense, w_router, w_gate_moe, w_up_moe,
           w_down_moe):
    B, S, D = x.shape
    depth = mix_norm.shape[0]
    assert depth % 2 == 0, "the final rmsnorm is fused into the last (routed) layer"
    x2 = x.reshape(B * S, D)
    fnorm = final_norm.reshape(1, D)
    w_out_att, w_in_conv, w_out_conv, w_gate_dense, w_up_dense, w_down_dense = (
        w.astype(BF16) for w in (w_out_att, w_in_conv, w_out_conv, w_gate_dense, w_up_dense, w_down_dense))
    for i in range(depth):
        j = i // 2
        mg = mix_norm[i].reshape(1, D)
        fg = ffn_norm[i].reshape(1, D)
        if i % 2 == 0:
            x2 = _attention_layer(x2, B, S, mg, w_in_att, j, b_forget[j], w_out_att)
            x2 = _dense_ffn_layer(x2, fg, w_gate_dense, w_up_dense, w_down_dense, j)
        else:
            x2 = _conv_layer(x2, S, mg, w_in_conv, conv_w[j], w_out_conv, j)
            x2 = _moe_layer(x2, fg, w_router[j], w_gate_moe, w_up_moe, w_down_moe, j,
                            fnorm, final_norm=(i == depth - 1))
    return x2.reshape(B, S, D)
```
